```python
import math
import jax, jax.numpy as jnp
from jax import lax
import numpy as np

D_MODEL = 2048
BATCH = 4
SEQ = 2048
DEPTH = 2

N_A_LAYERS = DEPTH // 2
N_B_LAYERS = DEPTH - N_A_LAYERS

CONV_WIDTH = 31
CONV_CH = D_MODEL

ATTN_WINDOWS = (128, 512, 2048)
ATTN_DILATIONS = (1, 4, 16)
N_ATTN_GROUPS = 3
HEADS_PER_GROUP = 8
HEAD_DIM = D_MODEL // 16
Q_WIDTH = N_ATTN_GROUPS * HEADS_PER_GROUP * HEAD_DIM
KV_WIDTH = 2 * Q_WIDTH
ATTN_OUT = HEADS_PER_GROUP * HEAD_DIM
BAND = 128

N_EXPERT_GROUPS = 4
EXPERTS_PER_GROUP = 8
N_EXPERTS = N_EXPERT_GROUPS * EXPERTS_PER_GROUP
TOPK_IN_GROUP = 2
D_EXPERT = D_MODEL // 4

EPS = 1e-6

kernel_name = "yoco_conformer_dilated_attn_hmoe"


def rms_norm(x, g):
    x32 = x.astype(jnp.float32)
    y = x32 * lax.rsqrt(jnp.mean(x32 * x32, axis=-1, keepdims=True) + EPS)
    return (y * g.astype(jnp.float32)).astype(x.dtype)


def layer_norm(x, g, b):
    x32 = x.astype(jnp.float32)
    mu = jnp.mean(x32, axis=-1, keepdims=True)
    xc = x32 - mu
    var = jnp.mean(xc * xc, axis=-1, keepdims=True)
    y = xc * lax.rsqrt(var + EPS) * g.astype(jnp.float32) + b.astype(jnp.float32)
    return y.astype(x.dtype)


def conformer_conv(xn, w_in, b_in, w_dw, b_dw, ln_g, ln_b, w_out, b_out):
    u = xn @ w_in + b_in
    val, gate = jnp.split(u, 2, axis=-1)
    u = val * jax.nn.sigmoid(gate)
    u = lax.conv_general_dilated(
        u, w_dw[:, None, :], window_strides=(1,), padding=[(CONV_WIDTH - 1, 0)],
        dimension_numbers=('NWC', 'WIO', 'NWC'), feature_group_count=CONV_CH) + b_dw
    u = jax.nn.silu(layer_norm(u, ln_g, ln_b))
    return u @ w_out + b_out


def dilated_group(q, k, v, window, dilation):
    bsz, seq, nh, dh = q.shape
    span = window // dilation
    sub_len = -(-seq // (dilation * BAND)) * BAND
    seq_pad = sub_len * dilation
    nblk = sub_len // BAND
    pad = [(0, 0), (0, seq_pad - seq), (0, 0), (0, 0)]

    def to_sub(t):
        return jnp.pad(t, pad).reshape(bsz, nblk, BAND, dilation, nh, dh)

    def band(t):
        prev = jnp.pad(t, [(0, 0), (1, 0), (0, 0), (0, 0), (0, 0), (0, 0)])[:, :-1]
        return jnp.concatenate([prev, t], axis=2)

    qb = to_sub(q)
    kk = band(to_sub(k))
    vv = band(to_sub(v))
    scores = jnp.einsum('bnqrhd,bnkrhd->bnrhqk', qb, kk).astype(jnp.float32) * (dh ** -0.5)
    qi = jnp.arange(BAND)[:, None]
    kj = jnp.arange(2 * BAND)[None, :]
    dist = qi + BAND - kj
    blk = jnp.arange(nblk)[:, None, None]
    valid = (dist >= 0) & (dist <= span) & (blk * BAND - BAND + kj >= 0)
    scores = jnp.where(valid[None, :, None, None], scores, -jnp.inf)
    m = jnp.max(scores, axis=-1, keepdims=True)
    p = jnp.exp(scores - m)
    denom = jnp.sum(p, axis=-1, keepdims=True)
    o = jnp.einsum('bnrhqk,bnkrhd->bnqrhd', p / denom, vv.astype(jnp.float32))
    lse = (m + jnp.log(denom))[..., 0]
    o = o.reshape(bsz, seq_pad, nh, dh)[:, :seq]
    lse = lse.transpose(0, 1, 4, 2, 3).reshape(bsz, seq_pad, nh)[:, :seq]
    return o, lse


def dilated_attention(xn, w_q, w_o, k_sh, v_sh):
    bsz, seq, _ = xn.shape
    q = (xn @ w_q).reshape(bsz, seq, N_ATTN_GROUPS, HEADS_PER_GROUP, HEAD_DIM)
    outs = []
    lses = []
    for g in range(N_ATTN_GROUPS):
        o_g, l_g = dilated_group(q[:, :, g], k_sh[:, :, g], v_sh[:, :, g],
                                 ATTN_WINDOWS[g], ATTN_DILATIONS[g])
        outs.append(o_g)
        lses.append(l_g)
    wts = jax.nn.softmax(jnp.stack(lses, axis=0), axis=0)
    o = jnp.sum(wts[..., None] * jnp.stack(outs, axis=0), axis=0)
    return o.reshape(bsz, seq, ATTN_OUT).astype(xn.dtype) @ w_o


def hier_moe(xn, w_group, b_group, w_expert, b_expert, w_gate, w_up, w_down):
    bsz, seq, d = xn.shape
    t = xn.reshape(-1, d)
    group_logits = (t @ w_group).astype(jnp.float32) + b_group.astype(jnp.float32)
    group_probs = jax.nn.softmax(group_logits, axis=-1)
    gp, gi = lax.top_k(group_probs, 1)
    exp_logits = jnp.einsum('nd,gde->nge', t, w_expert).astype(jnp.float32) + b_expert.astype(jnp.float32)
    sel_logits = jnp.take_along_axis(exp_logits, gi[:, :, None], axis=1)[:, 0]
    ev, ei = lax.top_k(sel_logits, TOPK_IN_GROUP)
    ew = jax.nn.softmax(ev, axis=-1) * gp
    eid = gi * EXPERTS_PER_GROUP + ei
    gates = jnp.sum(jax.nn.one_hot(eid, N_EXPERTS, dtype=jnp.float32) * ew[..., None], axis=1)
    hg = jnp.einsum('nd,edf->nef', t, w_gate)
    hu = jnp.einsum('nd,edf->nef', t, w_up)
    act = jax.nn.silu(hg) * hu * gates[:, :, None].astype(t.dtype)
    y = jnp.einsum('nef,efd->nd', act, w_down)
    return y.reshape(bsz, seq, d)


def setup_inputs(seed: int = 0) -> dict:
    key = jax.random.key(seed)
    ks = jax.random.split(key, 24)
    f32 = jnp.float32

    def nrm(k, shape, scale):
        return jax.random.normal(k, shape, f32) * scale

    def gain(k, shape):
        return 1.0 + 0.02 * jax.random.normal(k, shape, f32)

    return {
        "x": jax.random.normal(ks[0], (BATCH, SEQ, D_MODEL), f32),
        "norm_mix_g": gain(ks[1], (DEPTH, D_MODEL)),
        "norm_ffn_g": gain(ks[2], (DEPTH, D_MODEL)),
        "conv_w_in": nrm(ks[3], (N_A_LAYERS, D_MODEL, 2 * CONV_CH), D_MODEL ** -0.5),
        "conv_b_in": nrm(ks[4], (N_A_LAYERS, 2 * CONV_CH), 0.01),
        "conv_w_dw": nrm(ks[5], (N_A_LAYERS, CONV_WIDTH, CONV_CH), CONV_WIDTH ** -0.5),
        "conv_b_dw": nrm(ks[6], (N_A_LAYERS, CONV_CH), 0.01),
        "conv_ln_g": gain(ks[7], (N_A_LAYERS, CONV_CH)),
        "conv_ln_b": nrm(ks[8], (N_A_LAYERS, CONV_CH), 0.01),
        "conv_w_out": nrm(ks[9], (N_A_LAYERS, CONV_CH, D_MODEL), CONV_CH ** -0.5),
        "conv_b_out": nrm(ks[10], (N_A_LAYERS, D_MODEL), 0.01),
        "norm_kv_g": gain(ks[11], (D_MODEL,)),
        "w_kv": nrm(ks[12], (D_MODEL, KV_WIDTH), D_MODEL ** -0.5),
        "attn_w_q": nrm(ks[13], (N_B_LAYERS, D_MODEL, Q_WIDTH), D_MODEL ** -0.5),
        "attn_w_o": nrm(ks[14], (N_B_LAYERS, ATTN_OUT, D_MODEL), ATTN_OUT ** -0.5),
        "router_w_group": nrm(ks[15], (DEPTH, D_MODEL, N_EXPERT_GROUPS), D_MODEL ** -0.5),
        "router_b_group": nrm(ks[16], (DEPTH, N_EXPERT_GROUPS), 0.01),
        "router_w_expert": nrm(ks[17], (DEPTH, N_EXPERT_GROUPS, D_MODEL, EXPERTS_PER_GROUP), D_MODEL ** -0.5),
        "router_b_expert": nrm(ks[18], (DEPTH, N_EXPERT_GROUPS, EXPERTS_PER_GROUP), 0.01),
        "expert_w_gate": nrm(ks[19], (DEPTH, N_EXPERTS, D_MODEL, D_EXPERT), D_MODEL ** -0.5),
        "expert_w_up": nrm(ks[20], (DEPTH, N_EXPERTS, D_MODEL, D_EXPERT), D_MODEL ** -0.5),
        "expert_w_down": nrm(ks[21], (DEPTH, N_EXPERTS, D_EXPERT, D_MODEL), D_EXPERT ** -0.5),
        "norm_final_g": gain(ks[22], (D_MODEL,)),
    }


def reference(x, norm_mix_g, norm_ffn_g, conv_w_in, conv_b_in, conv_w_dw, conv_b_dw,
              conv_ln_g, conv_ln_b, conv_w_out, conv_b_out, norm_kv_g, w_kv,
              attn_w_q, attn_w_o, router_w_group, router_b_group, router_w_expert,
              router_b_expert, expert_w_gate, expert_w_up, expert_w_down, norm_final_g):
    bsz, seq, _ = x.shape
    h = x
    k_sh = None
    v_sh = None
    for layer in range(DEPTH):
        xn = rms_norm(h, norm_mix_g[layer])
        if layer < N_A_LAYERS:
            a = layer
            h = h + conformer_conv(xn, conv_w_in[a], conv_b_in[a], conv_w_dw[a], conv_b_dw[a],
                                   conv_ln_g[a], conv_ln_b[a], conv_w_out[a], conv_b_out[a])
        else:
            if layer == N_A_LAYERS:
                kv = (rms_norm(h, norm_kv_g) @ w_kv).reshape(
                    bsz, seq, 2, N_ATTN_GROUPS, HEADS_PER_GROUP, HEAD_DIM)
                k_sh = kv[:, :, 0]
                v_sh = kv[:, :, 1]
            b = layer - N_A_LAYERS
            h = h + dilated_attention(xn, attn_w_q[b], attn_w_o[b], k_sh, v_sh)
        h = h + hier_moe(rms_norm(h, norm_ffn_g[layer]), router_w_group[layer], router_b_group[layer],
                         router_w_expert[layer], router_b_expert[layer], expert_w_gate[layer],
                         expert_w_up[layer], expert_w_down[layer])
    return rms_norm(h, norm_final_g)
```

```python
import functools

import jax
import jax.numpy as jnp
from jax import lax
from jax.experimental import pallas as pl
from jax.experimental.pallas import tpu as pltpu

F32 = jnp.float32
BF16 = jnp.bfloat16
I32 = jnp.int32

EPS = 1e-6
LANES = 128
V7X_VMEM_BYTES = 64 * 1024 * 1024
VMEM_LIMIT = V7X_VMEM_BYTES - 8 * 1024 * 1024

CONV_WIDTH = 31
HALO = 32
ATTN_DILATIONS = (1, 4, 16)
N_ATTN_GROUPS = 3
HEADS = 8
HEAD_DIM = 128
BAND = 128
ATTN_OUT = HEADS * HEAD_DIM
N_GROUPS = 4
EXPERTS_PER_GROUP = 8
N_EXPERTS = N_GROUPS * EXPERTS_PER_GROUP

ROW_CHUNK = 32


def _params(*sem):
    return pltpu.CompilerParams(dimension_semantics=sem, vmem_limit_bytes=VMEM_LIMIT)


def _row_loop(n_rows, body, chunk=ROW_CHUNK):
    def step(c, carry):
        body(pl.multiple_of(c * chunk, chunk))
        return carry
    lax.fori_loop(0, n_rows // chunk, step, 0)


def _rms(x, g):
    ms = jnp.mean(x * x, axis=-1, keepdims=True)
    return x * lax.rsqrt(ms + EPS) * g


def _bdot(a, b):
    return jnp.dot(a, b, preferred_element_type=F32)


def _glu_in_kernel(x_ref, g_ref, wv_ref, wg_ref, bv_ref, bg_ref, o_ref, xn_ref):
    tm = x_ref.shape[0]

    @pl.when(pl.program_id(1) == 0)
    def _():
        def norm(r0):
            rows = pl.ds(r0, ROW_CHUNK)
            xn_ref[rows, :] = _rms(x_ref[rows, :], g_ref[...]).astype(BF16)
        _row_loop(tm, norm)

    xn = xn_ref[...]
    val = _bdot(xn, wv_ref[...].astype(BF16)) + bv_ref[...]
    gate = _bdot(xn, wg_ref[...].astype(BF16)) + bg_ref[...]
    o_ref[...] = (val * jax.nn.sigmoid(gate)).astype(o_ref.dtype)


def _glu_in(x, g, w_in, b_in, tm=1024, tn=512):
    n, d = x.shape
    c = w_in.shape[1] // 2
    nj = c // tn
    b2 = b_in.reshape(1, 2 * c)
    return pl.pallas_call(
        _glu_in_kernel,
        grid=(n // tm, nj),
        in_specs=[
            pl.BlockSpec((tm, d), lambda i, j: (i, 0)),
            pl.BlockSpec((1, d), lambda i, j: (0, 0)),
            pl.BlockSpec((d, tn), lambda i, j: (0, j)),
            pl.BlockSpec((d, tn), lambda i, j: (0, j + nj)),
            pl.BlockSpec((1, tn), lambda i, j: (0, j)),
            pl.BlockSpec((1, tn), lambda i, j: (0, j + nj)),
        ],
        out_specs=pl.BlockSpec((tm, tn), lambda i, j: (i, j)),
        out_shape=jax.ShapeDtypeStruct((n, c), BF16),
        scratch_shapes=[pltpu.VMEM((tm, d), BF16)],
        compiler_params=_params("arbitrary", "arbitrary"),
        name="glu_in",
    )(x, g.reshape(1, d), w_in, w_in, b2, b2)


def _dwconv_kernel(cur_ref, halo_ref, w_ref, b_ref, o_ref, buf_ref):
    ts, cw = cur_ref.shape[1], cur_ref.shape[2]
    rw = 64
    keep = jnp.where(pl.program_id(1) > 0, 1.0, 0.0)
    buf_ref[0:HALO, :] = halo_ref[0].astype(F32) * keep
    buf_ref[HALO:, :] = cur_ref[0].astype(F32)
    first = HALO - (CONV_WIDTH - 1)
    for c0 in range(0, cw, LANES):
        cols = slice(c0, c0 + LANES)
        for r0 in range(0, ts, rw):
            acc = jnp.broadcast_to(b_ref[:, cols], (rw, LANES))
            for k in range(CONV_WIDTH):
                acc = acc + w_ref[k:k + 1, cols] * buf_ref[r0 + first + k:r0 + first + k + rw, cols]
            o_ref[0, r0:r0 + rw, cols] = acc.astype(o_ref.dtype)


def _dwconv(u, w_dw, b_dw, ts=256, cw=512):
    b, s, c = u.shape
    hb = ts // HALO
    return pl.pallas_call(
        _dwconv_kernel,
        grid=(b, s // ts, c // cw),
        in_specs=[
            pl.BlockSpec((1, ts, cw), lambda bi, si, ci: (bi, si, ci)),
            pl.BlockSpec((1, HALO, cw), lambda bi, si, ci: (bi, jnp.maximum(si * hb - 1, 0), ci)),
            pl.BlockSpec((CONV_WIDTH, cw), lambda bi, si, ci: (0, ci)),
            pl.BlockSpec((1, cw), lambda bi, si, ci: (0, ci)),
        ],
        out_specs=pl.BlockSpec((1, ts, cw), lambda bi, si, ci: (bi, si, ci)),
        out_shape=jax.ShapeDtypeStruct((b, s, c), BF16),
        scratch_shapes=[pltpu.VMEM((ts + HALO, cw), F32)],
        compiler_params=_params("arbitrary", "arbitrary", "arbitrary"),
        name="dwconv",
    )(u, u, w_dw, b_dw.reshape(1, c))


def _ln_out_kernel(y_ref, lg_ref, lb_ref, w_ref, b_ref, res_ref, o_ref, a_ref):
    tm = y_ref.shape[0]

    @pl.when(pl.program_id(1) == 0)
    def _():
        def norm(r0):
            rows = pl.ds(r0, ROW_CHUNK)
            y = y_ref[rows, :].astype(F32)
            mu = jnp.mean(y, axis=-1, keepdims=True)
            yc = y - mu
            var = jnp.mean(yc * yc, axis=-1, keepdims=True)
            z = yc * lax.rsqrt(var + EPS) * lg_ref[...] + lb_ref[...]
            a_ref[rows, :] = (z * jax.nn.sigmoid(z)).astype(BF16)
        _row_loop(tm, norm)

    o_ref[...] = res_ref[...] + _bdot(a_ref[...], w_ref[...].astype(BF16)) + b_ref[...]


def _ln_out(y, ln_g, ln_b, w_out, b_out, res, tm=1024, tn=512):
    n, c = y.shape
    d = w_out.shape[1]
    return pl.pallas_call(
        _ln_out_kernel,
        grid=(n // tm, d // tn),
        in_specs=[
            pl.BlockSpec((tm, c), lambda i, j: (i, 0)),
            pl.BlockSpec((1, c), lambda i, j: (0, 0)),
            pl.BlockSpec((1, c), lambda i, j: (0, 0)),
            pl.BlockSpec((c, tn), lambda i, j: (0, j)),
            pl.BlockSpec((1, tn), lambda i, j: (0, j)),
            pl.BlockSpec((tm, tn), lambda i, j: (i, j)),
        ],
        out_specs=pl.BlockSpec((tm, tn), lambda i, j: (i, j)),
        out_shape=jax.ShapeDtypeStruct((n, d), F32),
        scratch_shapes=[pltpu.VMEM((tm, c), BF16)],
        compiler_params=_params("arbitrary", "arbitrary"),
        name="ln_out",
    )(y, ln_g.reshape(1, c), ln_b.reshape(1, c), w_out, b_out.reshape(1, d), res)


def _split_bf16(a):
    hi = a.astype(BF16)
    lo = (a - hi.astype(F32)).astype(BF16)
    return hi, lo


def _router_kernel(h_ref, g_ref, w_ref, b_ref, ids_ref, wts_ref):
    t = _rms(h_ref[...], g_ref[...])
    t_hi, t_lo = _split_bf16(t)
    w_hi, w_lo = _split_bf16(w_ref[...])
    logits = _bdot(t_hi, w_hi) + _bdot(t_hi, w_lo) + _bdot(t_lo, w_hi) + b_ref[...]

    lane = lax.broadcasted_iota(I32, logits.shape, 1)
    lanef = lane.astype(F32)
    neg = -jnp.inf
    is_group = lane < N_GROUPS
    glog = jnp.where(is_group, logits, neg)
    gmax = jnp.max(glog, axis=1, keepdims=True)
    gi = jnp.min(jnp.where(glog == gmax, lanef, float(LANES)), axis=1, keepdims=True)
    gsum = jnp.sum(jnp.where(is_group, jnp.exp(logits - gmax), 0.0), axis=1, keepdims=True)
    gp = 1.0 / gsum

    lo_lane = N_GROUPS + gi * EXPERTS_PER_GROUP
    in_sel = (lanef >= lo_lane) & (lanef < lo_lane + EXPERTS_PER_GROUP)
    sel = jnp.where(in_sel, logits, neg)
    m1 = jnp.max(sel, axis=1, keepdims=True)
    i1 = jnp.min(jnp.where(sel == m1, lanef, float(LANES)), axis=1, keepdims=True)
    sel2 = jnp.where(lanef == i1, neg, sel)
    m2 = jnp.max(sel2, axis=1, keepdims=True)
    i2 = jnp.min(jnp.where(sel2 == m2, lanef, float(LANES)), axis=1, keepdims=True)
    e2 = jnp.exp(m2 - m1)
    w1 = gp / (1.0 + e2)
    w2 = gp * e2 / (1.0 + e2)

    ids = jnp.where(lane == 0, i1 - N_GROUPS, jnp.where(lane == 1, i2 - N_GROUPS, 0.0))
    ids_ref[...] = ids.astype(I32)
    wts_ref[...] = jnp.where(lane == 0, w1, jnp.where(lane == 1, w2, 0.0))


def _router(h, g, w_group, b_group, w_expert, b_expert, tm=512):
    n, d = h.shape
    w_e = jnp.transpose(w_expert, (1, 0, 2)).reshape(d, N_EXPERTS)
    pad = LANES - N_GROUPS - N_EXPERTS
    wr = jnp.concatenate([w_group, w_e, jnp.zeros((d, pad), F32)], axis=1)
    br = jnp.concatenate([b_group, b_expert.reshape(N_EXPERTS), jnp.zeros((pad,), F32)]).reshape(1, LANES)
    return pl.pallas_call(
        _router_kernel,
        grid=(n // tm,),
        in_specs=[
            pl.BlockSpec((tm, d), lambda i: (i, 0)),
            pl.BlockSpec((1, d), lambda i: (0, 0)),
            pl.BlockSpec((d, LANES), lambda i: (0, 0)),
            pl.BlockSpec((1, LANES), lambda i: (0, 0)),
        ],
        out_specs=[pl.BlockSpec((tm, LANES), lambda i: (i, 0)),
                   pl.BlockSpec((tm, LANES), lambda i: (i, 0))],
        out_shape=[jax.ShapeDtypeStruct((n, LANES), I32), jax.ShapeDtypeStruct((n, LANES), F32)],
        compiler_params=_params("arbitrary"),
        name="router",
    )(h, g.reshape(1, d), wr, br)


def _dispatch_plan(ids, tm, n_items):
    e_flat = jnp.concatenate([ids[:, 0], ids[:, 1]])
    onehot = (e_flat[:, None] == jnp.arange(N_EXPERTS, dtype=I32)[None, :]).astype(I32)
    csum = jnp.cumsum(onehot, axis=0)
    rank = jnp.sum(onehot * csum, axis=1) - 1
    counts = csum[-1]
    ends = jnp.cumsum(counts)
    starts = ends - counts
    dest = jnp.sum(onehot * starts[None, :], axis=1) + rank

    first_tile = starts // tm
    last_tile = jnp.maximum(ends - 1, 0) // tm
    items = jnp.where(counts > 0, last_tile - first_tile + 1, 0)
    item_end = jnp.cumsum(items)
    item_start = item_end - items
    total = item_end[-1]
    w = jnp.arange(n_items, dtype=I32)
    valid = w < total
    wc = jnp.minimum(w, total - 1)
    e_w = jnp.searchsorted(item_end, wc, side="right").astype(I32)
    tile_w = first_tile[e_w] + (wc - item_start[e_w])
    lo = jnp.maximum(starts[e_w], tile_w * tm) - tile_w * tm
    hi = jnp.minimum(ends[e_w], (tile_w + 1) * tm) - tile_w * tm
    lo = jnp.where(valid, lo, 0)
    hi = jnp.where(valid, hi, 0)
    prev_tile = jnp.concatenate([jnp.full((1,), -1, I32), tile_w[:-1]])
    prev_e = jnp.concatenate([jnp.full((1,), -1, I32), e_w[:-1]])
    first = (valid & (tile_w != prev_tile)).astype(I32)
    new_e = (e_w != prev_e).astype(I32)
    plan = (tile_w.astype(I32), e_w, lo.astype(I32), hi.astype(I32), first, new_e, valid.astype(I32))
    return dest.astype(I32), plan


def _scatter_kernel(dest_ref, h_ref, g_ref, xs_ref, t_ref, sem):
    tm = h_ref.shape[0]
    n = pl.num_programs(0) * tm
    base = pl.program_id(0) * tm

    def norm(r0):
        rows = pl.ds(r0, ROW_CHUNK)
        t_ref[rows, :] = _rms(h_ref[rows, :], g_ref[...])
    _row_loop(tm, norm)

    def row_copy(r, slot):
        d = dest_ref[slot * n + base + r]
        return pltpu.make_async_copy(t_ref.at[pl.ds(r, 1), :], xs_ref.at[pl.ds(d, 1), :], sem)

    def start(r, carry):
        row_copy(r, 0).start()
        row_copy(r, 1).start()
        return carry
    lax.fori_loop(0, tm, start, 0)

    def wait(r, carry):
        row_copy(r, 0).wait()
        row_copy(r, 1).wait()
        return carry
    lax.fori_loop(0, tm, wait, 0)


def _scatter(h, g, dest, tm=256):
    n, d = h.shape
    return pl.pallas_call(
        _scatter_kernel,
        grid_spec=pltpu.PrefetchScalarGridSpec(
            num_scalar_prefetch=1,
            grid=(n // tm,),
            in_specs=[pl.BlockSpec((tm, d), lambda i, dest: (i, 0)),
                      pl.BlockSpec((1, d), lambda i, dest: (0, 0))],
            out_specs=pl.BlockSpec(memory_space=pl.ANY),
            scratch_shapes=[pltpu.VMEM((tm, d), F32), pltpu.SemaphoreType.DMA(())],
        ),
        out_shape=jax.ShapeDtypeStruct((2 * n, d), F32),
        compiler_params=_params("arbitrary"),
        name="moe_scatter",
    )(dest, h, g.reshape(1, d))


def _experts_kernel(tile_ref, exp_ref, lo_ref, hi_ref, first_ref, new_ref, valid_ref,
                    xs_ref, wg_ref, wu_ref, wd_ref, o_ref, wg_bf, wu_bf, wd_bf):
    w = pl.program_id(0)
    tm = xs_ref.shape[0]
    chunk = 256

    @pl.when(new_ref[w] == 1)
    def _():
        def cast_in(r0):
            rows = pl.ds(r0, chunk)
            wg_bf[rows, :] = wg_ref[0, rows, :].astype(BF16)
            wu_bf[rows, :] = wu_ref[0, rows, :].astype(BF16)
        _row_loop(wg_bf.shape[0], cast_in, chunk)
        wd_bf[...] = wd_ref[0].astype(BF16)

    @pl.when(valid_ref[w] == 1)
    def _():
        row = lax.broadcasted_iota(I32, (tm, 1), 0)
        mine = (row >= lo_ref[w]) & (row < hi_ref[w])
        x = jnp.where(mine, xs_ref[...], 0.0).astype(BF16)
        hg = _bdot(x, wg_bf[...])
        hu = _bdot(x, wu_bf[...])
        act = (hg * jax.nn.sigmoid(hg) * hu).astype(BF16)
        y = _bdot(act, wd_bf[...])

        @pl.when(first_ref[w] == 1)
        def _():
            o_ref[...] = y

        @pl.when(first_ref[w] == 0)
        def _():
            o_ref[...] += y


def _experts(xs, plan, w_gate, w_up, w_down, tm):
    p, d = xs.shape
    f = w_gate.shape[2]
    n_items = plan[0].shape[0]
    return pl.pallas_call(
        _experts_kernel,
        grid_spec=pltpu.PrefetchScalarGridSpec(
            num_scalar_prefetch=7,
            grid=(n_items,),
            in_specs=[
                pl.BlockSpec((tm, d), lambda w, tile, exp, *_: (tile[w], 0)),
                pl.BlockSpec((1, d, f), lambda w, tile, exp, *_: (exp[w], 0, 0)),
                pl.BlockSpec((1, d, f), lambda w, tile, exp, *_: (exp[w], 0, 0)),
                pl.BlockSpec((1, f, d), lambda w, tile, exp, *_: (exp[w], 0, 0)),
            ],
            out_specs=pl.BlockSpec((tm, d), lambda w, tile, exp, *_: (tile[w], 0)),
            scratch_shapes=[pltpu.VMEM((d, f), BF16), pltpu.VMEM((d, f), BF16), pltpu.VMEM((f, d), BF16)],
        ),
        out_shape=jax.ShapeDtypeStruct((p, d), F32),
        compiler_params=_params("arbitrary"),
        name="moe_experts",
    )(*plan, xs, w_gate, w_up, w_down)


def _combine_kernel(dest_ref, h_ref, wts_ref, g_ref, ys_ref, o_ref, y_ref, sem, *, final_norm):
    tm = h_ref.shape[0]
    n = pl.num_programs(0) * tm
    base = pl.program_id(0) * tm

    def row_copy(r, slot):
        d = dest_ref[slot * n + base + r]
        return pltpu.make_async_copy(ys_ref.at[pl.ds(d, 1), :], y_ref.at[slot, pl.ds(r, 1), :], sem)

    def start(r, carry):
        row_copy(r, 0).start()
        row_copy(r, 1).start()
        return carry
    lax.fori_loop(0, tm, start, 0)

    def wait(r, carry):
        row_copy(r, 0).wait()
        row_copy(r, 1).wait()
        return carry
    lax.fori_loop(0, tm, wait, 0)

    def mix(r0):
        rows = pl.ds(r0, ROW_CHUNK)
        wts = wts_ref[rows, :]
        out = h_ref[rows, :] + wts[:, 0:1] * y_ref[0, rows, :] + wts[:, 1:2] * y_ref[1, rows, :]
        if final_norm:
            out = _rms(out, g_ref[...])
        o_ref[rows, :] = out
    _row_loop(tm, mix)


def _combine(h, wts, ys, dest, g_final, final_norm, tm=256):
    n, d = h.shape
    return pl.pallas_call(
        functools.partial(_combine_kernel, final_norm=final_norm),
        grid_spec=pltpu.PrefetchScalarGridSpec(
            num_scalar_prefetch=1,
            grid=(n // tm,),
            in_specs=[pl.BlockSpec((tm, d), lambda i, dest: (i, 0)),
                      pl.BlockSpec((tm, LANES), lambda i, dest: (i, 0)),
                      pl.BlockSpec((1, d), lambda i, dest: (0, 0)),
                      pl.BlockSpec(memory_space=pl.ANY)],
            out_specs=pl.BlockSpec((tm, d), lambda i, dest: (i, 0)),
            scratch_shapes=[pltpu.VMEM((2, tm, d), F32), pltpu.SemaphoreType.DMA(())],
        ),
        out_shape=jax.ShapeDtypeStruct((n, d), F32),
        compiler_params=_params("arbitrary"),
        name="moe_combine",
    )(dest, h, wts, g_final.reshape(1, d), ys)


def _moe(h, g, w_group, b_group, w_expert, b_expert, w_gate, w_up, w_down, g_final, final_norm,
         tm=256):
    n = h.shape[0]
    ids, wts = _router(h, g, w_group, b_group, w_expert, b_expert)
    n_items = 2 * n // tm + N_EXPERTS
    dest, plan = _dispatch_plan(ids, tm, n_items)
    xs = _scatter(h, g, dest)
    ys = _experts(xs, plan, w_gate, w_up, w_down, tm)
    return _combine(h, wts, ys, dest, g_final, final_norm)


def _qkv_kernel(*refs, pieces):
    x_refs = refs[:pieces]
    gq_ref, gkv_ref, wq_ref, wkv_ref, o_ref, xq_ref, xkv_ref = refs[pieces:]
    j = pl.program_id(1)
    rp = x_refs[0].shape[1]
    q_blocks = ATTN_OUT // o_ref.shape[1]

    @pl.when(j == 0)
    def _():
        for p, x_ref in enumerate(x_refs):
            def norm(r0, x_ref=x_ref, p=p):
                x = x_ref[0, pl.ds(r0, ROW_CHUNK), :]
                xhat = x * lax.rsqrt(jnp.mean(x * x, axis=-1, keepdims=True) + EPS)
                rows = pl.ds(p * rp + r0, ROW_CHUNK)
                xq_ref[rows, :] = (xhat * gq_ref[...]).astype(BF16)
                xkv_ref[rows, :] = (xhat * gkv_ref[...]).astype(BF16)
            _row_loop(rp, norm)

    @pl.when(j < q_blocks)
    def _():
        o_ref[...] = _bdot(xq_ref[...], wq_ref[...].astype(BF16)).astype(o_ref.dtype)

    @pl.when(j >= q_blocks)
    def _():
        o_ref[...] = _bdot(xkv_ref[...], wkv_ref[...].astype(BF16)).astype(o_ref.dtype)


def _qkv(h3, g_q, g_kv, w_q, w_kv, group, tm=1024, tn=512):
    b, s, d = h3.shape
    r = ATTN_DILATIONS[group]
    sub = s // r
    rp = min(sub, tm)
    pieces = tm // rp
    tiles_per_class = sub // rp
    tiles_per_batch = s // tm
    hv = h3.reshape(b, sub, r * d)
    qb = ATTN_OUT // tn

    def x_map(p):
        def index(i, j):
            bi, ti = i // tiles_per_batch, i % tiles_per_batch
            return (bi, ti % tiles_per_class, (ti // tiles_per_class) * pieces + p)
        return index

    def wq_map(i, j):
        return (0, group * qb + jnp.minimum(j, qb - 1))

    def wkv_map(i, j):
        jj = jnp.maximum(j - qb, 0)
        return (0, (jj // qb) * N_ATTN_GROUPS * qb + group * qb + jj % qb)

    in_specs = [pl.BlockSpec((1, rp, d), x_map(p)) for p in range(pieces)]
    in_specs += [
        pl.BlockSpec((1, d), lambda i, j: (0, 0)),
        pl.BlockSpec((1, d), lambda i, j: (0, 0)),
        pl.BlockSpec((d, tn), wq_map),
        pl.BlockSpec((d, tn), wkv_map),
    ]
    return pl.pallas_call(
        functools.partial(_qkv_kernel, pieces=pieces),
        grid=(b * s // tm, 3 * qb),
        in_specs=in_specs,
        out_specs=pl.BlockSpec((tm, tn), lambda i, j: (i, j)),
        out_shape=jax.ShapeDtypeStruct((b * s, 3 * ATTN_OUT), BF16),
        scratch_shapes=[pltpu.VMEM((tm, d), BF16), pltpu.VMEM((tm, d), BF16)],
        compiler_params=_params("arbitrary", "arbitrary"),
        name=f"qkv_g{group}",
    )(*([hv] * pieces), g_q.reshape(1, d), g_kv.reshape(1, d), w_q, w_kv)


def _dot_t(a, b):
    return lax.dot_general(a, b, (((1,), (1,)), ((), ())), preferred_element_type=F32)


def _attn_kernel(q_ref, kc_ref, vc_ref, kp_ref, vp_ref, o_ref, lse_ref, *, seg_blocks):
    qblocks = q_ref.shape[0] // BAND
    step = pl.program_id(0)
    qi = lax.broadcasted_iota(I32, (BAND, BAND), 0)
    kj = lax.broadcasted_iota(I32, (BAND, BAND), 1)
    cur_mask = kj <= qi
    prev_mask = kj >= qi
    lane = lax.broadcasted_iota(I32, (BAND, LANES), 1)
    scale = HEAD_DIM ** -0.5
    neg = -jnp.inf

    for sb in range(qblocks):
        rows = slice(sb * BAND, (sb + 1) * BAND)
        if seg_blocks == 1 or (sb == 0 and seg_blocks <= qblocks):
            prev = None
        elif sb == 0:
            at_start = (step * qblocks) % seg_blocks == 0
            prev = (kp_ref, vp_ref, slice(0, BAND), jnp.where(at_start, neg, 0.0))
        else:
            prev = (kc_ref, vc_ref, slice((sb - 1) * BAND, sb * BAND), None)
        lse_slab = jnp.zeros((BAND, LANES), F32)
        for h in range(HEADS):
            cols = slice(h * HEAD_DIM, (h + 1) * HEAD_DIM)
            q = q_ref[rows, cols]
            s_c = jnp.where(cur_mask, _dot_t(q, kc_ref[rows, cols]) * scale, neg)
            m = jnp.max(s_c, axis=1, keepdims=True)
            if prev is not None:
                k_ref, v_ref, prow, off = prev
                s_p = jnp.where(prev_mask, _dot_t(q, k_ref[prow, cols]) * scale, neg)
                if off is not None:
                    s_p = s_p + off
                m = jnp.maximum(m, jnp.max(s_p, axis=1, keepdims=True))
            p_c = jnp.exp(s_c - m)
            l = jnp.sum(p_c, axis=1, keepdims=True)
            acc = _bdot(p_c.astype(BF16), vc_ref[rows, cols])
            if prev is not None:
                p_p = jnp.exp(s_p - m)
                l = l + jnp.sum(p_p, axis=1, keepdims=True)
                acc = acc + _bdot(p_p.astype(BF16), v_ref[prow, cols])
            o_ref[rows, cols] = (acc / l).astype(o_ref.dtype)
            lse_slab = jnp.where(lane == h, m + jnp.log(l), lse_slab)
        lse_ref[rows, :] = lse_slab


def _attn(qkv, group, seq, rows_per_step=512):
    n = qkv.shape[0]
    seg_blocks = seq // ATTN_DILATIONS[group] // BAND
    qblocks = rows_per_step // BAND
    return pl.pallas_call(
        functools.partial(_attn_kernel, seg_blocks=seg_blocks),
        grid=(n // rows_per_step,),
        in_specs=[
            pl.BlockSpec((rows_per_step, ATTN_OUT), lambda i: (i, 0)),
            pl.BlockSpec((rows_per_step, ATTN_OUT), lambda i: (i, 1)),
            pl.BlockSpec((rows_per_step, ATTN_OUT), lambda i: (i, 2)),
            pl.BlockSpec((BAND, ATTN_OUT), lambda i: (jnp.maximum(i * qblocks - 1, 0), 1)),
            pl.BlockSpec((BAND, ATTN_OUT), lambda i: (jnp.maximum(i * qblocks - 1, 0), 2)),
        ],
        out_specs=[pl.BlockSpec((rows_per_step, ATTN_OUT), lambda i: (i, 0)),
                   pl.BlockSpec((rows_per_step, LANES), lambda i: (i, 0))],
        out_shape=[jax.ShapeDtypeStruct((n, ATTN_OUT), BF16), jax.ShapeDtypeStruct((n, LANES), F32)],
        compiler_params=_params("arbitrary"),
        name=f"attn_g{group}",
    )(qkv, qkv, qkv, qkv, qkv)


def _merge_out_kernel(*refs, pieces):
    o0_ref, l0_ref, o2_ref, l2_ref, h_ref = refs[:5]
    g1_refs = refs[5:5 + 2 * pieces]
    w_ref, o_ref, m_ref, wbf_ref = refs[5 + 2 * pieces:]
    u = o0_ref.shape[1]
    d = w_ref.shape[1]

    @pl.when((pl.program_id(0) == 0) & (pl.program_id(1) == 0))
    def _():
        wbf_ref[...] = w_ref[...].astype(BF16)

    for p in range(pieces):
        rows = slice(p * u, (p + 1) * u)
        o1_ref, l1_ref = g1_refs[2 * p], g1_refs[2 * p + 1]
        lses = (l0_ref[0, :, p * LANES:(p + 1) * LANES], l1_ref[0], l2_ref[rows, :])
        for h in range(HEADS):
            cols = slice(h * HEAD_DIM, (h + 1) * HEAD_DIM)
            lh = [l[:, h:h + 1] for l in lses]
            mx = jnp.maximum(jnp.maximum(lh[0], lh[1]), lh[2])
            ex = [jnp.exp(l - mx) for l in lh]
            inv = 1.0 / (ex[0] + ex[1] + ex[2])
            vals = (o0_ref[0, :, p * ATTN_OUT + h * HEAD_DIM:p * ATTN_OUT + (h + 1) * HEAD_DIM],
                    o1_ref[0, :, cols], o2_ref[rows, cols])
            merged = sum((e * inv) * v.astype(F32) for e, v in zip(ex, vals))
            m_ref[rows, cols] = merged.astype(BF16)
    y = _bdot(m_ref[...], wbf_ref[...])
    for p in range(pieces):
        cols = slice(p * d, (p + 1) * d)
        o_ref[0, :, cols] = h_ref[0, :, cols] + y[p * u:(p + 1) * u, :]


def _merge_out(h3, attn, w_o, pieces=2):
    b, s, d = h3.shape
    r1, r2 = ATTN_DILATIONS[1], ATTN_DILATIONS[2]
    u = s // r2
    (o0, l0), (o1, l1), (o2, l2) = attn
    o0v = o0.reshape(b, u, r2 * ATTN_OUT)
    l0v = l0.reshape(b, u, r2 * LANES)
    o1v = o1.reshape(b * r1, u, (r2 // r1) * ATTN_OUT)
    l1v = l1.reshape(b * r1, u, (r2 // r1) * LANES)
    hv = h3.reshape(b, u, r2 * d)
    steps = r2 // pieces

    def g1_map(p):
        def index(bi, i):
            m16 = i * pieces + p
            return (bi * r1 + m16 % r1, 0, m16 // r1)
        return index

    in_specs = [
        pl.BlockSpec((1, u, pieces * ATTN_OUT), lambda bi, i: (bi, 0, i)),
        pl.BlockSpec((1, u, pieces * LANES), lambda bi, i: (bi, 0, i)),
        pl.BlockSpec((pieces * u, ATTN_OUT), lambda bi, i: (bi * steps + i, 0)),
        pl.BlockSpec((pieces * u, LANES), lambda bi, i: (bi * steps + i, 0)),
        pl.BlockSpec((1, u, pieces * d), lambda bi, i: (bi, 0, i)),
    ]
    args = [o0v, l0v, o2, l2, hv]
    for p in range(pieces):
        in_specs += [pl.BlockSpec((1, u, ATTN_OUT), g1_map(p)), pl.BlockSpec((1, u, LANES), g1_map(p))]
        args += [o1v, l1v]
    in_specs.append(pl.BlockSpec((ATTN_OUT, d), lambda bi, i: (0, 0)))
    args.append(w_o)
    out = pl.pallas_call(
        functools.partial(_merge_out_kernel, pieces=pieces),
        grid=(b, steps),
        in_specs=in_specs,
        out_specs=pl.BlockSpec((1, u, pieces * d), lambda bi, i: (bi, 0, i)),
        out_shape=jax.ShapeDtypeStruct((b, u, r2 * d), F32),
        scratch_shapes=[pltpu.VMEM((pieces * u, ATTN_OUT), BF16), pltpu.VMEM((ATTN_OUT, d), BF16)],
        compiler_params=_params("arbitrary", "arbitrary"),
        name="attn_merge_out",
    )(*args)
    return out.reshape(b, s, d)


def kernel(x, norm_mix_g, norm_ffn_g, conv_w_in, conv_b_in, conv_w_dw, conv_b_dw, conv_ln_g, conv_ln_b,
           conv_w_out, conv_b_out, norm_kv_g, w_kv, attn_w_q, attn_w_o, router_w_group, router_b_group,
           router_w_expert, router_b_expert, expert_w_gate, expert_w_up, expert_w_down, norm_final_g):
    b, s, d = x.shape
    n = b * s
    xf = x.reshape(n, d)

    u = _glu_in(xf, norm_mix_g[0], conv_w_in[0], conv_b_in[0])
    y = _dwconv(u.reshape(b, s, -1), conv_w_dw[0], conv_b_dw[0])
    h = _ln_out(y.reshape(n, -1), conv_ln_g[0], conv_ln_b[0], conv_w_out[0], conv_b_out[0], xf)
    h = _moe(h, norm_ffn_g[0], router_w_group[0], router_b_group[0], router_w_expert[0],
             router_b_expert[0], expert_w_gate[0], expert_w_up[0], expert_w_down[0],
             norm_final_g, False)

    h3 = h.reshape(b, s, d)
    attn = []
    for group in range(N_ATTN_GROUPS):
        qkv = _qkv(h3, norm_mix_g[1], norm_kv_g, attn_w_q[0], w_kv, group)
        attn.append(_attn(qkv, group, s))
    h = _merge_out(h3, attn, attn_w_o[0]).reshape(n, d)
    out = _moe(h, norm_ffn_g[1], router_w_group[1], router_b_group[1], router_w_expert[1],
               router_b_expert[1], expert_w_gate[1], expert_w_up[1], expert_w_down[1],
               norm_final_g, True)
    return out.reshape(b, s, d)
```

```python
import functools

import jax
import jax.numpy as jnp
from jax import lax
from jax.experimental import pallas as pl
from jax.experimental.pallas import tpu as pltpu

F32 = jnp.float32
BF16 = jnp.bfloat16
I32 = jnp.int32

EPS = 1e-6
LANES = 128
V7X_VMEM_BYTES = 64 * 1024 * 1024
VMEM_LIMIT = V7X_VMEM_BYTES - 8 * 1024 * 1024

CONV_WIDTH = 31
HALO = 32
ATTN_DILATIONS = (1, 4, 16)
N_ATTN_GROUPS = 3
HEADS = 8
HEAD_DIM = 128
BAND = 128
ATTN_OUT = HEADS * HEAD_DIM
N_GROUPS = 4
EXPERTS_PER_GROUP = 8
N_EXPERTS = N_GROUPS * EXPERTS_PER_GROUP

ROW_CHUNK = 32
SUBLANES = 8


def _params(*sem):
    return pltpu.CompilerParams(dimension_semantics=sem, vmem_limit_bytes=VMEM_LIMIT)


def _row_loop(n_rows, body, chunk=ROW_CHUNK):
    def step(c, carry):
        body(pl.multiple_of(c * chunk, chunk))
        return carry
    lax.fori_loop(0, n_rows // chunk, step, 0)


def _rms(x, g):
    ms = jnp.mean(x * x, axis=-1, keepdims=True)
    return x * lax.rsqrt(ms + EPS) * g


def _bdot(a, b):
    return jnp.dot(a, b, preferred_element_type=F32)


def _glu_in_kernel(x_ref, g_ref, wv_ref, wg_ref, bv_ref, bg_ref, o_ref, xn_ref):
    tm = x_ref.shape[0]

    @pl.when(pl.program_id(1) == 0)
    def _():
        def norm(r0):
            rows = pl.ds(r0, ROW_CHUNK)
            xn_ref[rows, :] = _rms(x_ref[rows, :], g_ref[...]).astype(BF16)
        _row_loop(tm, norm)

    xn = xn_ref[...]
    val = _bdot(xn, wv_ref[...].astype(BF16)) + bv_ref[...]
    gate = _bdot(xn, wg_ref[...].astype(BF16)) + bg_ref[...]
    o_ref[...] = (val * jax.nn.sigmoid(gate)).astype(o_ref.dtype)


def _glu_in(x, g, w_in, b_in, tm=1024, tn=512):
    n, d = x.shape
    c = w_in.shape[1] // 2
    nj = c // tn
    b2 = b_in.reshape(1, 2 * c)
    return pl.pallas_call(
        _glu_in_kernel,
        grid=(n // tm, nj),
        in_specs=[
            pl.BlockSpec((tm, d), lambda i, j: (i, 0)),
            pl.BlockSpec((1, d), lambda i, j: (0, 0)),
            pl.BlockSpec((d, tn), lambda i, j: (0, j)),
            pl.BlockSpec((d, tn), lambda i, j: (0, j + nj)),
            pl.BlockSpec((1, tn), lambda i, j: (0, j)),
            pl.BlockSpec((1, tn), lambda i, j: (0, j + nj)),
        ],
        out_specs=pl.BlockSpec((tm, tn), lambda i, j: (i, j)),
        out_shape=jax.ShapeDtypeStruct((n, c), BF16),
        scratch_shapes=[pltpu.VMEM((tm, d), BF16)],
        compiler_params=_params("arbitrary", "arbitrary"),
        name="glu_in",
    )(x, g.reshape(1, d), w_in, w_in, b2, b2)


def _dwconv_kernel(cur_ref, halo_ref, w_ref, b_ref, o_ref, buf_ref):
    ts, cw = cur_ref.shape[1], cur_ref.shape[2]
    rw = 64
    keep = jnp.where(pl.program_id(1) > 0, 1.0, 0.0)
    buf_ref[0:HALO, :] = halo_ref[0].astype(F32) * keep
    buf_ref[HALO:, :] = cur_ref[0].astype(F32)
    first = HALO - (CONV_WIDTH - 1)
    for c0 in range(0, cw, LANES):
        cols = slice(c0, c0 + LANES)
        for r0 in range(0, ts, rw):
            acc = jnp.broadcast_to(b_ref[:, cols], (rw, LANES))
            for k in range(CONV_WIDTH):
                acc = acc + w_ref[k:k + 1, cols] * buf_ref[r0 + first + k:r0 + first + k + rw, cols]
            o_ref[0, r0:r0 + rw, cols] = acc.astype(o_ref.dtype)


def _dwconv(u, w_dw, b_dw, ts=256, cw=512):
    b, s, c = u.shape
    hb = ts // HALO
    return pl.pallas_call(
        _dwconv_kernel,
        grid=(b, s // ts, c // cw),
        in_specs=[
            pl.BlockSpec((1, ts, cw), lambda bi, si, ci: (bi, si, ci)),
            pl.BlockSpec((1, HALO, cw), lambda bi, si, ci: (bi, jnp.maximum(si * hb - 1, 0), ci)),
            pl.BlockSpec((CONV_WIDTH, cw), lambda bi, si, ci: (0, ci)),
            pl.BlockSpec((1, cw), lambda bi, si, ci: (0, ci)),
        ],
        out_specs=pl.BlockSpec((1, ts, cw), lambda bi, si, ci: (bi, si, ci)),
        out_shape=jax.ShapeDtypeStruct((b, s, c), BF16),
        scratch_shapes=[pltpu.VMEM((ts + HALO, cw), F32)],
        compiler_params=_params("arbitrary", "arbitrary", "arbitrary"),
        name="dwconv",
    )(u, u, w_dw, b_dw.reshape(1, c))


def _ln_out_kernel(y_ref, lg_ref, lb_ref, w_ref, b_ref, res_ref, o_ref, a_ref):
    tm = y_ref.shape[0]

    @pl.when(pl.program_id(1) == 0)
    def _():
        def norm(r0):
            rows = pl.ds(r0, ROW_CHUNK)
            y = y_ref[rows, :].astype(F32)
            mu = jnp.mean(y, axis=-1, keepdims=True)
            yc = y - mu
            var = jnp.mean(yc * yc, axis=-1, keepdims=True)
            z = yc * lax.rsqrt(var + EPS) * lg_ref[...] + lb_ref[...]
            a_ref[rows, :] = (z * jax.nn.sigmoid(z)).astype(BF16)
        _row_loop(tm, norm)

    o_ref[...] = res_ref[...] + _bdot(a_ref[...], w_ref[...].astype(BF16)) + b_ref[...]


def _ln_out(y, ln_g, ln_b, w_out, b_out, res, tm=1024, tn=512):
    n, c = y.shape
    d = w_out.shape[1]
    return pl.pallas_call(
        _ln_out_kernel,
        grid=(n // tm, d // tn),
        in_specs=[
            pl.BlockSpec((tm, c), lambda i, j: (i, 0)),
            pl.BlockSpec((1, c), lambda i, j: (0, 0)),
            pl.BlockSpec((1, c), lambda i, j: (0, 0)),
            pl.BlockSpec((c, tn), lambda i, j: (0, j)),
            pl.BlockSpec((1, tn), lambda i, j: (0, j)),
            pl.BlockSpec((tm, tn), lambda i, j: (i, j)),
        ],
        out_specs=pl.BlockSpec((tm, tn), lambda i, j: (i, j)),
        out_shape=jax.ShapeDtypeStruct((n, d), F32),
        scratch_shapes=[pltpu.VMEM((tm, c), BF16)],
        compiler_params=_params("arbitrary", "arbitrary"),
        name="ln_out",
    )(y, ln_g.reshape(1, c), ln_b.reshape(1, c), w_out, b_out.reshape(1, d), res)


def _split_bf16(a):
    hi = a.astype(BF16)
    lo = (a - hi.astype(F32)).astype(BF16)
    return hi, lo


def _router_kernel(h_ref, g_ref, w_ref, b_ref, ids_ref, wts_ref):
    t = _rms(h_ref[...], g_ref[...])
    t_hi, t_lo = _split_bf16(t)
    w_hi, w_lo = _split_bf16(w_ref[...])
    logits = _bdot(t_hi, w_hi) + _bdot(t_hi, w_lo) + _bdot(t_lo, w_hi) + b_ref[...]

    lane = lax.broadcasted_iota(I32, logits.shape, 1)
    lanef = lane.astype(F32)
    neg = -jnp.inf
    is_group = lane < N_GROUPS
    glog = jnp.where(is_group, logits, neg)
    gmax = jnp.max(glog, axis=1, keepdims=True)
    gi = jnp.min(jnp.where(glog == gmax, lanef, float(LANES)), axis=1, keepdims=True)
    gsum = jnp.sum(jnp.where(is_group, jnp.exp(logits - gmax), 0.0), axis=1, keepdims=True)
    gp = 1.0 / gsum

    lo_lane = N_GROUPS + gi * EXPERTS_PER_GROUP
    in_sel = (lanef >= lo_lane) & (lanef < lo_lane + EXPERTS_PER_GROUP)
    sel = jnp.where(in_sel, logits, neg)
    m1 = jnp.max(sel, axis=1, keepdims=True)
    i1 = jnp.min(jnp.where(sel == m1, lanef, float(LANES)), axis=1, keepdims=True)
    sel2 = jnp.where(lanef == i1, neg, sel)
    m2 = jnp.max(sel2, axis=1, keepdims=True)
    i2 = jnp.min(jnp.where(sel2 == m2, lanef, float(LANES)), axis=1, keepdims=True)
    e2 = jnp.exp(m2 - m1)
    w1 = gp / (1.0 + e2)
    w2 = gp * e2 / (1.0 + e2)

    ids = jnp.where(lane == 0, i1 - N_GROUPS, jnp.where(lane == 1, i2 - N_GROUPS, 0.0))
    ids_ref[...] = ids.astype(I32)
    wts_ref[...] = jnp.where(lane == 0, w1, jnp.where(lane == 1, w2, 0.0))


def _router(h, g, w_group, b_group, w_expert, b_expert, tm=512):
    n, d = h.shape
    w_e = jnp.transpose(w_expert, (1, 0, 2)).reshape(d, N_EXPERTS)
    pad = LANES - N_GROUPS - N_EXPERTS
    wr = jnp.concatenate([w_group, w_e, jnp.zeros((d, pad), F32)], axis=1)
    br = jnp.concatenate([b_group, b_expert.reshape(N_EXPERTS), jnp.zeros((pad,), F32)]).reshape(1, LANES)
    return pl.pallas_call(
        _router_kernel,
        grid=(n // tm,),
        in_specs=[
            pl.BlockSpec((tm, d), lambda i: (i, 0)),
            pl.BlockSpec((1, d), lambda i: (0, 0)),
            pl.BlockSpec((d, LANES), lambda i: (0, 0)),
            pl.BlockSpec((1, LANES), lambda i: (0, 0)),
        ],
        out_specs=[pl.BlockSpec((tm, LANES), lambda i: (i, 0)),
                   pl.BlockSpec((tm, LANES), lambda i: (i, 0))],
        out_shape=[jax.ShapeDtypeStruct((n, LANES), I32), jax.ShapeDtypeStruct((n, LANES), F32)],
        compiler_params=_params("arbitrary"),
        name="router",
    )(h, g.reshape(1, d), wr, br)


def _dispatch_plan(ids, tm, n_items):
    e_flat = jnp.concatenate([ids[:, 0], ids[:, 1]])
    onehot = (e_flat[:, None] == jnp.arange(N_EXPERTS, dtype=I32)[None, :]).astype(I32)
    csum = jnp.cumsum(onehot, axis=0)
    rank = jnp.sum(onehot * csum, axis=1) - 1
    counts = csum[-1]
    ends = jnp.cumsum(counts)
    starts = ends - counts
    dest = jnp.sum(onehot * starts[None, :], axis=1) + rank

    first_tile = starts // tm
    last_tile = jnp.maximum(ends - 1, 0) // tm
    items = jnp.where(counts > 0, last_tile - first_tile + 1, 0)
    item_end = jnp.cumsum(items)
    item_start = item_end - items
    total = item_end[-1]
    w = jnp.arange(n_items, dtype=I32)
    valid = w < total
    wc = jnp.minimum(w, total - 1)
    e_w = jnp.searchsorted(item_end, wc, side="right").astype(I32)
    tile_w = first_tile[e_w] + (wc - item_start[e_w])
    lo = jnp.maximum(starts[e_w], tile_w * tm) - tile_w * tm
    hi = jnp.minimum(ends[e_w], (tile_w + 1) * tm) - tile_w * tm
    lo = jnp.where(valid, lo, 0)
    hi = jnp.where(valid, hi, 0)
    prev_tile = jnp.concatenate([jnp.full((1,), -1, I32), tile_w[:-1]])
    prev_e = jnp.concatenate([jnp.full((1,), -1, I32), e_w[:-1]])
    first = (valid & (tile_w != prev_tile)).astype(I32)
    new_e = (valid & (e_w != prev_e)).astype(I32)

    present = counts > 0
    n_present = jnp.sum(present.astype(I32))
    order = jnp.argsort(jnp.logical_not(present), stable=True).astype(I32)
    ordinal = jnp.cumsum(present.astype(I32)) - 1
    k_w = ordinal[e_w]
    slot = k_w % 2
    ahead = jnp.where(k_w + 2 < n_present, order[jnp.minimum(k_w + 2, N_EXPERTS - 1)], -1)
    head = jnp.stack([order[0], jnp.where(n_present > 1, order[1], -1)])
    plan = (tile_w.astype(I32), e_w, lo.astype(I32), hi.astype(I32), first, new_e, valid.astype(I32),
            slot.astype(I32), ahead.astype(I32), head.astype(I32))
    return dest.astype(I32), plan


def _scatter_kernel(dest_ref, h_ref, g_ref, xs_ref, t_ref, sem):
    tm = h_ref.shape[0]
    i = pl.program_id(0)
    steps = pl.num_programs(0)
    n = steps * tm
    buf = i % 2
    sub_chunk = ROW_CHUNK // SUBLANES

    def row_copy(step, b, grp, j, pick):
        d = dest_ref[pick * n + step * tm + grp * SUBLANES + j]
        return pltpu.make_async_copy(t_ref.at[b, grp, pl.ds(j, 1), :], xs_ref.at[pl.ds(d, 1), :], sem.at[b])

    def for_rows(step, b, fn):
        def body(grp, carry):
            for j in range(SUBLANES):
                fn(row_copy(step, b, grp, j, 0))
                fn(row_copy(step, b, grp, j, 1))
            return carry
        lax.fori_loop(0, tm // SUBLANES, body, 0)

    @pl.when(i >= 2)
    def _():
        for_rows(i - 2, buf, lambda copy: copy.wait())

    def norm(r0):
        t = _rms(h_ref[pl.ds(r0, ROW_CHUNK), :], g_ref[...])
        grps = pl.ds(pl.multiple_of(r0 // SUBLANES, sub_chunk), sub_chunk)
        t_ref[buf, grps, :, :] = t.reshape(sub_chunk, SUBLANES, t.shape[-1])
    _row_loop(tm, norm)

    for_rows(i, buf, lambda copy: copy.start())

    @pl.when(i == steps - 1)
    def _():
        @pl.when(i >= 1)
        def _():
            for_rows(i - 1, 1 - buf, lambda copy: copy.wait())
        for_rows(i, buf, lambda copy: copy.wait())


def _scatter(h, g, dest, tm=256):
    n, d = h.shape
    return pl.pallas_call(
        _scatter_kernel,
        grid_spec=pltpu.PrefetchScalarGridSpec(
            num_scalar_prefetch=1,
            grid=(n // tm,),
            in_specs=[pl.BlockSpec((tm, d), lambda i, dest: (i, 0)),
                      pl.BlockSpec((1, d), lambda i, dest: (0, 0))],
            out_specs=pl.BlockSpec(memory_space=pl.ANY),
            scratch_shapes=[pltpu.VMEM((2, tm // SUBLANES, SUBLANES, d), F32),
                            pltpu.SemaphoreType.DMA((2,))],
        ),
        out_shape=jax.ShapeDtypeStruct((2 * n, d), F32),
        compiler_params=_params("arbitrary"),
        name="moe_scatter",
    )(dest, h, g.reshape(1, d))


def _experts_kernel(tile_ref, exp_ref, lo_ref, hi_ref, first_ref, new_ref, valid_ref, slot_ref,
                    ahead_ref, head_ref, xs_ref, wg_hbm, wu_hbm, wd_hbm, o_ref,
                    wg_st, wu_st, wd_st, wg_bf, wu_bf, wd_bf, sem, *, layer):
    w = pl.program_id(0)
    tm = xs_ref.shape[0]
    chunk = 256

    def fetch(e, s):
        return (pltpu.make_async_copy(wg_hbm.at[layer, e], wg_st.at[s], sem.at[s, 0]),
                pltpu.make_async_copy(wu_hbm.at[layer, e], wu_st.at[s], sem.at[s, 1]),
                pltpu.make_async_copy(wd_hbm.at[layer, e], wd_st.at[s], sem.at[s, 2]))

    @pl.when(w == 0)
    def _():
        for copy in fetch(head_ref[0], 0):
            copy.start()

        @pl.when(head_ref[1] >= 0)
        def _():
            for copy in fetch(head_ref[1], 1):
                copy.start()

    @pl.when(new_ref[w] == 1)
    def _():
        s = slot_ref[w]
        for copy in fetch(exp_ref[w], s):
            copy.wait()

        def cast_in(r0):
            rows = pl.ds(r0, chunk)
            wg_bf[rows, :] = wg_st[s, rows, :].astype(BF16)
            wu_bf[rows, :] = wu_st[s, rows, :].astype(BF16)
        _row_loop(wg_bf.shape[0], cast_in, chunk)

        def cast_down(r0):
            rows = pl.ds(r0, chunk)
            wd_bf[rows, :] = wd_st[s, rows, :].astype(BF16)
        _row_loop(wd_bf.shape[0], cast_down, chunk)

        @pl.when(ahead_ref[w] >= 0)
        def _():
            for copy in fetch(ahead_ref[w], s):
                copy.start()

    @pl.when(valid_ref[w] == 1)
    def _():
        row = lax.broadcasted_iota(I32, (tm, 1), 0)
        mine = (row >= lo_ref[w]) & (row < hi_ref[w])
        x = jnp.where(mine, xs_ref[...], 0.0).astype(BF16)
        hg = _bdot(x, wg_bf[...])
        hu = _bdot(x, wu_bf[...])
        act = (hg * jax.nn.sigmoid(hg) * hu).astype(BF16)
        y = _bdot(act, wd_bf[...])

        @pl.when(first_ref[w] == 1)
        def _():
            o_ref[...] = y

        @pl.when(first_ref[w] == 0)
        def _():
            o_ref[...] += y


def _experts(xs, plan, w_gate, w_up, w_down, layer, tm):
    p, d = xs.shape
    f = w_gate.shape[3]
    n_items = plan[0].shape[0]
    return pl.pallas_call(
        functools.partial(_experts_kernel, layer=layer),
        grid_spec=pltpu.PrefetchScalarGridSpec(
            num_scalar_prefetch=len(plan),
            grid=(n_items,),
            in_specs=[
                pl.BlockSpec((tm, d), lambda w, tile, *_: (tile[w], 0)),
                pl.BlockSpec(memory_space=pl.ANY),
                pl.BlockSpec(memory_space=pl.ANY),
                pl.BlockSpec(memory_space=pl.ANY),
            ],
            out_specs=pl.BlockSpec((tm, d), lambda w, tile, *_: (tile[w], 0)),
            scratch_shapes=[
                pltpu.VMEM((2, d, f), F32), pltpu.VMEM((2, d, f), F32), pltpu.VMEM((2, f, d), F32),
                pltpu.VMEM((d, f), BF16), pltpu.VMEM((d, f), BF16), pltpu.VMEM((f, d), BF16),
                pltpu.SemaphoreType.DMA((2, 3)),
            ],
        ),
        out_shape=jax.ShapeDtypeStruct((p, d), F32),
        compiler_params=_params("arbitrary"),
        name="moe_experts",
    )(*plan, xs, w_gate, w_up, w_down)


def _combine_kernel(dest_ref, h_ref, wts_ref, g_ref, ys_ref, o_ref, y_ref, sem, *, final_norm):
    tm = h_ref.shape[0]
    i = pl.program_id(0)
    steps = pl.num_programs(0)
    n = steps * tm
    buf = i % 2

    sub_chunk = ROW_CHUNK // SUBLANES

    def row_copy(step, b, grp, j, pick):
        d = dest_ref[pick * n + step * tm + grp * SUBLANES + j]
        return pltpu.make_async_copy(ys_ref.at[pl.ds(d, 1), :], y_ref.at[b, pick, grp, pl.ds(j, 1), :],
                                     sem.at[b])

    def for_rows(step, b, fn):
        def body(grp, carry):
            for j in range(SUBLANES):
                fn(row_copy(step, b, grp, j, 0))
                fn(row_copy(step, b, grp, j, 1))
            return carry
        lax.fori_loop(0, tm // SUBLANES, body, 0)

    @pl.when(i == 0)
    def _():
        for_rows(0, 0, lambda copy: copy.start())

    @pl.when(i + 1 < steps)
    def _():
        for_rows(i + 1, 1 - buf, lambda copy: copy.start())

    for_rows(i, buf, lambda copy: copy.wait())

    def mix(r0):
        rows = pl.ds(r0, ROW_CHUNK)
        grps = pl.ds(pl.multiple_of(r0 // SUBLANES, sub_chunk), sub_chunk)
        wts = wts_ref[rows, :]
        y0 = y_ref[buf, 0, grps, :, :].reshape(ROW_CHUNK, -1)
        y1 = y_ref[buf, 1, grps, :, :].reshape(ROW_CHUNK, -1)
        out = h_ref[rows, :] + wts[:, 0:1] * y0 + wts[:, 1:2] * y1
        if final_norm:
            out = _rms(out, g_ref[...])
        o_ref[rows, :] = out
    _row_loop(tm, mix)


def _combine(h, wts, ys, dest, g_final, final_norm, tm=256):
    n, d = h.shape
    return pl.pallas_call(
        functools.partial(_combine_kernel, final_norm=final_norm),
        grid_spec=pltpu.PrefetchScalarGridSpec(
            num_scalar_prefetch=1,
            grid=(n // tm,),
            in_specs=[pl.BlockSpec((tm, d), lambda i, dest: (i, 0)),
                      pl.BlockSpec((tm, LANES), lambda i, dest: (i, 0)),
                      pl.BlockSpec((1, d), lambda i, dest: (0, 0)),
                      pl.BlockSpec(memory_space=pl.ANY)],
            out_specs=pl.BlockSpec((tm, d), lambda i, dest: (i, 0)),
            scratch_shapes=[pltpu.VMEM((2, 2, tm // SUBLANES, SUBLANES, d), F32),
                            pltpu.SemaphoreType.DMA((2,))],
        ),
        out_shape=jax.ShapeDtypeStruct((n, d), F32),
        compiler_params=_params("arbitrary"),
        name="moe_combine",
    )(dest, h, wts, g_final.reshape(1, d), ys)


def _moe(h, g, w_group, b_group, w_expert, b_expert, w_gate, w_up, w_down, layer, g_final, final_norm,
         tm=256):
    n = h.shape[0]
    ids, wts = _router(h, g, w_group, b_group, w_expert, b_expert)
    n_items = 2 * n // tm + N_EXPERTS
    dest, plan = _dispatch_plan(ids, tm, n_items)
    xs = _scatter(h, g, dest)
    ys = _experts(xs, plan, w_gate, w_up, w_down, layer, tm)
    return _combine(h, wts, ys, dest, g_final, final_norm)


def _qkv_kernel(*refs, pieces):
    x_refs = refs[:pieces]
    gq_ref, gkv_ref, wq_ref, wkv_ref, o_ref, xq_ref, xkv_ref = refs[pieces:]
    j = pl.program_id(1)
    rp = x_refs[0].shape[1]
    q_blocks = ATTN_OUT // o_ref.shape[1]

    @pl.when(j == 0)
    def _():
        for p, x_ref in enumerate(x_refs):
            def norm(r0, x_ref=x_ref, p=p):
                x = x_ref[0, pl.ds(r0, ROW_CHUNK), :]
                xhat = x * lax.rsqrt(jnp.mean(x * x, axis=-1, keepdims=True) + EPS)
                rows = pl.ds(p * rp + r0, ROW_CHUNK)
                xq_ref[rows, :] = (xhat * gq_ref[...]).astype(BF16)
                xkv_ref[rows, :] = (xhat * gkv_ref[...]).astype(BF16)
            _row_loop(rp, norm)

    @pl.when(j < q_blocks)
    def _():
        o_ref[...] = _bdot(xq_ref[...], wq_ref[...].astype(BF16)).astype(o_ref.dtype)

    @pl.when(j >= q_blocks)
    def _():
        o_ref[...] = _bdot(xkv_ref[...], wkv_ref[...].astype(BF16)).astype(o_ref.dtype)


def _qkv(h3, g_q, g_kv, w_q, w_kv, group, tm=1024, tn=512):
    b, s, d = h3.shape
    r = ATTN_DILATIONS[group]
    sub = s // r
    rp = min(sub, tm)
    pieces = tm // rp
    tiles_per_class = sub // rp
    tiles_per_batch = s // tm
    hv = h3.reshape(b, sub, r * d)
    qb = ATTN_OUT // tn

    def x_map(p):
        def index(i, j):
            bi, ti = i // tiles_per_batch, i % tiles_per_batch
            return (bi, ti % tiles_per_class, (ti // tiles_per_class) * pieces + p)
        return index

    def wq_map(i, j):
        return (0, group * qb + jnp.minimum(j, qb - 1))

    def wkv_map(i, j):
        jj = jnp.maximum(j - qb, 0)
        return (0, (jj // qb) * N_ATTN_GROUPS * qb + group * qb + jj % qb)

    in_specs = [pl.BlockSpec((1, rp, d), x_map(p)) for p in range(pieces)]
    in_specs += [
        pl.BlockSpec((1, d), lambda i, j: (0, 0)),
        pl.BlockSpec((1, d), lambda i, j: (0, 0)),
        pl.BlockSpec((d, tn), wq_map),
        pl.BlockSpec((d, tn), wkv_map),
    ]
    return pl.pallas_call(
        functools.partial(_qkv_kernel, pieces=pieces),
        grid=(b * s // tm, 3 * qb),
        in_specs=in_specs,
        out_specs=pl.BlockSpec((tm, tn), lambda i, j: (i, j)),
        out_shape=jax.ShapeDtypeStruct((b * s, 3 * ATTN_OUT), BF16),
        scratch_shapes=[pltpu.VMEM((tm, d), BF16), pltpu.VMEM((tm, d), BF16)],
        compiler_params=_params("arbitrary", "arbitrary"),
        name=f"qkv_g{group}",
    )(*([hv] * pieces), g_q.reshape(1, d), g_kv.reshape(1, d), w_q, w_kv)


def _dot_t(a, b):
    return lax.dot_general(a, b, (((1,), (1,)), ((), ())), preferred_element_type=F32)


def _attn_kernel(q_ref, kc_ref, vc_ref, kp_ref, vp_ref, o_ref, lse_ref, *, seg_blocks):
    qblocks = q_ref.shape[0] // BAND
    step = pl.program_id(0)
    qi = lax.broadcasted_iota(I32, (BAND, BAND), 0)
    kj = lax.broadcasted_iota(I32, (BAND, BAND), 1)
    cur_mask = kj <= qi
    prev_mask = kj >= qi
    lane = lax.broadcasted_iota(I32, (BAND, LANES), 1)
    scale = HEAD_DIM ** -0.5
    neg = -jnp.inf

    for sb in range(qblocks):
        rows = slice(sb * BAND, (sb + 1) * BAND)
        if seg_blocks == 1 or (sb == 0 and seg_blocks <= qblocks):
            prev = None
        elif sb == 0:
            at_start = (step * qblocks) % seg_blocks == 0
            prev = (kp_ref, vp_ref, slice(0, BAND), jnp.where(at_start, neg, 0.0))
        else:
            prev = (kc_ref, vc_ref, slice((sb - 1) * BAND, sb * BAND), None)
        lse_slab = jnp.zeros((BAND, LANES), F32)
        for h in range(HEADS):
            cols = slice(h * HEAD_DIM, (h + 1) * HEAD_DIM)
            q = q_ref[rows, cols]
            s_c = jnp.where(cur_mask, _dot_t(q, kc_ref[rows, cols]) * scale, neg)
            m = jnp.max(s_c, axis=1, keepdims=True)
            if prev is not None:
                k_ref, v_ref, prow, off = prev
                s_p = jnp.where(prev_mask, _dot_t(q, k_ref[prow, cols]) * scale, neg)
                if off is not None:
                    s_p = s_p + off
                m = jnp.maximum(m, jnp.max(s_p, axis=1, keepdims=True))
            p_c = jnp.exp(s_c - m)
            l = jnp.sum(p_c, axis=1, keepdims=True)
            acc = _bdot(p_c.astype(BF16), vc_ref[rows, cols])
            if prev is not None:
                p_p = jnp.exp(s_p - m)
                l = l + jnp.sum(p_p, axis=1, keepdims=True)
                acc = acc + _bdot(p_p.astype(BF16), v_ref[prow, cols])
            o_ref[rows, cols] = (acc / l).astype(o_ref.dtype)
            lse_slab = jnp.where(lane == h, m + jnp.log(l), lse_slab)
        lse_ref[rows, :] = lse_slab


def _attn(qkv, group, seq, rows_per_step=512):
    n = qkv.shape[0]
    seg_blocks = seq // ATTN_DILATIONS[group] // BAND
    qblocks = rows_per_step // BAND
    return pl.pallas_call(
        functools.partial(_attn_kernel, seg_blocks=seg_blocks),
        grid=(n // rows_per_step,),
        in_specs=[
            pl.BlockSpec((rows_per_step, ATTN_OUT), lambda i: (i, 0)),
            pl.BlockSpec((rows_per_step, ATTN_OUT), lambda i: (i, 1)),
            pl.BlockSpec((rows_per_step, ATTN_OUT), lambda i: (i, 2)),
            pl.BlockSpec((BAND, ATTN_OUT), lambda i: (jnp.maximum(i * qblocks - 1, 0), 1)),
            pl.BlockSpec((BAND, ATTN_OUT), lambda i: (jnp.maximum(i * qblocks - 1, 0), 2)),
        ],
        out_specs=[pl.BlockSpec((rows_per_step, ATTN_OUT), lambda i: (i, 0)),
                   pl.BlockSpec((rows_per_step, LANES), lambda i: (i, 0))],
        out_shape=[jax.ShapeDtypeStruct((n, ATTN_OUT), BF16), jax.ShapeDtypeStruct((n, LANES), F32)],
        compiler_params=_params("arbitrary"),
        name=f"attn_g{group}",
    )(qkv, qkv, qkv, qkv, qkv)


def _merge_out_kernel(*refs, pieces):
    o0_ref, l0_ref, o2_ref, l2_ref, h_ref = refs[:5]
    g1_refs = refs[5:5 + 2 * pieces]
    w_ref, o_ref, m_ref, wbf_ref = refs[5 + 2 * pieces:]
    u = o0_ref.shape[1]
    d = w_ref.shape[1]

    @pl.when((pl.program_id(0) == 0) & (pl.program_id(1) == 0))
    def _():
        wbf_ref[...] = w_ref[...].astype(BF16)

    for p in range(pieces):
        rows = slice(p * u, (p + 1) * u)
        o1_ref, l1_ref = g1_refs[2 * p], g1_refs[2 * p + 1]
        lses = (l0_ref[0, :, p * LANES:(p + 1) * LANES], l1_ref[0], l2_ref[rows, :])
        for h in range(HEADS):
            cols = slice(h * HEAD_DIM, (h + 1) * HEAD_DIM)
            lh = [l[:, h:h + 1] for l in lses]
            mx = jnp.maximum(jnp.maximum(lh[0], lh[1]), lh[2])
            ex = [jnp.exp(l - mx) for l in lh]
            inv = 1.0 / (ex[0] + ex[1] + ex[2])
            vals = (o0_ref[0, :, p * ATTN_OUT + h * HEAD_DIM:p * ATTN_OUT + (h + 1) * HEAD_DIM],
                    o1_ref[0, :, cols], o2_ref[rows, cols])
            merged = sum((e * inv) * v.astype(F32) for e, v in zip(ex, vals))
            m_ref[rows, cols] = merged.astype(BF16)
    y = _bdot(m_ref[...], wbf_ref[...])
    for p in range(pieces):
        cols = slice(p * d, (p + 1) * d)
        o_ref[0, :, cols] = h_ref[0, :, cols] + y[p * u:(p + 1) * u, :]


def _merge_out(h3, attn, w_o, pieces=2):
    b, s, d = h3.shape
    r1, r2 = ATTN_DILATIONS[1], ATTN_DILATIONS[2]
    u = s // r2
    (o0, l0), (o1, l1), (o2, l2) = attn
    o0v = o0.reshape(b, u, r2 * ATTN_OUT)
    l0v = l0.reshape(b, u, r2 * LANES)
    o1v = o1.reshape(b * r1, u, (r2 // r1) * ATTN_OUT)
    l1v = l1.reshape(b * r1, u, (r2 // r1) * LANES)
    hv = h3.reshape(b, u, r2 * d)
    steps = r2 // pieces

    def g1_map(p):
        def index(bi, i):
            m16 = i * pieces + p
            return (bi * r1 + m16 % r1, 0, m16 // r1)
        return index

    in_specs = [
        pl.BlockSpec((1, u, pieces * ATTN_OUT), lambda bi, i: (bi, 0, i)),
        pl.BlockSpec((1, u, pieces * LANES), lambda bi, i: (bi, 0, i)),
        pl.BlockSpec((pieces * u, ATTN_OUT), lambda bi, i: (bi * steps + i, 0)),
        pl.BlockSpec((pieces * u, LANES), lambda bi, i: (bi * steps + i, 0)),
        pl.BlockSpec((1, u, pieces * d), lambda bi, i: (bi, 0, i)),
    ]
    args = [o0v, l0v, o2, l2, hv]
    for p in range(pieces):
        in_specs += [pl.BlockSpec((1, u, ATTN_OUT), g1_map(p)), pl.BlockSpec((1, u, LANES), g1_map(p))]
        args += [o1v, l1v]
    in_specs.append(pl.BlockSpec((ATTN_OUT, d), lambda bi, i: (0, 0)))
    args.append(w_o)
    out = pl.pallas_call(
        functools.partial(_merge_out_kernel, pieces=pieces),
        grid=(b, steps),
        in_specs=in_specs,
        out_specs=pl.BlockSpec((1, u, pieces * d), lambda bi, i: (bi, 0, i)),
        out_shape=jax.ShapeDtypeStruct((b, u, r2 * d), F32),
        scratch_shapes=[pltpu.VMEM((pieces * u, ATTN_OUT), BF16), pltpu.VMEM((ATTN_OUT, d), BF16)],
        compiler_params=_params("arbitrary", "arbitrary"),
        name="attn_merge_out",
    )(*args)
    return out.reshape(b, s, d)


def kernel(x, norm_mix_g, norm_ffn_g, conv_w_in, conv_b_in, conv_w_dw, conv_b_dw, conv_ln_g, conv_ln_b,
           conv_w_out, conv_b_out, norm_kv_g, w_kv, attn_w_q, attn_w_o, router_w_group, router_b_group,
           router_w_expert, router_b_expert, expert_w_gate, expert_w_up, expert_w_down, norm_final_g):
    b, s, d = x.shape
    n = b * s
    xf = x.reshape(n, d)

    u = _glu_in(xf, norm_mix_g[0], conv_w_in[0], conv_b_in[0])
    y = _dwconv(u.reshape(b, s, -1), conv_w_dw[0], conv_b_dw[0])
    h = _ln_out(y.reshape(n, -1), conv_ln_g[0], conv_ln_b[0], conv_w_out[0], conv_b_out[0], xf)
    h = _moe(h, norm_ffn_g[0], router_w_group[0], router_b_group[0], router_w_expert[0],
             router_b_expert[0], expert_w_gate, expert_w_up, expert_w_down, 0,
             norm_final_g, False)

    h3 = h.reshape(b, s, d)
    attn = []
    for group in range(N_ATTN_GROUPS):
        qkv = _qkv(h3, norm_mix_g[1], norm_kv_g, attn_w_q[0], w_kv, group)
        attn.append(_attn(qkv, group, s))
    h = _merge_out(h3, attn, attn_w_o[0]).reshape(n, d)
    out = _moe(h, norm_ffn_g[1], router_w_group[1], router_b_group[1], router_w_expert[1],
               router_b_expert[1], expert_w_gate, expert_w_up, expert_w_down, 1,
               norm_final_g, True)
    return out.reshape(b, s, d)
```

```python
import functools

import jax
import jax.numpy as jnp
from jax import lax
from jax.experimental import pallas as pl
from jax.experimental.pallas import tpu as pltpu

F32 = jnp.float32
BF16 = jnp.bfloat16
I32 = jnp.int32

EPS = 1e-6
LANES = 128
V7X_VMEM_BYTES = 64 * 1024 * 1024
VMEM_LIMIT = V7X_VMEM_BYTES - 8 * 1024 * 1024

CONV_WIDTH = 31
HALO = 32
ATTN_DILATIONS = (1, 4, 16)
N_ATTN_GROUPS = 3
N_DILATED = 2
HEADS = 8
HEAD_DIM = 128
BAND = 128
BAND_LOG2 = 7
ATTN_OUT = HEADS * HEAD_DIM
N_GROUPS = 4
EXPERTS_PER_GROUP = 8
N_EXPERTS = N_GROUPS * EXPERTS_PER_GROUP

ROW_CHUNK = 32
SUBLANES = 8


def _params(*sem):
    return pltpu.CompilerParams(dimension_semantics=sem, vmem_limit_bytes=VMEM_LIMIT)


def _row_loop(n_rows, body, chunk=ROW_CHUNK):
    def step(c, carry):
        body(pl.multiple_of(c * chunk, chunk))
        return carry
    lax.fori_loop(0, n_rows // chunk, step, 0)


def _rms(x, g):
    ms = jnp.mean(x * x, axis=-1, keepdims=True)
    return x * lax.rsqrt(ms + EPS) * g


def _bdot(a, b):
    return jnp.dot(a, b, preferred_element_type=F32)


def _glu_in_kernel(x_ref, g_ref, wv_ref, wg_ref, bv_ref, bg_ref, o_ref, xn_ref):
    tm = x_ref.shape[0]

    @pl.when(pl.program_id(1) == 0)
    def _():
        def norm(r0):
            rows = pl.ds(r0, ROW_CHUNK)
            xn_ref[rows, :] = _rms(x_ref[rows, :], g_ref[...]).astype(BF16)
        _row_loop(tm, norm)

    xn = xn_ref[...]
    val = _bdot(xn, wv_ref[...].astype(BF16)) + bv_ref[...]
    gate = _bdot(xn, wg_ref[...].astype(BF16)) + bg_ref[...]
    o_ref[...] = (val * jax.nn.sigmoid(gate)).astype(o_ref.dtype)


def _glu_in(x, g, w_in, b_in, tm=1024, tn=512):
    n, d = x.shape
    c = w_in.shape[1] // 2
    nj = c // tn
    b2 = b_in.reshape(1, 2 * c)
    return pl.pallas_call(
        _glu_in_kernel,
        grid=(n // tm, nj),
        in_specs=[
            pl.BlockSpec((tm, d), lambda i, j: (i, 0)),
            pl.BlockSpec((1, d), lambda i, j: (0, 0)),
            pl.BlockSpec((d, tn), lambda i, j: (0, j)),
            pl.BlockSpec((d, tn), lambda i, j: (0, j + nj)),
            pl.BlockSpec((1, tn), lambda i, j: (0, j)),
            pl.BlockSpec((1, tn), lambda i, j: (0, j + nj)),
        ],
        out_specs=pl.BlockSpec((tm, tn), lambda i, j: (i, j)),
        out_shape=jax.ShapeDtypeStruct((n, c), BF16),
        scratch_shapes=[pltpu.VMEM((tm, d), BF16)],
        compiler_params=_params("arbitrary", "arbitrary"),
        name="glu_in",
    )(x, g.reshape(1, d), w_in, w_in, b2, b2)


def _dwconv_kernel(cur_ref, halo_ref, w_ref, b_ref, o_ref, buf_ref):
    ts, cw = cur_ref.shape[1], cur_ref.shape[2]
    rw = 64
    keep = jnp.where(pl.program_id(1) > 0, 1.0, 0.0)
    buf_ref[0, 0:HALO, :] = halo_ref[0].astype(F32) * keep
    buf_ref[0, HALO:, :] = cur_ref[0].astype(F32)
    shifted_rows = ts + HALO - SUBLANES
    for s in range(1, SUBLANES):
        buf_ref[s, 0:shifted_rows, :] = buf_ref[0, s:s + shifted_rows, :]
    first = HALO - (CONV_WIDTH - 1)
    for c0 in range(0, cw, LANES):
        cols = slice(c0, c0 + LANES)
        for r0 in range(0, ts, rw):
            acc = jnp.broadcast_to(b_ref[:, cols], (rw, LANES))
            for k in range(CONV_WIDTH):
                s = (first + k) % SUBLANES
                start = r0 + first + k - s
                acc = acc + w_ref[k:k + 1, cols] * buf_ref[s, start:start + rw, cols]
            o_ref[0, r0:r0 + rw, cols] = acc.astype(o_ref.dtype)


def _dwconv(u, w_dw, b_dw, ts=256, cw=512):
    b, s, c = u.shape
    hb = ts // HALO
    return pl.pallas_call(
        _dwconv_kernel,
        grid=(b, s // ts, c // cw),
        in_specs=[
            pl.BlockSpec((1, ts, cw), lambda bi, si, ci: (bi, si, ci)),
            pl.BlockSpec((1, HALO, cw), lambda bi, si, ci: (bi, jnp.maximum(si * hb - 1, 0), ci)),
            pl.BlockSpec((CONV_WIDTH, cw), lambda bi, si, ci: (0, ci)),
            pl.BlockSpec((1, cw), lambda bi, si, ci: (0, ci)),
        ],
        out_specs=pl.BlockSpec((1, ts, cw), lambda bi, si, ci: (bi, si, ci)),
        out_shape=jax.ShapeDtypeStruct((b, s, c), BF16),
        scratch_shapes=[pltpu.VMEM((SUBLANES, ts + HALO, cw), F32)],
        compiler_params=_params("arbitrary", "arbitrary", "arbitrary"),
        name="dwconv",
    )(u, u, w_dw, b_dw.reshape(1, c))


def _ln_out_kernel(y_ref, lg_ref, lb_ref, w_ref, b_ref, res_ref, o_ref, a_ref):
    tm = y_ref.shape[0]

    @pl.when(pl.program_id(1) == 0)
    def _():
        def norm(r0):
            rows = pl.ds(r0, ROW_CHUNK)
            y = y_ref[rows, :].astype(F32)
            mu = jnp.mean(y, axis=-1, keepdims=True)
            yc = y - mu
            var = jnp.mean(yc * yc, axis=-1, keepdims=True)
            z = yc * lax.rsqrt(var + EPS) * lg_ref[...] + lb_ref[...]
            a_ref[rows, :] = (z * jax.nn.sigmoid(z)).astype(BF16)
        _row_loop(tm, norm)

    o_ref[...] = res_ref[...] + _bdot(a_ref[...], w_ref[...].astype(BF16)) + b_ref[...]


def _ln_out(y, ln_g, ln_b, w_out, b_out, res, tm=1024, tn=512):
    n, c = y.shape
    d = w_out.shape[1]
    return pl.pallas_call(
        _ln_out_kernel,
        grid=(n // tm, d // tn),
        in_specs=[
            pl.BlockSpec((tm, c), lambda i, j: (i, 0)),
            pl.BlockSpec((1, c), lambda i, j: (0, 0)),
            pl.BlockSpec((1, c), lambda i, j: (0, 0)),
            pl.BlockSpec((c, tn), lambda i, j: (0, j)),
            pl.BlockSpec((1, tn), lambda i, j: (0, j)),
            pl.BlockSpec((tm, tn), lambda i, j: (i, j)),
        ],
        out_specs=pl.BlockSpec((tm, tn), lambda i, j: (i, j)),
        out_shape=jax.ShapeDtypeStruct((n, d), F32),
        scratch_shapes=[pltpu.VMEM((tm, c), BF16)],
        compiler_params=_params("arbitrary", "arbitrary"),
        name="ln_out",
    )(y, ln_g.reshape(1, c), ln_b.reshape(1, c), w_out, b_out.reshape(1, d), res)


def _split_bf16(a):
    hi = a.astype(BF16)
    lo = (a - hi.astype(F32)).astype(BF16)
    return hi, lo


def _router_kernel(h_ref, g_ref, w_ref, b_ref, ids_ref, wts_ref):
    t = _rms(h_ref[...], g_ref[...])
    t_hi, t_lo = _split_bf16(t)
    w_hi, w_lo = _split_bf16(w_ref[...])
    logits = _bdot(t_hi, w_hi) + _bdot(t_hi, w_lo) + _bdot(t_lo, w_hi) + b_ref[...]

    lane = lax.broadcasted_iota(I32, logits.shape, 1)
    lanef = lane.astype(F32)
    neg = -jnp.inf
    is_group = lane < N_GROUPS
    glog = jnp.where(is_group, logits, neg)
    gmax = jnp.max(glog, axis=1, keepdims=True)
    gi = jnp.min(jnp.where(glog == gmax, lanef, float(LANES)), axis=1, keepdims=True)
    gsum = jnp.sum(jnp.where(is_group, jnp.exp(logits - gmax), 0.0), axis=1, keepdims=True)
    gp = 1.0 / gsum

    lo_lane = N_GROUPS + gi * EXPERTS_PER_GROUP
    in_sel = (lanef >= lo_lane) & (lanef < lo_lane + EXPERTS_PER_GROUP)
    sel = jnp.where(in_sel, logits, neg)
    m1 = jnp.max(sel, axis=1, keepdims=True)
    i1 = jnp.min(jnp.where(sel == m1, lanef, float(LANES)), axis=1, keepdims=True)
    sel2 = jnp.where(lanef == i1, neg, sel)
    m2 = jnp.max(sel2, axis=1, keepdims=True)
    i2 = jnp.min(jnp.where(sel2 == m2, lanef, float(LANES)), axis=1, keepdims=True)
    e2 = jnp.exp(m2 - m1)
    w1 = gp / (1.0 + e2)
    w2 = gp * e2 / (1.0 + e2)

    ids = jnp.where(lane == 0, i1 - N_GROUPS, jnp.where(lane == 1, i2 - N_GROUPS, 0.0))
    ids_ref[...] = ids.astype(I32)
    wts_ref[...] = jnp.where(lane == 0, w1, jnp.where(lane == 1, w2, 0.0))


def _router(h, g, w_group, b_group, w_expert, b_expert, tm=512):
    n, d = h.shape
    w_e = jnp.transpose(w_expert, (1, 0, 2)).reshape(d, N_EXPERTS)
    pad = LANES - N_GROUPS - N_EXPERTS
    wr = jnp.concatenate([w_group, w_e, jnp.zeros((d, pad), F32)], axis=1)
    br = jnp.concatenate([b_group, b_expert.reshape(N_EXPERTS), jnp.zeros((pad,), F32)]).reshape(1, LANES)
    return pl.pallas_call(
        _router_kernel,
        grid=(n // tm,),
        in_specs=[
            pl.BlockSpec((tm, d), lambda i: (i, 0)),
            pl.BlockSpec((1, d), lambda i: (0, 0)),
            pl.BlockSpec((d, LANES), lambda i: (0, 0)),
            pl.BlockSpec((1, LANES), lambda i: (0, 0)),
        ],
        out_specs=[pl.BlockSpec((tm, LANES), lambda i: (i, 0)),
                   pl.BlockSpec((tm, LANES), lambda i: (i, 0))],
        out_shape=[jax.ShapeDtypeStruct((n, LANES), I32), jax.ShapeDtypeStruct((n, LANES), F32)],
        compiler_params=_params("arbitrary"),
        name="router",
    )(h, g.reshape(1, d), wr, br)


def _dispatch_plan(ids, tm, n_items):
    e_flat = jnp.concatenate([ids[:, 0], ids[:, 1]])
    onehot = (e_flat[:, None] == jnp.arange(N_EXPERTS, dtype=I32)[None, :]).astype(I32)
    csum = jnp.cumsum(onehot, axis=0)
    rank = jnp.sum(onehot * csum, axis=1) - 1
    counts = csum[-1]
    ends = jnp.cumsum(counts)
    starts = ends - counts
    dest = jnp.sum(onehot * starts[None, :], axis=1) + rank

    first_tile = starts // tm
    last_tile = jnp.maximum(ends - 1, 0) // tm
    items = jnp.where(counts > 0, last_tile - first_tile + 1, 0)
    item_end = jnp.cumsum(items)
    item_start = item_end - items
    total = item_end[-1]
    w = jnp.arange(n_items, dtype=I32)
    valid = w < total
    wc = jnp.minimum(w, total - 1)
    e_w = jnp.searchsorted(item_end, wc, side="right").astype(I32)
    tile_w = first_tile[e_w] + (wc - item_start[e_w])
    lo = jnp.maximum(starts[e_w], tile_w * tm) - tile_w * tm
    hi = jnp.minimum(ends[e_w], (tile_w + 1) * tm) - tile_w * tm
    lo = jnp.where(valid, lo, 0)
    hi = jnp.where(valid, hi, 0)
    prev_tile = jnp.concatenate([jnp.full((1,), -1, I32), tile_w[:-1]])
    prev_e = jnp.concatenate([jnp.full((1,), -1, I32), e_w[:-1]])
    first = (valid & (tile_w != prev_tile)).astype(I32)
    new_e = (valid & (e_w != prev_e)).astype(I32)

    present = counts > 0
    n_present = jnp.sum(present.astype(I32))
    order = jnp.argsort(jnp.logical_not(present), stable=True).astype(I32)
    ordinal = jnp.cumsum(present.astype(I32)) - 1
    k_w = ordinal[e_w]
    slot = k_w % 2
    ahead = jnp.where(k_w + 2 < n_present, order[jnp.minimum(k_w + 2, N_EXPERTS - 1)], -1)
    head = jnp.stack([order[0], jnp.where(n_present > 1, order[1], -1)])
    plan = (tile_w.astype(I32), e_w, lo.astype(I32), hi.astype(I32), first, new_e, valid.astype(I32),
            slot.astype(I32), ahead.astype(I32), head.astype(I32))
    return dest.astype(I32), plan


def _scatter_kernel(dest_ref, h_ref, g_ref, xs_ref, t_ref, sem):
    tm = h_ref.shape[0]
    i = pl.program_id(0)
    steps = pl.num_programs(0)
    n = steps * tm
    buf = i % 2
    sub_chunk = ROW_CHUNK // SUBLANES

    def row_copy(step, b, grp, j, pick):
        d = dest_ref[pick * n + step * tm + grp * SUBLANES + j]
        return pltpu.make_async_copy(t_ref.at[b, grp, pl.ds(j, 1), :], xs_ref.at[pl.ds(d, 1), :], sem.at[b])

    def for_rows(step, b, fn):
        def body(grp, carry):
            for j in range(SUBLANES):
                fn(row_copy(step, b, grp, j, 0))
                fn(row_copy(step, b, grp, j, 1))
            return carry
        lax.fori_loop(0, tm // SUBLANES, body, 0)

    @pl.when(i >= 2)
    def _():
        for_rows(i - 2, buf, lambda copy: copy.wait())

    def norm(r0):
        t = _rms(h_ref[pl.ds(r0, ROW_CHUNK), :], g_ref[...])
        grps = pl.ds(pl.multiple_of(r0 // SUBLANES, sub_chunk), sub_chunk)
        t_ref[buf, grps, :, :] = t.reshape(sub_chunk, SUBLANES, t.shape[-1])
    _row_loop(tm, norm)

    for_rows(i, buf, lambda copy: copy.start())

    @pl.when(i == steps - 1)
    def _():
        @pl.when(i >= 1)
        def _():
            for_rows(i - 1, 1 - buf, lambda copy: copy.wait())
        for_rows(i, buf, lambda copy: copy.wait())


def _scatter(h, g, dest, tm=256):
    n, d = h.shape
    return pl.pallas_call(
        _scatter_kernel,
        grid_spec=pltpu.PrefetchScalarGridSpec(
            num_scalar_prefetch=1,
            grid=(n // tm,),
            in_specs=[pl.BlockSpec((tm, d), lambda i, dest: (i, 0)),
                      pl.BlockSpec((1, d), lambda i, dest: (0, 0))],
            out_specs=pl.BlockSpec(memory_space=pl.ANY),
            scratch_shapes=[pltpu.VMEM((2, tm // SUBLANES, SUBLANES, d), F32),
                            pltpu.SemaphoreType.DMA((2,))],
        ),
        out_shape=jax.ShapeDtypeStruct((2 * n, d), F32),
        compiler_params=_params("arbitrary"),
        name="moe_scatter",
    )(dest, h, g.reshape(1, d))


def _experts_kernel(tile_ref, exp_ref, lo_ref, hi_ref, first_ref, new_ref, valid_ref, slot_ref,
                    ahead_ref, head_ref, xs_ref, wg_hbm, wu_hbm, wd_hbm, o_ref,
                    wg_st, wu_st, wd_st, wg_bf, wu_bf, wd_bf, sem, *, layer):
    w = pl.program_id(0)
    tm = xs_ref.shape[0]
    chunk = 256

    def fetch(e, s):
        return (pltpu.make_async_copy(wg_hbm.at[layer, e], wg_st.at[s], sem.at[s, 0]),
                pltpu.make_async_copy(wu_hbm.at[layer, e], wu_st.at[s], sem.at[s, 1]),
                pltpu.make_async_copy(wd_hbm.at[layer, e], wd_st.at[s], sem.at[s, 2]))

    @pl.when(w == 0)
    def _():
        for copy in fetch(head_ref[0], 0):
            copy.start()

        @pl.when(head_ref[1] >= 0)
        def _():
            for copy in fetch(head_ref[1], 1):
                copy.start()

    @pl.when(new_ref[w] == 1)
    def _():
        s = slot_ref[w]
        for copy in fetch(exp_ref[w], s):
            copy.wait()

        def cast_in(r0):
            rows = pl.ds(r0, chunk)
            wg_bf[rows, :] = wg_st[s, rows, :].astype(BF16)
            wu_bf[rows, :] = wu_st[s, rows, :].astype(BF16)
        _row_loop(wg_bf.shape[0], cast_in, chunk)

        def cast_down(r0):
            rows = pl.ds(r0, chunk)
            wd_bf[rows, :] = wd_st[s, rows, :].astype(BF16)
        _row_loop(wd_bf.shape[0], cast_down, chunk)

        @pl.when(ahead_ref[w] >= 0)
        def _():
            for copy in fetch(ahead_ref[w], s):
                copy.start()

    @pl.when(valid_ref[w] == 1)
    def _():
        row = lax.broadcasted_iota(I32, (tm, 1), 0)
        mine = (row >= lo_ref[w]) & (row < hi_ref[w])
        x = jnp.where(mine, xs_ref[...], 0.0).astype(BF16)
        hg = _bdot(x, wg_bf[...])
        hu = _bdot(x, wu_bf[...])
        act = (hg * jax.nn.sigmoid(hg) * hu).astype(BF16)
        y = _bdot(act, wd_bf[...])

        @pl.when(first_ref[w] == 1)
        def _():
            o_ref[...] = y

        @pl.when(first_ref[w] == 0)
        def _():
            o_ref[...] += y


def _experts(xs, plan, w_gate, w_up, w_down, layer, tm):
    p, d = xs.shape
    f = w_gate.shape[3]
    n_items = plan[0].shape[0]
    return pl.pallas_call(
        functools.partial(_experts_kernel, layer=layer),
        grid_spec=pltpu.PrefetchScalarGridSpec(
            num_scalar_prefetch=len(plan),
            grid=(n_items,),
            in_specs=[
                pl.BlockSpec((tm, d), lambda w, tile, *_: (tile[w], 0)),
                pl.BlockSpec(memory_space=pl.ANY),
                pl.BlockSpec(memory_space=pl.ANY),
                pl.BlockSpec(memory_space=pl.ANY),
            ],
            out_specs=pl.BlockSpec((tm, d), lambda w, tile, *_: (tile[w], 0)),
            scratch_shapes=[
                pltpu.VMEM((2, d, f), F32), pltpu.VMEM((2, d, f), F32), pltpu.VMEM((2, f, d), F32),
                pltpu.VMEM((d, f), BF16), pltpu.VMEM((d, f), BF16), pltpu.VMEM((f, d), BF16),
                pltpu.SemaphoreType.DMA((2, 3)),
            ],
        ),
        out_shape=jax.ShapeDtypeStruct((p, d), F32),
        compiler_params=_params("arbitrary"),
        name="moe_experts",
    )(*plan, xs, w_gate, w_up, w_down)


def _combine_kernel(dest_ref, h_ref, wts_ref, g_ref, ys_ref, o_ref, y_ref, sem, *, final_norm):
    tm = h_ref.shape[0]
    i = pl.program_id(0)
    steps = pl.num_programs(0)
    n = steps * tm
    buf = i % 2

    sub_chunk = ROW_CHUNK // SUBLANES

    def row_copy(step, b, grp, j, pick):
        d = dest_ref[pick * n + step * tm + grp * SUBLANES + j]
        return pltpu.make_async_copy(ys_ref.at[pl.ds(d, 1), :], y_ref.at[b, pick, grp, pl.ds(j, 1), :],
                                     sem.at[b])

    def for_rows(step, b, fn):
        def body(grp, carry):
            for j in range(SUBLANES):
                fn(row_copy(step, b, grp, j, 0))
                fn(row_copy(step, b, grp, j, 1))
            return carry
        lax.fori_loop(0, tm // SUBLANES, body, 0)

    @pl.when(i == 0)
    def _():
        for_rows(0, 0, lambda copy: copy.start())

    @pl.when(i + 1 < steps)
    def _():
        for_rows(i + 1, 1 - buf, lambda copy: copy.start())

    for_rows(i, buf, lambda copy: copy.wait())

    def mix(r0):
        rows = pl.ds(r0, ROW_CHUNK)
        grps = pl.ds(pl.multiple_of(r0 // SUBLANES, sub_chunk), sub_chunk)
        wts = wts_ref[rows, :]
        y0 = y_ref[buf, 0, grps, :, :].reshape(ROW_CHUNK, -1)
        y1 = y_ref[buf, 1, grps, :, :].reshape(ROW_CHUNK, -1)
        out = h_ref[rows, :] + wts[:, 0:1] * y0 + wts[:, 1:2] * y1
        if final_norm:
            out = _rms(out, g_ref[...])
        o_ref[rows, :] = out
    _row_loop(tm, mix)


def _combine(h, wts, ys, dest, g_final, final_norm, tm=256):
    n, d = h.shape
    return pl.pallas_call(
        functools.partial(_combine_kernel, final_norm=final_norm),
        grid_spec=pltpu.PrefetchScalarGridSpec(
            num_scalar_prefetch=1,
            grid=(n // tm,),
            in_specs=[pl.BlockSpec((tm, d), lambda i, dest: (i, 0)),
                      pl.BlockSpec((tm, LANES), lambda i, dest: (i, 0)),
                      pl.BlockSpec((1, d), lambda i, dest: (0, 0)),
                      pl.BlockSpec(memory_space=pl.ANY)],
            out_specs=pl.BlockSpec((tm, d), lambda i, dest: (i, 0)),
            scratch_shapes=[pltpu.VMEM((2, 2, tm // SUBLANES, SUBLANES, d), F32),
                            pltpu.SemaphoreType.DMA((2,))],
        ),
        out_shape=jax.ShapeDtypeStruct((n, d), F32),
        compiler_params=_params("arbitrary"),
        name="moe_combine",
    )(dest, h, wts, g_final.reshape(1, d), ys)


def _moe(h, g, w_group, b_group, w_expert, b_expert, w_gate, w_up, w_down, layer, g_final, final_norm,
         tm=256):
    n = h.shape[0]
    ids, wts = _router(h, g, w_group, b_group, w_expert, b_expert)
    n_items = 2 * n // tm + N_EXPERTS
    dest, plan = _dispatch_plan(ids, tm, n_items)
    xs = _scatter(h, g, dest)
    ys = _experts(xs, plan, w_gate, w_up, w_down, layer, tm)
    return _combine(h, wts, ys, dest, g_final, final_norm)


def _qkv_kernel(*refs, pieces):
    x_refs = refs[:pieces]
    gq_ref, gkv_ref, wq_ref, wkv_ref, o_ref, xq_ref, xkv_ref = refs[pieces:]
    j = pl.program_id(1)
    rp = x_refs[0].shape[1]
    q_blocks = ATTN_OUT // o_ref.shape[1]

    @pl.when(j == 0)
    def _():
        for p, x_ref in enumerate(x_refs):
            def norm(r0, x_ref=x_ref, p=p):
                x = x_ref[0, pl.ds(r0, ROW_CHUNK), :]
                xhat = x * lax.rsqrt(jnp.mean(x * x, axis=-1, keepdims=True) + EPS)
                rows = pl.ds(p * rp + r0, ROW_CHUNK)
                xq_ref[rows, :] = (xhat * gq_ref[...]).astype(BF16)
                xkv_ref[rows, :] = (xhat * gkv_ref[...]).astype(BF16)
            _row_loop(rp, norm)

    @pl.when(j < q_blocks)
    def _():
        o_ref[...] = _bdot(xq_ref[...], wq_ref[...].astype(BF16)).astype(o_ref.dtype)

    @pl.when(j >= q_blocks)
    def _():
        o_ref[...] = _bdot(xkv_ref[...], wkv_ref[...].astype(BF16)).astype(o_ref.dtype)


def _qkv(h3, g_q, g_kv, w_q, w_kv, group, tm=1024, tn=512):
    b, s, d = h3.shape
    r = ATTN_DILATIONS[group]
    sub = s // r
    rp = min(sub, tm)
    pieces = tm // rp
    tiles_per_class = sub // rp
    tiles_per_batch = s // tm
    hv = h3.reshape(b, sub, r * d)
    qb = ATTN_OUT // tn

    def x_map(p):
        def index(i, j):
            bi, ti = i // tiles_per_batch, i % tiles_per_batch
            return (bi, ti % tiles_per_class, (ti // tiles_per_class) * pieces + p)
        return index

    def wq_map(i, j):
        return (0, group * qb + jnp.minimum(j, qb - 1))

    def wkv_map(i, j):
        jj = jnp.maximum(j - qb, 0)
        return (0, (jj // qb) * N_ATTN_GROUPS * qb + group * qb + jj % qb)

    in_specs = [pl.BlockSpec((1, rp, d), x_map(p)) for p in range(pieces)]
    in_specs += [
        pl.BlockSpec((1, d), lambda i, j: (0, 0)),
        pl.BlockSpec((1, d), lambda i, j: (0, 0)),
        pl.BlockSpec((d, tn), wq_map),
        pl.BlockSpec((d, tn), wkv_map),
    ]
    return pl.pallas_call(
        functools.partial(_qkv_kernel, pieces=pieces),
        grid=(b * s // tm, 3 * qb),
        in_specs=in_specs,
        out_specs=pl.BlockSpec((tm, tn), lambda i, j: (i, j)),
        out_shape=jax.ShapeDtypeStruct((b * s, 3 * ATTN_OUT), BF16),
        scratch_shapes=[pltpu.VMEM((tm, d), BF16), pltpu.VMEM((tm, d), BF16)],
        compiler_params=_params("arbitrary", "arbitrary"),
        name=f"qkv_g{group}",
    )(*([hv] * pieces), g_q.reshape(1, d), g_kv.reshape(1, d), w_q, w_kv)


def _dot_t(a, b):
    return lax.dot_general(a, b, (((1,), (1,)), ((), ())), preferred_element_type=F32)


def _lane_slab(columns):
    lane = lax.broadcasted_iota(I32, (columns[0].shape[0], LANES), 1)
    slab = jnp.zeros(lane.shape, F32)
    for h, col in enumerate(columns):
        slab = jnp.where(lane == h, col, slab)
    return slab


def _attn_kernel(q_ref, kc_ref, vc_ref, kp_ref, vp_ref, o_ref, lse_ref, *, seg_blocks):
    qblocks = q_ref.shape[0] // BAND
    step = pl.program_id(0)
    banded = seg_blocks > 1
    assert not banded or seg_blocks % qblocks == 0
    n_keys = 2 * BAND if banded else BAND
    qi = lax.broadcasted_iota(I32, (BAND, n_keys), 0)
    kj = lax.broadcasted_iota(I32, (BAND, n_keys), 1)
    mask = (kj >= qi) & (kj <= qi + BAND) if banded else kj <= qi
    scale = HEAD_DIM ** -0.5
    neg = -jnp.inf
    heads = [slice(h * HEAD_DIM, (h + 1) * HEAD_DIM) for h in range(HEADS)]

    for sb in range(qblocks):
        rows = slice(sb * BAND, (sb + 1) * BAND)
        if not banded:
            scores = [_dot_t(q_ref[rows, c], kc_ref[rows, c]) for c in heads]
        elif sb > 0:
            krows = slice((sb - 1) * BAND, (sb + 1) * BAND)
            scores = [_dot_t(q_ref[rows, c], kc_ref[krows, c]) for c in heads]
        else:
            scores = [jnp.concatenate([_dot_t(q_ref[rows, c], kp_ref[:, c]),
                                       _dot_t(q_ref[rows, c], kc_ref[rows, c])], axis=1) for c in heads]
        scores = [jnp.where(mask, s * scale, neg) for s in scores]
        if banded and sb == 0:
            at_start = (step * qblocks) % seg_blocks == 0
            drop = jnp.where(kj < BAND, jnp.where(at_start, neg, 0.0), 0.0)
            scores = [s + drop for s in scores]
        ms = [jnp.max(s, axis=1, keepdims=True) for s in scores]
        ps = [jnp.exp(s - m) for s, m in zip(scores, ms)]
        ls = [jnp.sum(p, axis=1, keepdims=True) for p in ps]
        pb = [p.astype(BF16) for p in ps]
        if not banded:
            accs = [_bdot(p, vc_ref[rows, c]) for p, c in zip(pb, heads)]
        elif sb > 0:
            accs = [_bdot(p, vc_ref[krows, c]) for p, c in zip(pb, heads)]
        else:
            accs = [_bdot(p[:, :BAND], vp_ref[:, c]) + _bdot(p[:, BAND:], vc_ref[rows, c])
                    for p, c in zip(pb, heads)]
        for acc, l, c in zip(accs, ls, heads):
            o_ref[rows, c] = (acc / l).astype(o_ref.dtype)
        lse_ref[rows, :] = _lane_slab([m + jnp.log(l) for m, l in zip(ms, ls)])


def _attn(qkv, cols, seg_blocks, out_dtype, name, rows_per_step=512):
    n = qkv.shape[0]
    cq, ck, cv = cols
    qblocks = rows_per_step // BAND
    return pl.pallas_call(
        functools.partial(_attn_kernel, seg_blocks=seg_blocks),
        grid=(n // rows_per_step,),
        in_specs=[
            pl.BlockSpec((rows_per_step, ATTN_OUT), lambda i: (i, cq)),
            pl.BlockSpec((rows_per_step, ATTN_OUT), lambda i: (i, ck)),
            pl.BlockSpec((rows_per_step, ATTN_OUT), lambda i: (i, cv)),
            pl.BlockSpec((BAND, ATTN_OUT), lambda i: (jnp.maximum(i * qblocks - 1, 0), ck)),
            pl.BlockSpec((BAND, ATTN_OUT), lambda i: (jnp.maximum(i * qblocks - 1, 0), cv)),
        ],
        out_specs=[pl.BlockSpec((rows_per_step, ATTN_OUT), lambda i: (i, 0)),
                   pl.BlockSpec((rows_per_step, LANES), lambda i: (i, 0))],
        out_shape=[jax.ShapeDtypeStruct((n, ATTN_OUT), out_dtype), jax.ShapeDtypeStruct((n, LANES), F32)],
        compiler_params=_params("arbitrary"),
        name=name,
    )(qkv, qkv, qkv, qkv, qkv)


def _attn_skew_kernel(q_ref, k_ref, v_ref, o_ref, lse_ref):
    subs = q_ref.shape[0] // BAND
    keys = k_ref.shape[0]
    u = lax.broadcasted_iota(I32, (BAND, keys), 0)
    kc = lax.broadcasted_iota(I32, (BAND, keys), 1)
    base = subs * (u - (kc & (BAND - 1))) - lax.shift_right_logical(kc, BAND_LOG2)
    scale = HEAD_DIM ** -0.5
    neg = -jnp.inf
    heads = [slice(h * HEAD_DIM, (h + 1) * HEAD_DIM) for h in range(HEADS)]
    for s in range(subs):
        rows = slice(s * BAND, (s + 1) * BAND)
        delta = base + s
        mask = (delta >= 0) & (delta <= BAND)
        scores = [jnp.where(mask, _dot_t(q_ref[rows, c], k_ref[:, c]) * scale, neg) for c in heads]
        ms = [jnp.max(sc, axis=1, keepdims=True) for sc in scores]
        ps = [jnp.exp(sc - m) for sc, m in zip(scores, ms)]
        ls = [jnp.sum(p, axis=1, keepdims=True) for p in ps]
        accs = [_bdot(p.astype(BF16), v_ref[:, c]) for p, c in zip(ps, heads)]
        for acc, l, c in zip(accs, ls, heads):
            o_ref[rows, c] = (acc / l).astype(o_ref.dtype)
        lse_ref[rows, :] = _lane_slab([m + jnp.log(l) for m, l in zip(ms, ls)])


def _attn_skew(qkv, cols, out_dtype, name):
    n = qkv.shape[0]
    cq, ck, cv = cols
    rows = (ATTN_DILATIONS[2] // ATTN_DILATIONS[1]) * BAND
    return pl.pallas_call(
        _attn_skew_kernel,
        grid=(n // rows,),
        in_specs=[pl.BlockSpec((rows, ATTN_OUT), lambda i: (i, cq)),
                  pl.BlockSpec((rows, ATTN_OUT), lambda i: (i, ck)),
                  pl.BlockSpec((rows, ATTN_OUT), lambda i: (i, cv))],
        out_specs=[pl.BlockSpec((rows, ATTN_OUT), lambda i: (i, 0)),
                   pl.BlockSpec((rows, LANES), lambda i: (i, 0))],
        out_shape=[jax.ShapeDtypeStruct((n, ATTN_OUT), out_dtype), jax.ShapeDtypeStruct((n, LANES), F32)],
        compiler_params=_params("arbitrary"),
        name=name,
    )(qkv, qkv, qkv)


def _slot_class(c):
    return (c % 4) * 4 + c // 4


def _qkv_cls_kernel(h_hbm, gq_ref, gkv_ref, wq_ref, wkv_ref, o_ref, x_ref, xq_ref, xkv_ref, sem,
                    *, tiles_per_batch):
    i, j = pl.program_id(0), pl.program_id(1)
    tiles = pl.num_programs(0)
    tm = x_ref.shape[1]
    per_class = h_hbm.shape[0] // (tiles // tiles_per_batch)
    slots = tm // per_class
    q_blocks = N_DILATED * ATTN_OUT // o_ref.shape[1]

    def fetch(tile, buf, fn):
        bi, ti = tile // tiles_per_batch, tile % tiles_per_batch
        for c in range(slots):
            cls = _slot_class(ti * slots + c)
            fn(pltpu.make_async_copy(h_hbm.at[pl.ds(bi * per_class, per_class), cls, :],
                                     x_ref.at[buf, pl.ds(c * per_class, per_class), :],
                                     sem.at[buf]))

    @pl.when(j == 0)
    def _():
        buf = i % 2

        @pl.when(i == 0)
        def _():
            fetch(0, 0, lambda copy: copy.start())

        @pl.when(i + 1 < tiles)
        def _():
            fetch(i + 1, 1 - buf, lambda copy: copy.start())

        fetch(i, buf, lambda copy: copy.wait())

        def norm(r0):
            x = x_ref[buf, pl.ds(r0, ROW_CHUNK), :]
            xhat = x * lax.rsqrt(jnp.mean(x * x, axis=-1, keepdims=True) + EPS)
            rows = pl.ds(r0, ROW_CHUNK)
            xq_ref[rows, :] = (xhat * gq_ref[...]).astype(BF16)
            xkv_ref[rows, :] = (xhat * gkv_ref[...]).astype(BF16)
        _row_loop(tm, norm)

    @pl.when(j < q_blocks)
    def _():
        o_ref[...] = _bdot(xq_ref[...], wq_ref[...].astype(BF16)).astype(o_ref.dtype)

    @pl.when(j >= q_blocks)
    def _():
        o_ref[...] = _bdot(xkv_ref[...], wkv_ref[...].astype(BF16)).astype(o_ref.dtype)


def _qkv_cls(h, batch, g_q, g_kv, w_q, w_kv, tm=1024, tn=512):
    n, d = h.shape
    r2 = ATTN_DILATIONS[2]
    hv = h.reshape(n // r2, r2, d)
    per = ATTN_OUT // tn
    q_blocks = N_DILATED * per
    kv_blocks = 2 * N_DILATED * per

    def wq_map(i, j):
        return (0, per + jnp.minimum(j, q_blocks - 1))

    def wkv_map(i, j):
        jj = jnp.clip(j - q_blocks, 0, kv_blocks - 1)
        return (0, per + jj + jnp.where(jj >= N_DILATED * per, per, 0))

    return pl.pallas_call(
        functools.partial(_qkv_cls_kernel, tiles_per_batch=(n // batch) // tm),
        grid=(n // tm, q_blocks + kv_blocks),
        in_specs=[
            pl.BlockSpec(memory_space=pl.ANY),
            pl.BlockSpec((1, d), lambda i, j: (0, 0)),
            pl.BlockSpec((1, d), lambda i, j: (0, 0)),
            pl.BlockSpec((d, tn), wq_map),
            pl.BlockSpec((d, tn), wkv_map),
        ],
        out_specs=pl.BlockSpec((tm, tn), lambda i, j: (i, j)),
        out_shape=jax.ShapeDtypeStruct((n, 3 * N_DILATED * ATTN_OUT), BF16),
        scratch_shapes=[pltpu.VMEM((2, tm, d), F32), pltpu.VMEM((tm, d), BF16), pltpu.VMEM((tm, d), BF16),
                        pltpu.SemaphoreType.DMA((2,))],
        compiler_params=_params("arbitrary", "arbitrary"),
        name="qkv_dilated",
    )(hv, g_q.reshape(1, d), g_kv.reshape(1, d), w_q, w_kv)


def _merge_out_kernel(o0_ref, l0_ref, h_ref, w_ref, o1_hbm, l1_hbm, o2_hbm, l2_hbm, out_ref,
                      ob_ref, lb_ref, m_ref, sem, *, tiles_per_batch):
    t = pl.program_id(0)
    tiles = pl.num_programs(0)
    buf = t % 2
    per = ob_ref.shape[2]
    n_cls = ob_ref.shape[3]
    rows = per * n_cls

    def fetch(tile, b, fn):
        bi, ti = tile // tiles_per_batch, tile % tiles_per_batch
        for c in range(n_cls):
            cls = _slot_class(c)
            src_rows = pl.ds(ti * per, per)
            for g, (o_hbm, l_hbm) in enumerate(((o1_hbm, l1_hbm), (o2_hbm, l2_hbm))):
                fn(pltpu.make_async_copy(o_hbm.at[bi * n_cls + c, src_rows, :],
                                         ob_ref.at[b, g, :, cls, :], sem.at[b, 0]))
                fn(pltpu.make_async_copy(l_hbm.at[bi * n_cls + c, src_rows, :],
                                         lb_ref.at[b, g, :, cls, :], sem.at[b, 1]))

    @pl.when(t == 0)
    def _():
        fetch(0, 0, lambda copy: copy.start())

    @pl.when(t + 1 < tiles)
    def _():
        fetch(t + 1, 1 - buf, lambda copy: copy.start())

    fetch(t, buf, lambda copy: copy.wait())

    chunk = BAND
    for r0 in range(0, rows, chunk):
        rs = slice(r0, r0 + chunk)
        gs = slice(r0 // n_cls, (r0 + chunk) // n_cls)
        lses = (l0_ref[rs, :],
                lb_ref[buf, 0, gs, :, :].reshape(chunk, LANES),
                lb_ref[buf, 1, gs, :, :].reshape(chunk, LANES))
        o1 = ob_ref[buf, 0, gs, :, :].reshape(chunk, ATTN_OUT)
        o2 = ob_ref[buf, 1, gs, :, :].reshape(chunk, ATTN_OUT)
        mx = jnp.maximum(jnp.maximum(lses[0], lses[1]), lses[2])
        ex = [jnp.exp(l - mx) for l in lses]
        inv = 1.0 / (ex[0] + ex[1] + ex[2])
        wts = [e * inv for e in ex]
        spread = [[jnp.broadcast_to(w[:, h:h + 1], (chunk, HEAD_DIM)) for w in wts] for h in range(HEADS)]
        for h in range(HEADS):
            cols = slice(h * HEAD_DIM, (h + 1) * HEAD_DIM)
            merged = (spread[h][0] * o0_ref[rs, cols].astype(F32) + spread[h][1] * o1[:, cols]
                      + spread[h][2] * o2[:, cols])
            m_ref[rs, cols] = merged.astype(BF16)
    out_ref[...] = h_ref[...] + _bdot(m_ref[...], w_ref[...])


def _merge_out(h, seq, g0, g1, g2, w_o, rows=512):
    n, d = h.shape
    n_cls = ATTN_DILATIONS[2]
    per = rows // n_cls
    cls_len = seq // n_cls
    (o0, l0), (o1, l1), (o2, l2) = g0, g1, g2

    def by_class(a):
        return a.reshape(n // cls_len, cls_len, a.shape[-1])

    return pl.pallas_call(
        functools.partial(_merge_out_kernel, tiles_per_batch=seq // rows),
        grid=(n // rows,),
        in_specs=[
            pl.BlockSpec((rows, ATTN_OUT), lambda t: (t, 0)),
            pl.BlockSpec((rows, LANES), lambda t: (t, 0)),
            pl.BlockSpec((rows, d), lambda t: (t, 0)),
            pl.BlockSpec((ATTN_OUT, d), lambda t: (0, 0)),
            pl.BlockSpec(memory_space=pl.ANY),
            pl.BlockSpec(memory_space=pl.ANY),
            pl.BlockSpec(memory_space=pl.ANY),
            pl.BlockSpec(memory_space=pl.ANY),
        ],
        out_specs=pl.BlockSpec((rows, d), lambda t: (t, 0)),
        out_shape=jax.ShapeDtypeStruct((n, d), F32),
        scratch_shapes=[pltpu.VMEM((2, N_DILATED, per, n_cls, ATTN_OUT), F32),
                        pltpu.VMEM((2, N_DILATED, per, n_cls, LANES), F32),
                        pltpu.VMEM((rows, ATTN_OUT), BF16),
                        pltpu.SemaphoreType.DMA((2, 2))],
        compiler_params=_params("arbitrary"),
        name="attn_merge_out",
    )(o0, l0, h, w_o.astype(BF16), by_class(o1), by_class(l1), by_class(o2), by_class(l2))


def kernel(x, norm_mix_g, norm_ffn_g, conv_w_in, conv_b_in, conv_w_dw, conv_b_dw, conv_ln_g, conv_ln_b,
           conv_w_out, conv_b_out, norm_kv_g, w_kv, attn_w_q, attn_w_o, router_w_group, router_b_group,
           router_w_expert, router_b_expert, expert_w_gate, expert_w_up, expert_w_down, norm_final_g):
    b, s, d = x.shape
    n = b * s
    xf = x.reshape(n, d)

    u = _glu_in(xf, norm_mix_g[0], conv_w_in[0], conv_b_in[0])
    y = _dwconv(u.reshape(b, s, -1), conv_w_dw[0], conv_b_dw[0])
    h = _ln_out(y.reshape(n, -1), conv_ln_g[0], conv_ln_b[0], conv_w_out[0], conv_b_out[0], xf)
    h = _moe(h, norm_ffn_g[0], router_w_group[0], router_b_group[0], router_w_expert[0],
             router_b_expert[0], expert_w_gate, expert_w_up, expert_w_down, 0,
             norm_final_g, False)

    qkv0 = _qkv(h.reshape(b, s, d), norm_mix_g[1], norm_kv_g, attn_w_q[0], w_kv, 0)
    qkvd = _qkv_cls(h, b, norm_mix_g[1], norm_kv_g, attn_w_q[0], w_kv)
    g0 = _attn(qkv0, (0, 1, 2), s // BAND, BF16, "attn_g0")
    g1 = _attn_skew(qkvd, (0, 2, 4), F32, "attn_g1")
    g2 = _attn(qkvd, (1, 3, 5), 1, F32, "attn_g2")
    h = _merge_out(h, s, g0, g1, g2, attn_w_o[0])
    out = _moe(h, norm_ffn_g[1], router_w_group[1], router_b_group[1], router_w_expert[1],
               router_b_expert[1], expert_w_gate, expert_w_up, expert_w_down, 1,
               norm_final_g, True)
    return out.reshape(b, s, d)
```

```python
import functools

import jax
import jax.numpy as jnp
from jax import lax
from jax.experimental import pallas as pl
from jax.experimental.pallas import tpu as pltpu

F32 = jnp.float32
BF16 = jnp.bfloat16
I32 = jnp.int32

EPS = 1e-6
LANES = 128
V7X_VMEM_BYTES = 64 * 1024 * 1024
VMEM_LIMIT = V7X_VMEM_BYTES - 8 * 1024 * 1024

CONV_WIDTH = 31
HALO = 32
ATTN_DILATIONS = (1, 4, 16)
N_ATTN_GROUPS = 3
N_DILATED = 2
HEADS = 8
HEAD_DIM = 128
BAND = 128
BAND_LOG2 = 7
ATTN_OUT = HEADS * HEAD_DIM
N_GROUPS = 4
EXPERTS_PER_GROUP = 8
N_EXPERTS = N_GROUPS * EXPERTS_PER_GROUP

ROW_CHUNK = 32
ROW_UNROLL = 4
SUBLANES = 8
BULK_DMA_PRIORITY = 1


def _params(*sem):
    return pltpu.CompilerParams(dimension_semantics=sem, vmem_limit_bytes=VMEM_LIMIT)


def _row_loop(n_rows, body, chunk=ROW_CHUNK, unroll=ROW_UNROLL):
    def step(c, carry):
        body(pl.multiple_of(c * chunk, chunk))
        return carry
    lax.fori_loop(0, n_rows // chunk, step, 0, unroll=unroll)


def _rms(x, g):
    ms = jnp.mean(x * x, axis=-1, keepdims=True)
    return x * lax.rsqrt(ms + EPS) * g


def _bdot(a, b):
    return jnp.dot(a, b, preferred_element_type=F32)


def _glu_in_kernel(x_ref, g_ref, wv_ref, wg_ref, bv_ref, bg_ref, o_ref, xn_ref):
    tm = x_ref.shape[0]

    @pl.when(pl.program_id(1) == 0)
    def _():
        def norm(r0):
            rows = pl.ds(r0, ROW_CHUNK)
            xn_ref[rows, :] = _rms(x_ref[rows, :], g_ref[...]).astype(BF16)
        _row_loop(tm, norm)

    xn = xn_ref[...]
    val = _bdot(xn, wv_ref[...].astype(BF16)) + bv_ref[...]
    gate = _bdot(xn, wg_ref[...].astype(BF16)) + bg_ref[...]
    o_ref[...] = (val * jax.nn.sigmoid(gate)).astype(o_ref.dtype)


def _glu_in(x, g, w_in, b_in, tm=1024, tn=512):
    n, d = x.shape
    c = w_in.shape[1] // 2
    nj = c // tn
    b2 = b_in.reshape(1, 2 * c)
    return pl.pallas_call(
        _glu_in_kernel,
        grid=(n // tm, nj),
        in_specs=[
            pl.BlockSpec((tm, d), lambda i, j: (i, 0)),
            pl.BlockSpec((1, d), lambda i, j: (0, 0)),
            pl.BlockSpec((d, tn), lambda i, j: (0, j)),
            pl.BlockSpec((d, tn), lambda i, j: (0, j + nj)),
            pl.BlockSpec((1, tn), lambda i, j: (0, j)),
            pl.BlockSpec((1, tn), lambda i, j: (0, j + nj)),
        ],
        out_specs=pl.BlockSpec((tm, tn), lambda i, j: (i, j)),
        out_shape=jax.ShapeDtypeStruct((n, c), BF16),
        scratch_shapes=[pltpu.VMEM((tm, d), BF16)],
        compiler_params=_params("arbitrary", "arbitrary"),
        name="glu_in",
    )(x, g.reshape(1, d), w_in, w_in, b2, b2)


def _dwconv_kernel(cur_ref, halo_ref, w_ref, b_ref, o_ref, buf_ref):
    ts, cw = cur_ref.shape[1], cur_ref.shape[2]
    rw = 64
    keep = jnp.where(pl.program_id(1) > 0, 1.0, 0.0)
    buf_ref[0, 0:HALO, :] = halo_ref[0].astype(F32) * keep
    buf_ref[0, HALO:, :] = cur_ref[0].astype(F32)
    shifted_rows = ts + HALO - SUBLANES
    for s in range(1, SUBLANES):
        buf_ref[s, 0:shifted_rows, :] = buf_ref[0, s:s + shifted_rows, :]
    first = HALO - (CONV_WIDTH - 1)
    for c0 in range(0, cw, LANES):
        cols = slice(c0, c0 + LANES)
        for r0 in range(0, ts, rw):
            acc = jnp.broadcast_to(b_ref[:, cols], (rw, LANES))
            for k in range(CONV_WIDTH):
                s = (first + k) % SUBLANES
                start = r0 + first + k - s
                acc = acc + w_ref[k:k + 1, cols] * buf_ref[s, start:start + rw, cols]
            o_ref[0, r0:r0 + rw, cols] = acc.astype(o_ref.dtype)


def _dwconv(u, w_dw, b_dw, ts=256, cw=512):
    b, s, c = u.shape
    hb = ts // HALO
    return pl.pallas_call(
        _dwconv_kernel,
        grid=(b, s // ts, c // cw),
        in_specs=[
            pl.BlockSpec((1, ts, cw), lambda bi, si, ci: (bi, si, ci)),
            pl.BlockSpec((1, HALO, cw), lambda bi, si, ci: (bi, jnp.maximum(si * hb - 1, 0), ci)),
            pl.BlockSpec((CONV_WIDTH, cw), lambda bi, si, ci: (0, ci)),
            pl.BlockSpec((1, cw), lambda bi, si, ci: (0, ci)),
        ],
        out_specs=pl.BlockSpec((1, ts, cw), lambda bi, si, ci: (bi, si, ci)),
        out_shape=jax.ShapeDtypeStruct((b, s, c), BF16),
        scratch_shapes=[pltpu.VMEM((SUBLANES, ts + HALO, cw), F32)],
        compiler_params=_params("arbitrary", "arbitrary", "arbitrary"),
        name="dwconv",
    )(u, u, w_dw, b_dw.reshape(1, c))


def _ln_out_kernel(y_ref, lg_ref, lb_ref, w_ref, b_ref, res_ref, o_ref, a_ref):
    tm = y_ref.shape[0]

    @pl.when(pl.program_id(1) == 0)
    def _():
        def norm(r0):
            rows = pl.ds(r0, ROW_CHUNK)
            y = y_ref[rows, :].astype(F32)
            mu = jnp.mean(y, axis=-1, keepdims=True)
            yc = y - mu
            var = jnp.mean(yc * yc, axis=-1, keepdims=True)
            z = yc * lax.rsqrt(var + EPS) * lg_ref[...] + lb_ref[...]
            a_ref[rows, :] = (z * jax.nn.sigmoid(z)).astype(BF16)
        _row_loop(tm, norm)

    o_ref[...] = res_ref[...] + _bdot(a_ref[...], w_ref[...].astype(BF16)) + b_ref[...]


def _ln_out(y, ln_g, ln_b, w_out, b_out, res, tm=1024, tn=512):
    n, c = y.shape
    d = w_out.shape[1]
    return pl.pallas_call(
        _ln_out_kernel,
        grid=(n // tm, d // tn),
        in_specs=[
            pl.BlockSpec((tm, c), lambda i, j: (i, 0)),
            pl.BlockSpec((1, c), lambda i, j: (0, 0)),
            pl.BlockSpec((1, c), lambda i, j: (0, 0)),
            pl.BlockSpec((c, tn), lambda i, j: (0, j)),
            pl.BlockSpec((1, tn), lambda i, j: (0, j)),
            pl.BlockSpec((tm, tn), lambda i, j: (i, j)),
        ],
        out_specs=pl.BlockSpec((tm, tn), lambda i, j: (i, j)),
        out_shape=jax.ShapeDtypeStruct((n, d), F32),
        scratch_shapes=[pltpu.VMEM((tm, c), BF16)],
        compiler_params=_params("arbitrary", "arbitrary"),
        name="ln_out",
    )(y, ln_g.reshape(1, c), ln_b.reshape(1, c), w_out, b_out.reshape(1, d), res)


def _split_bf16(a):
    hi = a.astype(BF16)
    lo = (a - hi.astype(F32)).astype(BF16)
    return hi, lo


def _router_kernel(h_ref, g_ref, w_ref, b_ref, ids_ref, wts_ref):
    t = _rms(h_ref[...], g_ref[...])
    t_hi, t_lo = _split_bf16(t)
    w_hi, w_lo = _split_bf16(w_ref[...])
    logits = _bdot(t_hi, w_hi) + _bdot(t_hi, w_lo) + _bdot(t_lo, w_hi) + b_ref[...]

    lane = lax.broadcasted_iota(I32, logits.shape, 1)
    lanef = lane.astype(F32)
    neg = -jnp.inf
    is_group = lane < N_GROUPS
    glog = jnp.where(is_group, logits, neg)
    gmax = jnp.max(glog, axis=1, keepdims=True)
    gi = jnp.min(jnp.where(glog == gmax, lanef, float(LANES)), axis=1, keepdims=True)
    gsum = jnp.sum(jnp.where(is_group, jnp.exp(logits - gmax), 0.0), axis=1, keepdims=True)
    gp = 1.0 / gsum

    lo_lane = N_GROUPS + gi * EXPERTS_PER_GROUP
    in_sel = (lanef >= lo_lane) & (lanef < lo_lane + EXPERTS_PER_GROUP)
    sel = jnp.where(in_sel, logits, neg)
    m1 = jnp.max(sel, axis=1, keepdims=True)
    i1 = jnp.min(jnp.where(sel == m1, lanef, float(LANES)), axis=1, keepdims=True)
    sel2 = jnp.where(lanef == i1, neg, sel)
    m2 = jnp.max(sel2, axis=1, keepdims=True)
    i2 = jnp.min(jnp.where(sel2 == m2, lanef, float(LANES)), axis=1, keepdims=True)
    e2 = jnp.exp(m2 - m1)
    w1 = gp / (1.0 + e2)
    w2 = gp * e2 / (1.0 + e2)

    ids = jnp.where(lane == 0, i1 - N_GROUPS, jnp.where(lane == 1, i2 - N_GROUPS, 0.0))
    ids_ref[...] = ids.astype(I32)
    wts_ref[...] = jnp.where(lane == 0, w1, jnp.where(lane == 1, w2, 0.0))


def _router(h, g, w_group, b_group, w_expert, b_expert, tm=512):
    n, d = h.shape
    w_e = jnp.transpose(w_expert, (1, 0, 2)).reshape(d, N_EXPERTS)
    pad = LANES - N_GROUPS - N_EXPERTS
    wr = jnp.concatenate([w_group, w_e, jnp.zeros((d, pad), F32)], axis=1)
    br = jnp.concatenate([b_group, b_expert.reshape(N_EXPERTS), jnp.zeros((pad,), F32)]).reshape(1, LANES)
    return pl.pallas_call(
        _router_kernel,
        grid=(n // tm,),
        in_specs=[
            pl.BlockSpec((tm, d), lambda i: (i, 0)),
            pl.BlockSpec((1, d), lambda i: (0, 0)),
            pl.BlockSpec((d, LANES), lambda i: (0, 0)),
            pl.BlockSpec((1, LANES), lambda i: (0, 0)),
        ],
        out_specs=[pl.BlockSpec((tm, LANES), lambda i: (i, 0)),
                   pl.BlockSpec((tm, LANES), lambda i: (i, 0))],
        out_shape=[jax.ShapeDtypeStruct((n, LANES), I32), jax.ShapeDtypeStruct((n, LANES), F32)],
        compiler_params=_params("arbitrary"),
        name="router",
    )(h, g.reshape(1, d), wr, br)


def _dispatch_plan(ids, tm, n_items):
    e_flat = jnp.concatenate([ids[:, 0], ids[:, 1]])
    onehot = (e_flat[:, None] == jnp.arange(N_EXPERTS, dtype=I32)[None, :]).astype(I32)
    csum = jnp.cumsum(onehot, axis=0)
    rank = jnp.sum(onehot * csum, axis=1) - 1
    counts = csum[-1]
    ends = jnp.cumsum(counts)
    starts = ends - counts
    dest = jnp.sum(onehot * starts[None, :], axis=1) + rank

    first_tile = starts // tm
    last_tile = jnp.maximum(ends - 1, 0) // tm
    items = jnp.where(counts > 0, last_tile - first_tile + 1, 0)
    item_end = jnp.cumsum(items)
    item_start = item_end - items
    total = item_end[-1]
    w = jnp.arange(n_items, dtype=I32)
    valid = w < total
    wc = jnp.minimum(w, total - 1)
    e_w = jnp.searchsorted(item_end, wc, side="right").astype(I32)
    tile_w = first_tile[e_w] + (wc - item_start[e_w])
    lo = jnp.maximum(starts[e_w], tile_w * tm) - tile_w * tm
    hi = jnp.minimum(ends[e_w], (tile_w + 1) * tm) - tile_w * tm
    lo = jnp.where(valid, lo, 0)
    hi = jnp.where(valid, hi, 0)
    prev_tile = jnp.concatenate([jnp.full((1,), -1, I32), tile_w[:-1]])
    prev_e = jnp.concatenate([jnp.full((1,), -1, I32), e_w[:-1]])
    first = (valid & (tile_w != prev_tile)).astype(I32)
    new_e = (valid & (e_w != prev_e)).astype(I32)

    present = counts > 0
    n_present = jnp.sum(present.astype(I32))
    order = jnp.argsort(jnp.logical_not(present), stable=True).astype(I32)
    ordinal = jnp.cumsum(present.astype(I32)) - 1
    k_w = ordinal[e_w]
    slot = k_w % 2
    ahead = jnp.where(k_w + 2 < n_present, order[jnp.minimum(k_w + 2, N_EXPERTS - 1)], -1)
    head = jnp.stack([order[0], jnp.where(n_present > 1, order[1], -1)])
    plan = (tile_w.astype(I32), e_w, lo.astype(I32), hi.astype(I32), first, new_e, valid.astype(I32),
            slot.astype(I32), ahead.astype(I32), head.astype(I32))
    return dest.astype(I32), plan


def _scatter_kernel(dest_ref, h_ref, g_ref, xs_ref, t_ref, sem):
    tm = h_ref.shape[0]
    i = pl.program_id(0)
    steps = pl.num_programs(0)
    n = steps * tm
    buf = i % 2
    groups = tm // SUBLANES

    def row_copy(step, b, grp, j, pick):
        d = dest_ref[pick * n + step * tm + grp * SUBLANES + j]
        return pltpu.make_async_copy(t_ref.at[b, grp, pl.ds(j, 1), :], xs_ref.at[pl.ds(d, 1), :], sem.at[b])

    def for_group(step, b, grp, fn):
        for j in range(SUBLANES):
            fn(row_copy(step, b, grp, j, 0), 0)
            fn(row_copy(step, b, grp, j, 1), 1)

    def for_rows(step, b, fn):
        def body(grp, carry):
            for_group(step, b, grp, fn)
            return carry
        lax.fori_loop(0, groups, body, 0)

    @pl.when(i >= 2)
    def _():
        for_rows(i - 2, buf, lambda copy, pick: copy.wait())

    per = ROW_CHUNK // SUBLANES
    chunks = groups // per

    def norm(c):
        rows = pl.ds(pl.multiple_of(c * ROW_CHUNK, ROW_CHUNK), ROW_CHUNK)
        t = _rms(h_ref[rows, :], g_ref[...])
        t_ref[buf, pl.ds(pl.multiple_of(c * per, per), per)] = t.reshape(per, SUBLANES, t.shape[-1])

    def start(c):
        for k in range(per):
            for_group(i, buf, c * per + k, lambda copy, pick: copy.start(priority=pick))

    norm(0)

    def body(c, carry):
        start(c)
        norm(c + 1)
        return carry
    lax.fori_loop(0, chunks - 1, body, 0)
    start(chunks - 1)

    @pl.when(i == steps - 1)
    def _():
        @pl.when(i >= 1)
        def _():
            for_rows(i - 1, 1 - buf, lambda copy, pick: copy.wait())
        for_rows(i, buf, lambda copy, pick: copy.wait())


def _scatter(h, g, dest, tm=256):
    n, d = h.shape
    return pl.pallas_call(
        _scatter_kernel,
        grid_spec=pltpu.PrefetchScalarGridSpec(
            num_scalar_prefetch=1,
            grid=(n // tm,),
            in_specs=[pl.BlockSpec((tm, d), lambda i, dest: (i, 0)),
                      pl.BlockSpec((1, d), lambda i, dest: (0, 0))],
            out_specs=pl.BlockSpec(memory_space=pl.ANY),
            scratch_shapes=[pltpu.VMEM((2, tm // SUBLANES, SUBLANES, d), F32),
                            pltpu.SemaphoreType.DMA((2,))],
        ),
        out_shape=jax.ShapeDtypeStruct((2 * n, d), F32),
        compiler_params=_params("arbitrary"),
        name="moe_scatter",
    )(dest, h, g.reshape(1, d))


def _experts_kernel(tile_ref, exp_ref, lo_ref, hi_ref, first_ref, new_ref, valid_ref, slot_ref,
                    ahead_ref, head_ref, xs_ref, wg_hbm, wu_hbm, wd_hbm, o_ref,
                    wg_st, wu_st, wd_st, wg_bf, wu_bf, wd_bf, sem, *, layer):
    w = pl.program_id(0)
    tm = xs_ref.shape[0]
    chunk = 256

    def fetch(e, s):
        return (pltpu.make_async_copy(wg_hbm.at[layer, e], wg_st.at[s], sem.at[s, 0]),
                pltpu.make_async_copy(wu_hbm.at[layer, e], wu_st.at[s], sem.at[s, 1]),
                pltpu.make_async_copy(wd_hbm.at[layer, e], wd_st.at[s], sem.at[s, 2]))

    @pl.when(w == 0)
    def _():
        for copy in fetch(head_ref[0], 0):
            copy.start(priority=BULK_DMA_PRIORITY)

        @pl.when(head_ref[1] >= 0)
        def _():
            for copy in fetch(head_ref[1], 1):
                copy.start(priority=BULK_DMA_PRIORITY)

    @pl.when(new_ref[w] == 1)
    def _():
        s = slot_ref[w]
        for copy in fetch(exp_ref[w], s):
            copy.wait()

        def cast_in(r0):
            rows = pl.ds(r0, chunk)
            wg_bf[rows, :] = wg_st[s, rows, :].astype(BF16)
            wu_bf[rows, :] = wu_st[s, rows, :].astype(BF16)
        _row_loop(wg_bf.shape[0], cast_in, chunk)

        def cast_down(r0):
            rows = pl.ds(r0, chunk)
            wd_bf[rows, :] = wd_st[s, rows, :].astype(BF16)
        _row_loop(wd_bf.shape[0], cast_down, chunk)

        @pl.when(ahead_ref[w] >= 0)
        def _():
            for copy in fetch(ahead_ref[w], s):
                copy.start(priority=BULK_DMA_PRIORITY)

    @pl.when(valid_ref[w] == 1)
    def _():
        row = lax.broadcasted_iota(I32, (tm, 1), 0)
        mine = (row >= lo_ref[w]) & (row < hi_ref[w])
        x = jnp.where(mine, xs_ref[...], 0.0).astype(BF16)
        hg = _bdot(x, wg_bf[...])
        hu = _bdot(x, wu_bf[...])
        act = (hg * jax.nn.sigmoid(hg) * hu).astype(BF16)
        y = _bdot(act, wd_bf[...])

        @pl.when(first_ref[w] == 1)
        def _():
            o_ref[...] = y

        @pl.when(first_ref[w] == 0)
        def _():
            o_ref[...] += y


def _experts(xs, plan, w_gate, w_up, w_down, layer, tm):
    p, d = xs.shape
    f = w_gate.shape[3]
    n_items = plan[0].shape[0]
    return pl.pallas_call(
        functools.partial(_experts_kernel, layer=layer),
        grid_spec=pltpu.PrefetchScalarGridSpec(
            num_scalar_prefetch=len(plan),
            grid=(n_items,),
            in_specs=[
                pl.BlockSpec((tm, d), lambda w, tile, *_: (tile[w], 0)),
                pl.BlockSpec(memory_space=pl.ANY),
                pl.BlockSpec(memory_space=pl.ANY),
                pl.BlockSpec(memory_space=pl.ANY),
            ],
            out_specs=pl.BlockSpec((tm, d), lambda w, tile, *_: (tile[w], 0)),
            scratch_shapes=[
                pltpu.VMEM((2, d, f), F32), pltpu.VMEM((2, d, f), F32), pltpu.VMEM((2, f, d), F32),
                pltpu.VMEM((d, f), BF16), pltpu.VMEM((d, f), BF16), pltpu.VMEM((f, d), BF16),
                pltpu.SemaphoreType.DMA((2, 3)),
            ],
        ),
        out_shape=jax.ShapeDtypeStruct((p, d), F32),
        compiler_params=_params("arbitrary"),
        name="moe_experts",
    )(*plan, xs, w_gate, w_up, w_down)


def _combine_kernel(dest_ref, h_ref, wts_ref, g_ref, ys_ref, o_ref, y_ref, sem, *, final_norm):
    tm = h_ref.shape[0]
    i = pl.program_id(0)
    steps = pl.num_programs(0)
    n = steps * tm
    buf = i % 2
    groups = tm // SUBLANES

    def row_copy(step, b, grp, j, pick):
        d = dest_ref[pick * n + step * tm + grp * SUBLANES + j]
        return pltpu.make_async_copy(ys_ref.at[pl.ds(d, 1), :], y_ref.at[b, pick, grp, pl.ds(j, 1), :],
                                     sem.at[b])

    def for_group(step, b, grp, fn):
        for j in range(SUBLANES):
            fn(row_copy(step, b, grp, j, 0))
            fn(row_copy(step, b, grp, j, 1))

    def for_rows(step, b, fn):
        def body(grp, carry):
            for_group(step, b, grp, fn)
            return carry
        lax.fori_loop(0, groups, body, 0)

    per = ROW_CHUNK // SUBLANES
    chunks = groups // per

    def mix(c):
        rows = pl.ds(pl.multiple_of(c * ROW_CHUNK, ROW_CHUNK), ROW_CHUNK)
        grps = pl.ds(pl.multiple_of(c * per, per), per)
        wts = wts_ref[rows, :]
        y0 = y_ref[buf, 0, grps].reshape(ROW_CHUNK, -1)
        y1 = y_ref[buf, 1, grps].reshape(ROW_CHUNK, -1)
        out = h_ref[rows, :] + wts[:, 0:1] * y0 + wts[:, 1:2] * y1
        if final_norm:
            out = _rms(out, g_ref[...])
        o_ref[rows, :] = out

    @pl.when(i == 0)
    def _():
        for_rows(0, 0, lambda copy: copy.start())

    for_rows(i, buf, lambda copy: copy.wait())

    @pl.when(i + 1 < steps)
    def _():
        def body(c, carry):
            for k in range(per):
                for_group(i + 1, 1 - buf, c * per + k, lambda copy: copy.start())
            mix(c)
            return carry
        lax.fori_loop(0, chunks, body, 0)

    @pl.when(i + 1 == steps)
    def _():
        def body(c, carry):
            mix(c)
            return carry
        lax.fori_loop(0, chunks, body, 0, unroll=ROW_UNROLL)


def _combine(h, wts, ys, dest, g_final, final_norm, tm=256):
    n, d = h.shape
    return pl.pallas_call(
        functools.partial(_combine_kernel, final_norm=final_norm),
        grid_spec=pltpu.PrefetchScalarGridSpec(
            num_scalar_prefetch=1,
            grid=(n // tm,),
            in_specs=[pl.BlockSpec((tm, d), lambda i, dest: (i, 0)),
                      pl.BlockSpec((tm, LANES), lambda i, dest: (i, 0)),
                      pl.BlockSpec((1, d), lambda i, dest: (0, 0)),
                      pl.BlockSpec(memory_space=pl.ANY)],
            out_specs=pl.BlockSpec((tm, d), lambda i, dest: (i, 0)),
            scratch_shapes=[pltpu.VMEM((2, 2, tm // SUBLANES, SUBLANES, d), F32),
                            pltpu.SemaphoreType.DMA((2,))],
        ),
        out_shape=jax.ShapeDtypeStruct((n, d), F32),
        compiler_params=_params("arbitrary"),
        name="moe_combine",
    )(dest, h, wts, g_final.reshape(1, d), ys)


def _moe(h, g, w_group, b_group, w_expert, b_expert, w_gate, w_up, w_down, layer, g_final, final_norm,
         tm=256):
    n = h.shape[0]
    ids, wts = _router(h, g, w_group, b_group, w_expert, b_expert)
    n_items = 2 * n // tm + N_EXPERTS
    dest, plan = _dispatch_plan(ids, tm, n_items)
    xs = _scatter(h, g, dest)
    ys = _experts(xs, plan, w_gate, w_up, w_down, layer, tm)
    return _combine(h, wts, ys, dest, g_final, final_norm)


def _qkv_kernel(*refs, pieces):
    x_refs = refs[:pieces]
    gq_ref, gkv_ref, wq_ref, wkv_ref, o_ref, xq_ref, xkv_ref = refs[pieces:]
    j = pl.program_id(1)
    rp = x_refs[0].shape[1]
    q_blocks = ATTN_OUT // o_ref.shape[1]

    @pl.when(j == 0)
    def _():
        for p, x_ref in enumerate(x_refs):
            def norm(r0, x_ref=x_ref, p=p):
                x = x_ref[0, pl.ds(r0, ROW_CHUNK), :]
                xhat = x * lax.rsqrt(jnp.mean(x * x, axis=-1, keepdims=True) + EPS)
                rows = pl.ds(p * rp + r0, ROW_CHUNK)
                xq_ref[rows, :] = (xhat * gq_ref[...]).astype(BF16)
                xkv_ref[rows, :] = (xhat * gkv_ref[...]).astype(BF16)
            _row_loop(rp, norm)

    @pl.when(j < q_blocks)
    def _():
        o_ref[...] = _bdot(xq_ref[...], wq_ref[...].astype(BF16)).astype(o_ref.dtype)

    @pl.when(j >= q_blocks)
    def _():
        o_ref[...] = _bdot(xkv_ref[...], wkv_ref[...].astype(BF16)).astype(o_ref.dtype)


def _qkv(h3, g_q, g_kv, w_q, w_kv, group, tm=1024, tn=512):
    b, s, d = h3.shape
    r = ATTN_DILATIONS[group]
    sub = s // r
    rp = min(sub, tm)
    pieces = tm // rp
    tiles_per_class = sub // rp
    tiles_per_batch = s // tm
    hv = h3.reshape(b, sub, r * d)
    qb = ATTN_OUT // tn

    def x_map(p):
        def index(i, j):
            bi, ti = i // tiles_per_batch, i % tiles_per_batch
            return (bi, ti % tiles_per_class, (ti // tiles_per_class) * pieces + p)
        return index

    def wq_map(i, j):
        return (0, group * qb + jnp.minimum(j, qb - 1))

    def wkv_map(i, j):
        jj = jnp.maximum(j - qb, 0)
        return (0, (jj // qb) * N_ATTN_GROUPS * qb + group * qb + jj % qb)

    in_specs = [pl.BlockSpec((1, rp, d), x_map(p)) for p in range(pieces)]
    in_specs += [
        pl.BlockSpec((1, d), lambda i, j: (0, 0)),
        pl.BlockSpec((1, d), lambda i, j: (0, 0)),
        pl.BlockSpec((d, tn), wq_map),
        pl.BlockSpec((d, tn), wkv_map),
    ]
    return pl.pallas_call(
        functools.partial(_qkv_kernel, pieces=pieces),
        grid=(b * s // tm, 3 * qb),
        in_specs=in_specs,
        out_specs=pl.BlockSpec((tm, tn), lambda i, j: (i, j)),
        out_shape=jax.ShapeDtypeStruct((b * s, 3 * ATTN_OUT), BF16),
        scratch_shapes=[pltpu.VMEM((tm, d), BF16), pltpu.VMEM((tm, d), BF16)],
        compiler_params=_params("arbitrary", "arbitrary"),
        name=f"qkv_g{group}",
    )(*([hv] * pieces), g_q.reshape(1, d), g_kv.reshape(1, d), w_q, w_kv)


def _dot_t(a, b):
    return lax.dot_general(a, b, (((1,), (1,)), ((), ())), preferred_element_type=F32)


def _lane_slab(columns):
    lane = lax.broadcasted_iota(I32, (columns[0].shape[0], LANES), 1)
    slab = jnp.zeros(lane.shape, F32)
    for h, col in enumerate(columns):
        slab = jnp.where(lane == h, col, slab)
    return slab


def _attn_kernel(q_ref, kc_ref, vc_ref, kp_ref, vp_ref, o_ref, lse_ref, *, seg_blocks):
    qblocks = q_ref.shape[0] // BAND
    step = pl.program_id(0)
    banded = seg_blocks > 1
    assert not banded or seg_blocks % qblocks == 0
    n_keys = 2 * BAND if banded else BAND
    qi = lax.broadcasted_iota(I32, (BAND, n_keys), 0)
    kj = lax.broadcasted_iota(I32, (BAND, n_keys), 1)
    mask = (kj >= qi) & (kj <= qi + BAND) if banded else kj <= qi
    scale = HEAD_DIM ** -0.5
    neg = -jnp.inf
    heads = [slice(h * HEAD_DIM, (h + 1) * HEAD_DIM) for h in range(HEADS)]

    for sb in range(qblocks):
        rows = slice(sb * BAND, (sb + 1) * BAND)
        if not banded:
            scores = [_dot_t(q_ref[rows, c], kc_ref[rows, c]) for c in heads]
        elif sb > 0:
            krows = slice((sb - 1) * BAND, (sb + 1) * BAND)
            scores = [_dot_t(q_ref[rows, c], kc_ref[krows, c]) for c in heads]
        else:
            scores = [jnp.concatenate([_dot_t(q_ref[rows, c], kp_ref[:, c]),
                                       _dot_t(q_ref[rows, c], kc_ref[rows, c])], axis=1) for c in heads]
        scores = [jnp.where(mask, s * scale, neg) for s in scores]
        if banded and sb == 0:
            at_start = (step * qblocks) % seg_blocks == 0
            drop = jnp.where(kj < BAND, jnp.where(at_start, neg, 0.0), 0.0)
            scores = [s + drop for s in scores]
        ms = [jnp.max(s, axis=1, keepdims=True) for s in scores]
        ps = [jnp.exp(s - m) for s, m in zip(scores, ms)]
        ls = [jnp.sum(p, axis=1, keepdims=True) for p in ps]
        pb = [p.astype(BF16) for p in ps]
        if not banded:
            accs = [_bdot(p, vc_ref[rows, c]) for p, c in zip(pb, heads)]
        elif sb > 0:
            accs = [_bdot(p, vc_ref[krows, c]) for p, c in zip(pb, heads)]
        else:
            accs = [_bdot(p[:, :BAND], vp_ref[:, c]) + _bdot(p[:, BAND:], vc_ref[rows, c])
                    for p, c in zip(pb, heads)]
        for acc, l, c in zip(accs, ls, heads):
            o_ref[rows, c] = (acc / l).astype(o_ref.dtype)
        lse_ref[rows, :] = _lane_slab([m + jnp.log(l) for m, l in zip(ms, ls)])


def _attn(qkv, cols, seg_blocks, out_dtype, name, rows_per_step=512):
    n = qkv.shape[0]
    cq, ck, cv = cols
    qblocks = rows_per_step // BAND
    return pl.pallas_call(
        functools.partial(_attn_kernel, seg_blocks=seg_blocks),
        grid=(n // rows_per_step,),
        in_specs=[
            pl.BlockSpec((rows_per_step, ATTN_OUT), lambda i: (i, cq)),
            pl.BlockSpec((rows_per_step, ATTN_OUT), lambda i: (i, ck)),
            pl.BlockSpec((rows_per_step, ATTN_OUT), lambda i: (i, cv)),
            pl.BlockSpec((BAND, ATTN_OUT), lambda i: (jnp.maximum(i * qblocks - 1, 0), ck)),
            pl.BlockSpec((BAND, ATTN_OUT), lambda i: (jnp.maximum(i * qblocks - 1, 0), cv)),
        ],
        out_specs=[pl.BlockSpec((rows_per_step, ATTN_OUT), lambda i: (i, 0)),
                   pl.BlockSpec((rows_per_step, LANES), lambda i: (i, 0))],
        out_shape=[jax.ShapeDtypeStruct((n, ATTN_OUT), out_dtype), jax.ShapeDtypeStruct((n, LANES), F32)],
        compiler_params=_params("arbitrary"),
        name=name,
    )(qkv, qkv, qkv, qkv, qkv)


def _attn_skew_kernel(q_ref, k_ref, v_ref, o_ref, lse_ref):
    subs = q_ref.shape[0] // BAND
    keys = k_ref.shape[0]
    u = lax.broadcasted_iota(I32, (BAND, keys), 0)
    kc = lax.broadcasted_iota(I32, (BAND, keys), 1)
    base = subs * (u - (kc & (BAND - 1))) - lax.shift_right_logical(kc, BAND_LOG2)
    scale = HEAD_DIM ** -0.5
    neg = -jnp.inf
    heads = [slice(h * HEAD_DIM, (h + 1) * HEAD_DIM) for h in range(HEADS)]
    for s in range(subs):
        rows = slice(s * BAND, (s + 1) * BAND)
        delta = base + s
        mask = (delta >= 0) & (delta <= BAND)
        scores = [jnp.where(mask, _dot_t(q_ref[rows, c], k_ref[:, c]) * scale, neg) for c in heads]
        ms = [jnp.max(sc, axis=1, keepdims=True) for sc in scores]
        ps = [jnp.exp(sc - m) for sc, m in zip(scores, ms)]
        ls = [jnp.sum(p, axis=1, keepdims=True) for p in ps]
        accs = [_bdot(p.astype(BF16), v_ref[:, c]) for p, c in zip(ps, heads)]
        for acc, l, c in zip(accs, ls, heads):
            o_ref[rows, c] = (acc / l).astype(o_ref.dtype)
        lse_ref[rows, :] = _lane_slab([m + jnp.log(l) for m, l in zip(ms, ls)])


def _attn_skew(qkv, cols, out_dtype, name):
    n = qkv.shape[0]
    cq, ck, cv = cols
    rows = (ATTN_DILATIONS[2] // ATTN_DILATIONS[1]) * BAND
    return pl.pallas_call(
        _attn_skew_kernel,
        grid=(n // rows,),
        in_specs=[pl.BlockSpec((rows, ATTN_OUT), lambda i: (i, cq)),
                  pl.BlockSpec((rows, ATTN_OUT), lambda i: (i, ck)),
                  pl.BlockSpec((rows, ATTN_OUT), lambda i: (i, cv))],
        out_specs=[pl.BlockSpec((rows, ATTN_OUT), lambda i: (i, 0)),
                   pl.BlockSpec((rows, LANES), lambda i: (i, 0))],
        out_shape=[jax.ShapeDtypeStruct((n, ATTN_OUT), out_dtype), jax.ShapeDtypeStruct((n, LANES), F32)],
        compiler_params=_params("arbitrary"),
        name=name,
    )(qkv, qkv, qkv)


def _slot_class(c):
    return (c % 4) * 4 + c // 4


def _qkv_cls_kernel(h_hbm, gq_ref, gkv_ref, wq_ref, wkv_ref, o_ref, x_ref, xq_ref, xkv_ref, sem,
                    *, tiles_per_batch):
    i, j = pl.program_id(0), pl.program_id(1)
    tiles = pl.num_programs(0)
    tm = x_ref.shape[1]
    per_class = h_hbm.shape[0] // (tiles // tiles_per_batch)
    slots = tm // per_class
    q_blocks = N_DILATED * ATTN_OUT // o_ref.shape[1]

    def fetch(tile, buf, fn):
        bi, ti = tile // tiles_per_batch, tile % tiles_per_batch
        for c in range(slots):
            cls = _slot_class(ti * slots + c)
            fn(pltpu.make_async_copy(h_hbm.at[pl.ds(bi * per_class, per_class), cls, :],
                                     x_ref.at[buf, pl.ds(c * per_class, per_class), :],
                                     sem.at[buf]))

    @pl.when(j == 0)
    def _():
        buf = i % 2

        @pl.when(i == 0)
        def _():
            fetch(0, 0, lambda copy: copy.start())

        @pl.when(i + 1 < tiles)
        def _():
            fetch(i + 1, 1 - buf, lambda copy: copy.start())

        fetch(i, buf, lambda copy: copy.wait())

        def norm(r0):
            x = x_ref[buf, pl.ds(r0, ROW_CHUNK), :]
            xhat = x * lax.rsqrt(jnp.mean(x * x, axis=-1, keepdims=True) + EPS)
            rows = pl.ds(r0, ROW_CHUNK)
            xq_ref[rows, :] = (xhat * gq_ref[...]).astype(BF16)
            xkv_ref[rows, :] = (xhat * gkv_ref[...]).astype(BF16)
        _row_loop(tm, norm)

    @pl.when(j < q_blocks)
    def _():
        o_ref[...] = _bdot(xq_ref[...], wq_ref[...].astype(BF16)).astype(o_ref.dtype)

    @pl.when(j >= q_blocks)
    def _():
        o_ref[...] = _bdot(xkv_ref[...], wkv_ref[...].astype(BF16)).astype(o_ref.dtype)


def _qkv_cls(h, batch, g_q, g_kv, w_q, w_kv, tm=1024, tn=512):
    n, d = h.shape
    r2 = ATTN_DILATIONS[2]
    hv = h.reshape(n // r2, r2, d)
    per = ATTN_OUT // tn
    q_blocks = N_DILATED * per
    kv_blocks = 2 * N_DILATED * per

    def wq_map(i, j):
        return (0, per + jnp.minimum(j, q_blocks - 1))

    def wkv_map(i, j):
        jj = jnp.clip(j - q_blocks, 0, kv_blocks - 1)
        return (0, per + jj + jnp.where(jj >= N_DILATED * per, per, 0))

    return pl.pallas_call(
        functools.partial(_qkv_cls_kernel, tiles_per_batch=(n // batch) // tm),
        grid=(n // tm, q_blocks + kv_blocks),
        in_specs=[
            pl.BlockSpec(memory_space=pl.ANY),
            pl.BlockSpec((1, d), lambda i, j: (0, 0)),
            pl.BlockSpec((1, d), lambda i, j: (0, 0)),
            pl.BlockSpec((d, tn), wq_map),
            pl.BlockSpec((d, tn), wkv_map),
        ],
        out_specs=pl.BlockSpec((tm, tn), lambda i, j: (i, j)),
        out_shape=jax.ShapeDtypeStruct((n, 3 * N_DILATED * ATTN_OUT), BF16),
        scratch_shapes=[pltpu.VMEM((2, tm, d), F32), pltpu.VMEM((tm, d), BF16), pltpu.VMEM((tm, d), BF16),
                        pltpu.SemaphoreType.DMA((2,))],
        compiler_params=_params("arbitrary", "arbitrary"),
        name="qkv_dilated",
    )(hv, g_q.reshape(1, d), g_kv.reshape(1, d), w_q, w_kv)


def _merge_out_kernel(o0_ref, l0_ref, h_ref, w_ref, o1_hbm, l1_hbm, o2_hbm, l2_hbm, out_ref,
                      ob_ref, lb_ref, m_ref, sem, *, tiles_per_batch):
    t = pl.program_id(0)
    tiles = pl.num_programs(0)
    buf = t % 2
    per = ob_ref.shape[2]
    n_cls = ob_ref.shape[3]
    rows = per * n_cls

    def fetch(tile, b, fn):
        bi, ti = tile // tiles_per_batch, tile % tiles_per_batch
        for c in range(n_cls):
            cls = _slot_class(c)
            src_rows = pl.ds(ti * per, per)
            for g, (o_hbm, l_hbm) in enumerate(((o1_hbm, l1_hbm), (o2_hbm, l2_hbm))):
                fn(pltpu.make_async_copy(o_hbm.at[bi * n_cls + c, src_rows, :],
                                         ob_ref.at[b, g, :, cls, :], sem.at[b, 0]))
                fn(pltpu.make_async_copy(l_hbm.at[bi * n_cls + c, src_rows, :],
                                         lb_ref.at[b, g, :, cls, :], sem.at[b, 1]))

    @pl.when(t == 0)
    def _():
        fetch(0, 0, lambda copy: copy.start())

    @pl.when(t + 1 < tiles)
    def _():
        fetch(t + 1, 1 - buf, lambda copy: copy.start())

    fetch(t, buf, lambda copy: copy.wait())

    chunk = BAND
    for r0 in range(0, rows, chunk):
        rs = slice(r0, r0 + chunk)
        gs = slice(r0 // n_cls, (r0 + chunk) // n_cls)
        lses = (l0_ref[rs, :],
                lb_ref[buf, 0, gs, :, :].reshape(chunk, LANES),
                lb_ref[buf, 1, gs, :, :].reshape(chunk, LANES))
        o1 = ob_ref[buf, 0, gs, :, :].reshape(chunk, ATTN_OUT)
        o2 = ob_ref[buf, 1, gs, :, :].reshape(chunk, ATTN_OUT)
        mx = jnp.maximum(jnp.maximum(lses[0], lses[1]), lses[2])
        ex = [jnp.exp(l - mx) for l in lses]
        inv = 1.0 / (ex[0] + ex[1] + ex[2])
        wts = [e * inv for e in ex]
        spread = [[jnp.broadcast_to(w[:, h:h + 1], (chunk, HEAD_DIM)) for w in wts] for h in range(HEADS)]
        for h in range(HEADS):
            cols = slice(h * HEAD_DIM, (h + 1) * HEAD_DIM)
            merged = (spread[h][0] * o0_ref[rs, cols].astype(F32) + spread[h][1] * o1[:, cols]
                      + spread[h][2] * o2[:, cols])
            m_ref[rs, cols] = merged.astype(BF16)
    out_ref[...] = h_ref[...] + _bdot(m_ref[...], w_ref[...])


def _merge_out(h, seq, g0, g1, g2, w_o, rows=512):
    n, d = h.shape
    n_cls = ATTN_DILATIONS[2]
    per = rows // n_cls
    cls_len = seq // n_cls
    (o0, l0), (o1, l1), (o2, l2) = g0, g1, g2

    def by_class(a):
        return a.reshape(n // cls_len, cls_len, a.shape[-1])

    return pl.pallas_call(
        functools.partial(_merge_out_kernel, tiles_per_batch=seq // rows),
        grid=(n // rows,),
        in_specs=[
            pl.BlockSpec((rows, ATTN_OUT), lambda t: (t, 0)),
            pl.BlockSpec((rows, LANES), lambda t: (t, 0)),
            pl.BlockSpec((rows, d), lambda t: (t, 0)),
            pl.BlockSpec((ATTN_OUT, d), lambda t: (0, 0)),
            pl.BlockSpec(memory_space=pl.ANY),
            pl.BlockSpec(memory_space=pl.ANY),
            pl.BlockSpec(memory_space=pl.ANY),
            pl.BlockSpec(memory_space=pl.ANY),
        ],
        out_specs=pl.BlockSpec((rows, d), lambda t: (t, 0)),
        out_shape=jax.ShapeDtypeStruct((n, d), F32),
        scratch_shapes=[pltpu.VMEM((2, N_DILATED, per, n_cls, ATTN_OUT), F32),
                        pltpu.VMEM((2, N_DILATED, per, n_cls, LANES), F32),
                        pltpu.VMEM((rows, ATTN_OUT), BF16),
                        pltpu.SemaphoreType.DMA((2, 2))],
        compiler_params=_params("arbitrary"),
        name="attn_merge_out",
    )(o0, l0, h, w_o.astype(BF16), by_class(o1), by_class(l1), by_class(o2), by_class(l2))


def kernel(x, norm_mix_g, norm_ffn_g, conv_w_in, conv_b_in, conv_w_dw, conv_b_dw, conv_ln_g, conv_ln_b,
           conv_w_out, conv_b_out, norm_kv_g, w_kv, attn_w_q, attn_w_o, router_w_group, router_b_group,
           router_w_expert, router_b_expert, expert_w_gate, expert_w_up, expert_w_down, norm_final_g):
    b, s, d = x.shape
    n = b * s
    xf = x.reshape(n, d)

    u = _glu_in(xf, norm_mix_g[0], conv_w_in[0], conv_b_in[0])
    y = _dwconv(u.reshape(b, s, -1), conv_w_dw[0], conv_b_dw[0])
    h = _ln_out(y.reshape(n, -1), conv_ln_g[0], conv_ln_b[0], conv_w_out[0], conv_b_out[0], xf)
    h = _moe(h, norm_ffn_g[0], router_w_group[0], router_b_group[0], router_w_expert[0],
             router_b_expert[0], expert_w_gate, expert_w_up, expert_w_down, 0,
             norm_final_g, False)

    qkv0 = _qkv(h.reshape(b, s, d), norm_mix_g[1], norm_kv_g, attn_w_q[0], w_kv, 0)
    qkvd = _qkv_cls(h, b, norm_mix_g[1], norm_kv_g, attn_w_q[0], w_kv)
    g0 = _attn(qkv0, (0, 1, 2), s // BAND, BF16, "attn_g0")
    g1 = _attn_skew(qkvd, (0, 2, 4), F32, "attn_g1")
    g2 = _attn(qkvd, (1, 3, 5), 1, F32, "attn_g2")
    h = _merge_out(h, s, g0, g1, g2, attn_w_o[0])
    out = _moe(h, norm_ffn_g[1], router_w_group[1], router_b_group[1], router_w_expert[1],
               router_b_expert[1], expert_w_gate, expert_w_up, expert_w_down, 1,
               norm_final_g, True)
    return out.reshape(b, s, d)
```

```python
import functools

import jax
import jax.numpy as jnp
from jax import lax
from jax.experimental import pallas as pl
from jax.experimental.pallas import tpu as pltpu

F32 = jnp.float32
BF16 = jnp.bfloat16
I32 = jnp.int32

EPS = 1e-6
LANES = 128
V7X_VMEM_BYTES = 64 * 1024 * 1024
VMEM_LIMIT = V7X_VMEM_BYTES - 8 * 1024 * 1024

CONV_WIDTH = 31
HALO = 32
ATTN_DILATIONS = (1, 4, 16)
N_ATTN_GROUPS = 3
N_DILATED = 2
HEADS = 8
HEAD_DIM = 128
BAND = 128
BAND_LOG2 = 7
ATTN_OUT = HEADS * HEAD_DIM
N_GROUPS = 4
EXPERTS_PER_GROUP = 8
N_EXPERTS = N_GROUPS * EXPERTS_PER_GROUP

ROW_CHUNK = 32
ROW_UNROLL = 4
SUBLANES = 8
BULK_DMA_PRIORITY = 1


def _params(*sem):
    return pltpu.CompilerParams(dimension_semantics=sem, vmem_limit_bytes=VMEM_LIMIT)


def _row_loop(n_rows, body, chunk=ROW_CHUNK, unroll=ROW_UNROLL):
    def step(c, carry):
        body(pl.multiple_of(c * chunk, chunk))
        return carry
    lax.fori_loop(0, n_rows // chunk, step, 0, unroll=unroll)


def _rms(x, g):
    ms = jnp.mean(x * x, axis=-1, keepdims=True)
    return x * lax.rsqrt(ms + EPS) * g


def _bdot(a, b):
    return jnp.dot(a, b, preferred_element_type=F32)


def _prefetch_tile(i, n_tiles, fetch):
    buf = i % 2

    def start(copy):
        copy.start(priority=BULK_DMA_PRIORITY)

    @pl.when(i == 0)
    def _():
        fetch(0, 0, start)

    @pl.when(i + 1 < n_tiles)
    def _():
        fetch(i + 1, 1 - buf, start)

    fetch(i, buf, lambda copy: copy.wait())


def _glu_in_kernel(x_hbm, g_ref, wv_ref, wg_ref, bv_ref, bg_ref, o_ref, x_ref, xn_ref, sem):
    tm = x_ref.shape[1]
    i = pl.program_id(0)

    @pl.when(pl.program_id(1) == 0)
    def _():
        def fetch(tile, buf, fn):
            fn(pltpu.make_async_copy(x_hbm.at[pl.ds(tile * tm, tm), :], x_ref.at[buf], sem.at[buf]))
        _prefetch_tile(i, pl.num_programs(0), fetch)

        def norm(r0):
            rows = pl.ds(r0, ROW_CHUNK)
            xn_ref[rows, :] = _rms(x_ref[i % 2, rows, :], g_ref[...]).astype(BF16)
        _row_loop(tm, norm)

    xn = xn_ref[...]
    val = _bdot(xn, wv_ref[...].astype(BF16)) + bv_ref[...]
    gate = _bdot(xn, wg_ref[...].astype(BF16)) + bg_ref[...]
    o_ref[...] = (val * jax.nn.sigmoid(gate)).astype(o_ref.dtype)


def _glu_in(x, g, w_in, b_in, tm=1024, tn=512):
    n, d = x.shape
    c = w_in.shape[1] // 2
    nj = c // tn
    b2 = b_in.reshape(1, 2 * c)
    return pl.pallas_call(
        _glu_in_kernel,
        grid=(n // tm, nj),
        in_specs=[
            pl.BlockSpec(memory_space=pl.ANY),
            pl.BlockSpec((1, d), lambda i, j: (0, 0)),
            pl.BlockSpec((d, tn), lambda i, j: (0, j)),
            pl.BlockSpec((d, tn), lambda i, j: (0, j + nj)),
            pl.BlockSpec((1, tn), lambda i, j: (0, j)),
            pl.BlockSpec((1, tn), lambda i, j: (0, j + nj)),
        ],
        out_specs=pl.BlockSpec((tm, tn), lambda i, j: (i, j)),
        out_shape=jax.ShapeDtypeStruct((n, c), BF16),
        scratch_shapes=[pltpu.VMEM((2, tm, d), F32), pltpu.VMEM((tm, d), BF16), pltpu.SemaphoreType.DMA((2,))],
        compiler_params=_params("arbitrary", "arbitrary"),
        name="glu_in",
    )(x, g.reshape(1, d), w_in, w_in, b2, b2)


def _dwconv_kernel(cur_ref, halo_ref, w_ref, b_ref, o_ref, buf_ref):
    ts, cw = cur_ref.shape[1], cur_ref.shape[2]
    rw = 64
    keep = jnp.where(pl.program_id(1) > 0, 1.0, 0.0)
    buf_ref[0, 0:HALO, :] = halo_ref[0].astype(F32) * keep
    buf_ref[0, HALO:, :] = cur_ref[0].astype(F32)
    shifted_rows = ts + HALO - SUBLANES
    for s in range(1, SUBLANES):
        buf_ref[s, 0:shifted_rows, :] = buf_ref[0, s:s + shifted_rows, :]
    first = HALO - (CONV_WIDTH - 1)
    for c0 in range(0, cw, LANES):
        cols = slice(c0, c0 + LANES)
        for r0 in range(0, ts, rw):
            acc = jnp.broadcast_to(b_ref[:, cols], (rw, LANES))
            for k in range(CONV_WIDTH):
                s = (first + k) % SUBLANES
                start = r0 + first + k - s
                acc = acc + w_ref[k:k + 1, cols] * buf_ref[s, start:start + rw, cols]
            o_ref[0, r0:r0 + rw, cols] = acc.astype(o_ref.dtype)


def _dwconv(u, w_dw, b_dw, ts=256, cw=512):
    b, s, c = u.shape
    hb = ts // HALO
    return pl.pallas_call(
        _dwconv_kernel,
        grid=(b, s // ts, c // cw),
        in_specs=[
            pl.BlockSpec((1, ts, cw), lambda bi, si, ci: (bi, si, ci)),
            pl.BlockSpec((1, HALO, cw), lambda bi, si, ci: (bi, jnp.maximum(si * hb - 1, 0), ci)),
            pl.BlockSpec((CONV_WIDTH, cw), lambda bi, si, ci: (0, ci)),
            pl.BlockSpec((1, cw), lambda bi, si, ci: (0, ci)),
        ],
        out_specs=pl.BlockSpec((1, ts, cw), lambda bi, si, ci: (bi, si, ci)),
        out_shape=jax.ShapeDtypeStruct((b, s, c), BF16),
        scratch_shapes=[pltpu.VMEM((SUBLANES, ts + HALO, cw), F32)],
        compiler_params=_params("arbitrary", "arbitrary", "arbitrary"),
        name="dwconv",
    )(u, u, w_dw, b_dw.reshape(1, c))


def _ln_out_kernel(y_hbm, lg_ref, lb_ref, w_ref, b_ref, res_ref, o_ref, y_ref, a_ref, sem):
    tm = y_ref.shape[1]
    i = pl.program_id(0)

    @pl.when(pl.program_id(1) == 0)
    def _():
        def fetch(tile, buf, fn):
            fn(pltpu.make_async_copy(y_hbm.at[pl.ds(tile * tm, tm), :], y_ref.at[buf], sem.at[buf]))
        _prefetch_tile(i, pl.num_programs(0), fetch)

        def norm(r0):
            rows = pl.ds(r0, ROW_CHUNK)
            y = y_ref[i % 2, rows, :].astype(F32)
            mu = jnp.mean(y, axis=-1, keepdims=True)
            yc = y - mu
            var = jnp.mean(yc * yc, axis=-1, keepdims=True)
            z = yc * lax.rsqrt(var + EPS) * lg_ref[...] + lb_ref[...]
            a_ref[rows, :] = (z * jax.nn.sigmoid(z)).astype(BF16)
        _row_loop(tm, norm)

    o_ref[...] = res_ref[...] + _bdot(a_ref[...], w_ref[...].astype(BF16)) + b_ref[...]


def _ln_out(y, ln_g, ln_b, w_out, b_out, res, tm=1024, tn=512):
    n, c = y.shape
    d = w_out.shape[1]
    return pl.pallas_call(
        _ln_out_kernel,
        grid=(n // tm, d // tn),
        in_specs=[
            pl.BlockSpec(memory_space=pl.ANY),
            pl.BlockSpec((1, c), lambda i, j: (0, 0)),
            pl.BlockSpec((1, c), lambda i, j: (0, 0)),
            pl.BlockSpec((c, tn), lambda i, j: (0, j)),
            pl.BlockSpec((1, tn), lambda i, j: (0, j)),
            pl.BlockSpec((tm, tn), lambda i, j: (i, j)),
        ],
        out_specs=pl.BlockSpec((tm, tn), lambda i, j: (i, j)),
        out_shape=jax.ShapeDtypeStruct((n, d), F32),
        scratch_shapes=[pltpu.VMEM((2, tm, c), BF16), pltpu.VMEM((tm, c), BF16), pltpu.SemaphoreType.DMA((2,))],
        compiler_params=_params("arbitrary", "arbitrary"),
        name="ln_out",
    )(y, ln_g.reshape(1, c), ln_b.reshape(1, c), w_out, b_out.reshape(1, d), res)


def _split_bf16(a):
    hi = a.astype(BF16)
    lo = (a - hi.astype(F32)).astype(BF16)
    return hi, lo


def _router_kernel(h_ref, g_ref, w_ref, b_ref, ids_ref, wts_ref):
    t = _rms(h_ref[...], g_ref[...])
    t_hi, t_lo = _split_bf16(t)
    w_hi, w_lo = _split_bf16(w_ref[...])
    logits = _bdot(t_hi, w_hi) + _bdot(t_hi, w_lo) + _bdot(t_lo, w_hi) + b_ref[...]

    lane = lax.broadcasted_iota(I32, logits.shape, 1)
    lanef = lane.astype(F32)
    neg = -jnp.inf
    is_group = lane < N_GROUPS
    glog = jnp.where(is_group, logits, neg)
    gmax = jnp.max(glog, axis=1, keepdims=True)
    gi = jnp.min(jnp.where(glog == gmax, lanef, float(LANES)), axis=1, keepdims=True)
    gsum = jnp.sum(jnp.where(is_group, jnp.exp(logits - gmax), 0.0), axis=1, keepdims=True)
    gp = 1.0 / gsum

    lo_lane = N_GROUPS + gi * EXPERTS_PER_GROUP
    in_sel = (lanef >= lo_lane) & (lanef < lo_lane + EXPERTS_PER_GROUP)
    sel = jnp.where(in_sel, logits, neg)
    m1 = jnp.max(sel, axis=1, keepdims=True)
    i1 = jnp.min(jnp.where(sel == m1, lanef, float(LANES)), axis=1, keepdims=True)
    sel2 = jnp.where(lanef == i1, neg, sel)
    m2 = jnp.max(sel2, axis=1, keepdims=True)
    i2 = jnp.min(jnp.where(sel2 == m2, lanef, float(LANES)), axis=1, keepdims=True)
    e2 = jnp.exp(m2 - m1)
    w1 = gp / (1.0 + e2)
    w2 = gp * e2 / (1.0 + e2)

    ids = jnp.where(lane == 0, i1 - N_GROUPS, jnp.where(lane == 1, i2 - N_GROUPS, 0.0))
    ids_ref[...] = ids.astype(I32)
    wts_ref[...] = jnp.where(lane == 0, w1, jnp.where(lane == 1, w2, 0.0))


def _router(h, g, w_group, b_group, w_expert, b_expert, tm=512):
    n, d = h.shape
    w_e = jnp.transpose(w_expert, (1, 0, 2)).reshape(d, N_EXPERTS)
    pad = LANES - N_GROUPS - N_EXPERTS
    wr = jnp.concatenate([w_group, w_e, jnp.zeros((d, pad), F32)], axis=1)
    br = jnp.concatenate([b_group, b_expert.reshape(N_EXPERTS), jnp.zeros((pad,), F32)]).reshape(1, LANES)
    return pl.pallas_call(
        _router_kernel,
        grid=(n // tm,),
        in_specs=[
            pl.BlockSpec((tm, d), lambda i: (i, 0)),
            pl.BlockSpec((1, d), lambda i: (0, 0)),
            pl.BlockSpec((d, LANES), lambda i: (0, 0)),
            pl.BlockSpec((1, LANES), lambda i: (0, 0)),
        ],
        out_specs=[pl.BlockSpec((tm, LANES), lambda i: (i, 0)),
                   pl.BlockSpec((tm, LANES), lambda i: (i, 0))],
        out_shape=[jax.ShapeDtypeStruct((n, LANES), I32), jax.ShapeDtypeStruct((n, LANES), F32)],
        compiler_params=_params("arbitrary"),
        name="router",
    )(h, g.reshape(1, d), wr, br)


def _dispatch_plan(ids, tm, n_items):
    e_flat = jnp.concatenate([ids[:, 0], ids[:, 1]])
    onehot = (e_flat[:, None] == jnp.arange(N_EXPERTS, dtype=I32)[None, :]).astype(I32)
    csum = jnp.cumsum(onehot, axis=0)
    rank = jnp.sum(onehot * csum, axis=1) - 1
    counts = csum[-1]
    ends = jnp.cumsum(counts)
    starts = ends - counts
    dest = jnp.sum(onehot * starts[None, :], axis=1) + rank

    first_tile = starts // tm
    last_tile = jnp.maximum(ends - 1, 0) // tm
    items = jnp.where(counts > 0, last_tile - first_tile + 1, 0)
    item_end = jnp.cumsum(items)
    item_start = item_end - items
    total = item_end[-1]
    w = jnp.arange(n_items, dtype=I32)
    valid = w < total
    wc = jnp.minimum(w, total - 1)
    e_w = jnp.searchsorted(item_end, wc, side="right").astype(I32)
    tile_w = first_tile[e_w] + (wc - item_start[e_w])
    lo = jnp.maximum(starts[e_w], tile_w * tm) - tile_w * tm
    hi = jnp.minimum(ends[e_w], (tile_w + 1) * tm) - tile_w * tm
    lo = jnp.where(valid, lo, 0)
    hi = jnp.where(valid, hi, 0)
    prev_tile = jnp.concatenate([jnp.full((1,), -1, I32), tile_w[:-1]])
    prev_e = jnp.concatenate([jnp.full((1,), -1, I32), e_w[:-1]])
    first = (valid & (tile_w != prev_tile)).astype(I32)
    new_e = (valid & (e_w != prev_e)).astype(I32)

    present = counts > 0
    n_present = jnp.sum(present.astype(I32))
    order = jnp.argsort(jnp.logical_not(present), stable=True).astype(I32)
    ordinal = jnp.cumsum(present.astype(I32)) - 1
    k_w = ordinal[e_w]
    slot = k_w % 2
    ahead = jnp.where(k_w + 2 < n_present, order[jnp.minimum(k_w + 2, N_EXPERTS - 1)], -1)
    head = jnp.stack([order[0], jnp.where(n_present > 1, order[1], -1)])
    plan = (tile_w.astype(I32), e_w, lo.astype(I32), hi.astype(I32), first, new_e, valid.astype(I32),
            slot.astype(I32), ahead.astype(I32), head.astype(I32))
    return dest.astype(I32), plan


def _scatter_kernel(dest_ref, h_ref, g_ref, xs_ref, t_ref, sem):
    tm = h_ref.shape[0]
    i = pl.program_id(0)
    steps = pl.num_programs(0)
    n = steps * tm
    buf = i % 2
    groups = tm // SUBLANES

    def row_copy(step, b, grp, j, pick):
        d = dest_ref[pick * n + step * tm + grp * SUBLANES + j]
        return pltpu.make_async_copy(t_ref.at[b, grp, pl.ds(j, 1), :], xs_ref.at[pl.ds(d, 1), :], sem.at[b])

    def for_group(step, b, grp, fn):
        for j in range(SUBLANES):
            fn(row_copy(step, b, grp, j, 0), 0)
            fn(row_copy(step, b, grp, j, 1), 1)

    def for_rows(step, b, fn):
        def body(grp, carry):
            for_group(step, b, grp, fn)
            return carry
        lax.fori_loop(0, groups, body, 0)

    @pl.when(i >= 2)
    def _():
        for_rows(i - 2, buf, lambda copy, pick: copy.wait())

    per = ROW_CHUNK // SUBLANES
    chunks = groups // per

    def norm(c):
        rows = pl.ds(pl.multiple_of(c * ROW_CHUNK, ROW_CHUNK), ROW_CHUNK)
        t = _rms(h_ref[rows, :], g_ref[...])
        t_ref[buf, pl.ds(pl.multiple_of(c * per, per), per)] = t.reshape(per, SUBLANES, t.shape[-1])

    def start(c):
        for k in range(per):
            for_group(i, buf, c * per + k, lambda copy, pick: copy.start(priority=pick))

    norm(0)

    def body(c, carry):
        start(c)
        norm(c + 1)
        return carry
    lax.fori_loop(0, chunks - 1, body, 0)
    start(chunks - 1)

    @pl.when(i == steps - 1)
    def _():
        @pl.when(i >= 1)
        def _():
            for_rows(i - 1, 1 - buf, lambda copy, pick: copy.wait())
        for_rows(i, buf, lambda copy, pick: copy.wait())


def _scatter(h, g, dest, tm=256):
    n, d = h.shape
    return pl.pallas_call(
        _scatter_kernel,
        grid_spec=pltpu.PrefetchScalarGridSpec(
            num_scalar_prefetch=1,
            grid=(n // tm,),
            in_specs=[pl.BlockSpec((tm, d), lambda i, dest: (i, 0)),
                      pl.BlockSpec((1, d), lambda i, dest: (0, 0))],
            out_specs=pl.BlockSpec(memory_space=pl.ANY),
            scratch_shapes=[pltpu.VMEM((2, tm // SUBLANES, SUBLANES, d), F32),
                            pltpu.SemaphoreType.DMA((2,))],
        ),
        out_shape=jax.ShapeDtypeStruct((2 * n, d), F32),
        compiler_params=_params("arbitrary"),
        name="moe_scatter",
    )(dest, h, g.reshape(1, d))


def _experts_kernel(tile_ref, exp_ref, lo_ref, hi_ref, first_ref, new_ref, valid_ref, slot_ref,
                    ahead_ref, head_ref, xs_ref, wg_hbm, wu_hbm, wd_hbm, o_ref,
                    wg_st, wu_st, wd_st, wg_bf, wu_bf, wd_bf, sem, *, layer):
    w = pl.program_id(0)
    tm = xs_ref.shape[0]
    chunk = 256

    def fetch(e, s):
        return (pltpu.make_async_copy(wg_hbm.at[layer, e], wg_st.at[s], sem.at[s, 0]),
                pltpu.make_async_copy(wu_hbm.at[layer, e], wu_st.at[s], sem.at[s, 1]),
                pltpu.make_async_copy(wd_hbm.at[layer, e], wd_st.at[s], sem.at[s, 2]))

    @pl.when(w == 0)
    def _():
        for copy in fetch(head_ref[0], 0):
            copy.start(priority=BULK_DMA_PRIORITY)

        @pl.when(head_ref[1] >= 0)
        def _():
            for copy in fetch(head_ref[1], 1):
                copy.start(priority=BULK_DMA_PRIORITY)

    @pl.when(new_ref[w] == 1)
    def _():
        s = slot_ref[w]
        for copy in fetch(exp_ref[w], s):
            copy.wait()

        def cast_in(r0):
            rows = pl.ds(r0, chunk)
            wg_bf[rows, :] = wg_st[s, rows, :].astype(BF16)
            wu_bf[rows, :] = wu_st[s, rows, :].astype(BF16)
        _row_loop(wg_bf.shape[0], cast_in, chunk)

        def cast_down(r0):
            rows = pl.ds(r0, chunk)
            wd_bf[rows, :] = wd_st[s, rows, :].astype(BF16)
        _row_loop(wd_bf.shape[0], cast_down, chunk)

        @pl.when(ahead_ref[w] >= 0)
        def _():
            for copy in fetch(ahead_ref[w], s):
                copy.start(priority=BULK_DMA_PRIORITY)

    @pl.when(valid_ref[w] == 1)
    def _():
        row = lax.broadcasted_iota(I32, (tm, 1), 0)
        mine = (row >= lo_ref[w]) & (row < hi_ref[w])
        x = jnp.where(mine, xs_ref[...], 0.0).astype(BF16)
        hg = _bdot(x, wg_bf[...])
        hu = _bdot(x, wu_bf[...])
        act = (hg * jax.nn.sigmoid(hg) * hu).astype(BF16)
        y = _bdot(act, wd_bf[...])

        @pl.when(first_ref[w] == 1)
        def _():
            o_ref[...] = y

        @pl.when(first_ref[w] == 0)
        def _():
            o_ref[...] += y


def _experts(xs, plan, w_gate, w_up, w_down, layer, tm):
    p, d = xs.shape
    f = w_gate.shape[3]
    n_items = plan[0].shape[0]
    return pl.pallas_call(
        functools.partial(_experts_kernel, layer=layer),
        grid_spec=pltpu.PrefetchScalarGridSpec(
            num_scalar_prefetch=len(plan),
            grid=(n_items,),
            in_specs=[
                pl.BlockSpec((tm, d), lambda w, tile, *_: (tile[w], 0)),
                pl.BlockSpec(memory_space=pl.ANY),
                pl.BlockSpec(memory_space=pl.ANY),
                pl.BlockSpec(memory_space=pl.ANY),
            ],
            out_specs=pl.BlockSpec((tm, d), lambda w, tile, *_: (tile[w], 0)),
            scratch_shapes=[
                pltpu.VMEM((2, d, f), F32), pltpu.VMEM((2, d, f), F32), pltpu.VMEM((2, f, d), F32),
                pltpu.VMEM((d, f), BF16), pltpu.VMEM((d, f), BF16), pltpu.VMEM((f, d), BF16),
                pltpu.SemaphoreType.DMA((2, 3)),
            ],
        ),
        out_shape=jax.ShapeDtypeStruct((p, d), F32),
        compiler_params=_params("arbitrary"),
        name="moe_experts",
    )(*plan, xs, w_gate, w_up, w_down)


def _combine_kernel(dest_ref, h_ref, wts_ref, g_ref, ys_ref, o_ref, y_ref, sem, *, final_norm):
    tm = h_ref.shape[0]
    i = pl.program_id(0)
    steps = pl.num_programs(0)
    n = steps * tm
    buf = i % 2
    groups = tm // SUBLANES

    def row_copy(step, b, grp, j, pick):
        d = dest_ref[pick * n + step * tm + grp * SUBLANES + j]
        return pltpu.make_async_copy(ys_ref.at[pl.ds(d, 1), :], y_ref.at[b, pick, grp, pl.ds(j, 1), :],
                                     sem.at[b])

    def for_group(step, b, grp, fn):
        for j in range(SUBLANES):
            fn(row_copy(step, b, grp, j, 0))
            fn(row_copy(step, b, grp, j, 1))

    def for_rows(step, b, fn):
        def body(grp, carry):
            for_group(step, b, grp, fn)
            return carry
        lax.fori_loop(0, groups, body, 0)

    per = ROW_CHUNK // SUBLANES
    chunks = groups // per

    def mix(c):
        rows = pl.ds(pl.multiple_of(c * ROW_CHUNK, ROW_CHUNK), ROW_CHUNK)
        grps = pl.ds(pl.multiple_of(c * per, per), per)
        wts = wts_ref[rows, :]
        y0 = y_ref[buf, 0, grps].reshape(ROW_CHUNK, -1)
        y1 = y_ref[buf, 1, grps].reshape(ROW_CHUNK, -1)
        out = h_ref[rows, :] + wts[:, 0:1] * y0 + wts[:, 1:2] * y1
        if final_norm:
            out = _rms(out, g_ref[...])
        o_ref[rows, :] = out

    @pl.when(i == 0)
    def _():
        for_rows(0, 0, lambda copy: copy.start())

    for_rows(i, buf, lambda copy: copy.wait())

    @pl.when(i + 1 < steps)
    def _():
        def body(c, carry):
            for k in range(per):
                for_group(i + 1, 1 - buf, c * per + k, lambda copy: copy.start())
            mix(c)
            return carry
        lax.fori_loop(0, chunks, body, 0)

    @pl.when(i + 1 == steps)
    def _():
        def body(c, carry):
            mix(c)
            return carry
        lax.fori_loop(0, chunks, body, 0, unroll=ROW_UNROLL)


def _combine(h, wts, ys, dest, g_final, final_norm, tm=256):
    n, d = h.shape
    return pl.pallas_call(
        functools.partial(_combine_kernel, final_norm=final_norm),
        grid_spec=pltpu.PrefetchScalarGridSpec(
            num_scalar_prefetch=1,
            grid=(n // tm,),
            in_specs=[pl.BlockSpec((tm, d), lambda i, dest: (i, 0)),
                      pl.BlockSpec((tm, LANES), lambda i, dest: (i, 0)),
                      pl.BlockSpec((1, d), lambda i, dest: (0, 0)),
                      pl.BlockSpec(memory_space=pl.ANY)],
            out_specs=pl.BlockSpec((tm, d), lambda i, dest: (i, 0)),
            scratch_shapes=[pltpu.VMEM((2, 2, tm // SUBLANES, SUBLANES, d), F32),
                            pltpu.SemaphoreType.DMA((2,))],
        ),
        out_shape=jax.ShapeDtypeStruct((n, d), F32),
        compiler_params=_params("arbitrary"),
        name="moe_combine",
    )(dest, h, wts, g_final.reshape(1, d), ys)


def _moe(h, g, w_group, b_group, w_expert, b_expert, w_gate, w_up, w_down, layer, g_final, final_norm,
         tm=256):
    n = h.shape[0]
    ids, wts = _router(h, g, w_group, b_group, w_expert, b_expert)
    n_items = 2 * n // tm + N_EXPERTS
    dest, plan = _dispatch_plan(ids, tm, n_items)
    xs = _scatter(h, g, dest)
    ys = _experts(xs, plan, w_gate, w_up, w_down, layer, tm)
    return _combine(h, wts, ys, dest, g_final, final_norm)


def _dot_t(a, b):
    return lax.dot_general(a, b, (((1,), (1,)), ((), ())), preferred_element_type=F32)


def _lane_slab(columns):
    lane = lax.broadcasted_iota(I32, (columns[0].shape[0], LANES), 1)
    slab = jnp.zeros(lane.shape, F32)
    for h, col in enumerate(columns):
        slab = jnp.where(lane == h, col, slab)
    return slab


def _attn_kernel(q_ref, kc_ref, vc_ref, kp_ref, vp_ref, o_ref, lse_ref, *, seg_blocks):
    qblocks = q_ref.shape[0] // BAND
    step = pl.program_id(0)
    banded = seg_blocks > 1
    assert not banded or seg_blocks % qblocks == 0
    n_keys = 2 * BAND if banded else BAND
    qi = lax.broadcasted_iota(I32, (BAND, n_keys), 0)
    kj = lax.broadcasted_iota(I32, (BAND, n_keys), 1)
    mask = (kj >= qi) & (kj <= qi + BAND) if banded else kj <= qi
    scale = HEAD_DIM ** -0.5
    neg = -jnp.inf
    heads = [slice(h * HEAD_DIM, (h + 1) * HEAD_DIM) for h in range(HEADS)]

    for sb in range(qblocks):
        rows = slice(sb * BAND, (sb + 1) * BAND)
        if not banded:
            scores = [_dot_t(q_ref[rows, c], kc_ref[rows, c]) for c in heads]
        elif sb > 0:
            krows = slice((sb - 1) * BAND, (sb + 1) * BAND)
            scores = [_dot_t(q_ref[rows, c], kc_ref[krows, c]) for c in heads]
        else:
            scores = [jnp.concatenate([_dot_t(q_ref[rows, c], kp_ref[:, c]),
                                       _dot_t(q_ref[rows, c], kc_ref[rows, c])], axis=1) for c in heads]
        scores = [jnp.where(mask, s * scale, neg) for s in scores]
        if banded and sb == 0:
            at_start = (step * qblocks) % seg_blocks == 0
            drop = jnp.where(kj < BAND, jnp.where(at_start, neg, 0.0), 0.0)
            scores = [s + drop for s in scores]
        ms = [jnp.max(s, axis=1, keepdims=True) for s in scores]
        ps = [jnp.exp(s - m) for s, m in zip(scores, ms)]
        ls = [jnp.sum(p, axis=1, keepdims=True) for p in ps]
        pb = [p.astype(BF16) for p in ps]
        if not banded:
            accs = [_bdot(p, vc_ref[rows, c]) for p, c in zip(pb, heads)]
        elif sb > 0:
            accs = [_bdot(p, vc_ref[krows, c]) for p, c in zip(pb, heads)]
        else:
            accs = [_bdot(p[:, :BAND], vp_ref[:, c]) + _bdot(p[:, BAND:], vc_ref[rows, c])
                    for p, c in zip(pb, heads)]
        for acc, l, c in zip(accs, ls, heads):
            o_ref[rows, c] = (acc / l).astype(o_ref.dtype)
        lse_ref[rows, :] = _lane_slab([m + jnp.log(l) for m, l in zip(ms, ls)])


def _attn(qkv, cols, seg_blocks, out_dtype, name, rows_per_step=512):
    n = qkv.shape[0]
    cq, ck, cv = cols
    qblocks = rows_per_step // BAND
    return pl.pallas_call(
        functools.partial(_attn_kernel, seg_blocks=seg_blocks),
        grid=(n // rows_per_step,),
        in_specs=[
            pl.BlockSpec((rows_per_step, ATTN_OUT), lambda i: (i, cq)),
            pl.BlockSpec((rows_per_step, ATTN_OUT), lambda i: (i, ck)),
            pl.BlockSpec((rows_per_step, ATTN_OUT), lambda i: (i, cv)),
            pl.BlockSpec((BAND, ATTN_OUT), lambda i: (jnp.maximum(i * qblocks - 1, 0), ck)),
            pl.BlockSpec((BAND, ATTN_OUT), lambda i: (jnp.maximum(i * qblocks - 1, 0), cv)),
        ],
        out_specs=[pl.BlockSpec((rows_per_step, ATTN_OUT), lambda i: (i, 0)),
                   pl.BlockSpec((rows_per_step, LANES), lambda i: (i, 0))],
        out_shape=[jax.ShapeDtypeStruct((n, ATTN_OUT), out_dtype), jax.ShapeDtypeStruct((n, LANES), F32)],
        compiler_params=_params("arbitrary"),
        name=name,
    )(qkv, qkv, qkv, qkv, qkv)


def _attn_skew_kernel(q_ref, k_ref, v_ref, o_ref, lse_ref):
    subs = q_ref.shape[0] // BAND
    keys = k_ref.shape[0]
    u = lax.broadcasted_iota(I32, (BAND, keys), 0)
    kc = lax.broadcasted_iota(I32, (BAND, keys), 1)
    base = subs * (u - (kc & (BAND - 1))) - lax.shift_right_logical(kc, BAND_LOG2)
    scale = HEAD_DIM ** -0.5
    neg = -jnp.inf
    heads = [slice(h * HEAD_DIM, (h + 1) * HEAD_DIM) for h in range(HEADS)]
    for s in range(subs):
        rows = slice(s * BAND, (s + 1) * BAND)
        delta = base + s
        mask = (delta >= 0) & (delta <= BAND)
        scores = [jnp.where(mask, _dot_t(q_ref[rows, c], k_ref[:, c]) * scale, neg) for c in heads]
        ms = [jnp.max(sc, axis=1, keepdims=True) for sc in scores]
        ps = [jnp.exp(sc - m) for sc, m in zip(scores, ms)]
        ls = [jnp.sum(p, axis=1, keepdims=True) for p in ps]
        accs = [_bdot(p.astype(BF16), v_ref[:, c]) for p, c in zip(ps, heads)]
        for acc, l, c in zip(accs, ls, heads):
            o_ref[rows, c] = (acc / l).astype(o_ref.dtype)
        lse_ref[rows, :] = _lane_slab([m + jnp.log(l) for m, l in zip(ms, ls)])


def _attn_skew(qkv, cols, out_dtype, name):
    n = qkv.shape[0]
    cq, ck, cv = cols
    rows = (ATTN_DILATIONS[2] // ATTN_DILATIONS[1]) * BAND
    return pl.pallas_call(
        _attn_skew_kernel,
        grid=(n // rows,),
        in_specs=[pl.BlockSpec((rows, ATTN_OUT), lambda i: (i, cq)),
                  pl.BlockSpec((rows, ATTN_OUT), lambda i: (i, ck)),
                  pl.BlockSpec((rows, ATTN_OUT), lambda i: (i, cv))],
        out_specs=[pl.BlockSpec((rows, ATTN_OUT), lambda i: (i, 0)),
                   pl.BlockSpec((rows, LANES), lambda i: (i, 0))],
        out_shape=[jax.ShapeDtypeStruct((n, ATTN_OUT), out_dtype), jax.ShapeDtypeStruct((n, LANES), F32)],
        compiler_params=_params("arbitrary"),
        name=name,
    )(qkv, qkv, qkv)


def _slot_class(c):
    return (c % 4) * 4 + c // 4


def _qkv_kernel(h_hbm, gq_ref, gkv_ref, wq_ref, wkv_ref, o_ref, x_ref, xq_ref, xkv_ref, sem,
                *, tiles_per_batch, n_groups):
    i, j = pl.program_id(0), pl.program_id(1)
    tiles = pl.num_programs(0)
    tm = x_ref.shape[1]
    q_blocks = n_groups * ATTN_OUT // o_ref.shape[1]

    def fetch(tile, buf, fn):
        if n_groups == 1:
            fn(pltpu.make_async_copy(h_hbm.at[pl.ds(tile * tm, tm), :], x_ref.at[buf], sem.at[buf]))
            return
        per_class = h_hbm.shape[0] // (tiles // tiles_per_batch)
        slots = tm // per_class
        bi, ti = tile // tiles_per_batch, tile % tiles_per_batch
        for c in range(slots):
            cls = _slot_class(ti * slots + c)
            fn(pltpu.make_async_copy(h_hbm.at[pl.ds(bi * per_class, per_class), cls, :],
                                     x_ref.at[buf, pl.ds(c * per_class, per_class), :],
                                     sem.at[buf]))

    @pl.when(j == 0)
    def _():
        buf = i % 2
        _prefetch_tile(i, tiles, fetch)

        def norm(r0):
            x = x_ref[buf, pl.ds(r0, ROW_CHUNK), :]
            xhat = x * lax.rsqrt(jnp.mean(x * x, axis=-1, keepdims=True) + EPS)
            rows = pl.ds(r0, ROW_CHUNK)
            xq_ref[rows, :] = (xhat * gq_ref[...]).astype(BF16)
            xkv_ref[rows, :] = (xhat * gkv_ref[...]).astype(BF16)
        _row_loop(tm, norm)

    @pl.when(j < q_blocks)
    def _():
        o_ref[...] = _bdot(xq_ref[...], wq_ref[...].astype(BF16)).astype(o_ref.dtype)

    @pl.when(j >= q_blocks)
    def _():
        o_ref[...] = _bdot(xkv_ref[...], wkv_ref[...].astype(BF16)).astype(o_ref.dtype)


def _qkv(h, batch, g_q, g_kv, w_q, w_kv, dilated, tm=1024, tn=512):
    n, d = h.shape
    first, n_groups = (1, N_DILATED) if dilated else (0, 1)
    r2 = ATTN_DILATIONS[2]
    hv = h.reshape(n // r2, r2, d) if dilated else h
    per = ATTN_OUT // tn
    q_blocks = n_groups * per
    kv_blocks = 2 * n_groups * per

    def wq_map(i, j):
        return (0, first * per + jnp.minimum(j, q_blocks - 1))

    def wkv_map(i, j):
        jj = jnp.clip(j - q_blocks, 0, kv_blocks - 1)
        return (0, first * per + jj + jnp.where(jj >= q_blocks, (N_ATTN_GROUPS - n_groups) * per, 0))

    return pl.pallas_call(
        functools.partial(_qkv_kernel, tiles_per_batch=(n // batch) // tm, n_groups=n_groups),
        grid=(n // tm, q_blocks + kv_blocks),
        in_specs=[
            pl.BlockSpec(memory_space=pl.ANY),
            pl.BlockSpec((1, d), lambda i, j: (0, 0)),
            pl.BlockSpec((1, d), lambda i, j: (0, 0)),
            pl.BlockSpec((d, tn), wq_map),
            pl.BlockSpec((d, tn), wkv_map),
        ],
        out_specs=pl.BlockSpec((tm, tn), lambda i, j: (i, j)),
        out_shape=jax.ShapeDtypeStruct((n, 3 * n_groups * ATTN_OUT), BF16),
        scratch_shapes=[pltpu.VMEM((2, tm, d), F32), pltpu.VMEM((tm, d), BF16), pltpu.VMEM((tm, d), BF16),
                        pltpu.SemaphoreType.DMA((2,))],
        compiler_params=_params("arbitrary", "arbitrary"),
        name="qkv_dilated" if dilated else "qkv_g0",
    )(hv, g_q.reshape(1, d), g_kv.reshape(1, d), w_q, w_kv)


def _merge_out_kernel(o0_ref, l0_ref, h_ref, w_ref, o1_hbm, l1_hbm, o2_hbm, l2_hbm, out_ref,
                      ob_ref, lb_ref, m_ref, sem, *, tiles_per_batch):
    t = pl.program_id(0)
    tiles = pl.num_programs(0)
    buf = t % 2
    per = ob_ref.shape[2]
    n_cls = ob_ref.shape[3]
    rows = per * n_cls

    def fetch(tile, b, fn):
        bi, ti = tile // tiles_per_batch, tile % tiles_per_batch
        for c in range(n_cls):
            cls = _slot_class(c)
            src_rows = pl.ds(ti * per, per)
            for g, (o_hbm, l_hbm) in enumerate(((o1_hbm, l1_hbm), (o2_hbm, l2_hbm))):
                fn(pltpu.make_async_copy(o_hbm.at[bi * n_cls + c, src_rows, :],
                                         ob_ref.at[b, g, :, cls, :], sem.at[b, 0]))
                fn(pltpu.make_async_copy(l_hbm.at[bi * n_cls + c, src_rows, :],
                                         lb_ref.at[b, g, :, cls, :], sem.at[b, 1]))

    _prefetch_tile(t, tiles, fetch)

    chunk = BAND
    for r0 in range(0, rows, chunk):
        rs = slice(r0, r0 + chunk)
        gs = slice(r0 // n_cls, (r0 + chunk) // n_cls)
        lses = (l0_ref[rs, :],
                lb_ref[buf, 0, gs, :, :].reshape(chunk, LANES),
                lb_ref[buf, 1, gs, :, :].reshape(chunk, LANES))
        o1 = ob_ref[buf, 0, gs, :, :].reshape(chunk, ATTN_OUT)
        o2 = ob_ref[buf, 1, gs, :, :].reshape(chunk, ATTN_OUT)
        mx = jnp.maximum(jnp.maximum(lses[0], lses[1]), lses[2])
        ex = [jnp.exp(l - mx) for l in lses]
        inv = 1.0 / (ex[0] + ex[1] + ex[2])
        wts = [e * inv for e in ex]
        spread = [[jnp.broadcast_to(w[:, h:h + 1], (chunk, HEAD_DIM)) for w in wts] for h in range(HEADS)]
        for h in range(HEADS):
            cols = slice(h * HEAD_DIM, (h + 1) * HEAD_DIM)
            merged = (spread[h][0] * o0_ref[rs, cols].astype(F32) + spread[h][1] * o1[:, cols]
                      + spread[h][2] * o2[:, cols])
            m_ref[rs, cols] = merged.astype(BF16)
    out_ref[...] = h_ref[...] + _bdot(m_ref[...], w_ref[...])


def _merge_out(h, seq, g0, g1, g2, w_o, rows=512):
    n, d = h.shape
    n_cls = ATTN_DILATIONS[2]
    per = rows // n_cls
    cls_len = seq // n_cls
    (o0, l0), (o1, l1), (o2, l2) = g0, g1, g2

    def by_class(a):
        return a.reshape(n // cls_len, cls_len, a.shape[-1])

    return pl.pallas_call(
        functools.partial(_merge_out_kernel, tiles_per_batch=seq // rows),
        grid=(n // rows,),
        in_specs=[
            pl.BlockSpec((rows, ATTN_OUT), lambda t: (t, 0)),
            pl.BlockSpec((rows, LANES), lambda t: (t, 0)),
            pl.BlockSpec((rows, d), lambda t: (t, 0)),
            pl.BlockSpec((ATTN_OUT, d), lambda t: (0, 0)),
            pl.BlockSpec(memory_space=pl.ANY),
            pl.BlockSpec(memory_space=pl.ANY),
            pl.BlockSpec(memory_space=pl.ANY),
            pl.BlockSpec(memory_space=pl.ANY),
        ],
        out_specs=pl.BlockSpec((rows, d), lambda t: (t, 0)),
        out_shape=jax.ShapeDtypeStruct((n, d), F32),
        scratch_shapes=[pltpu.VMEM((2, N_DILATED, per, n_cls, ATTN_OUT), F32),
                        pltpu.VMEM((2, N_DILATED, per, n_cls, LANES), F32),
                        pltpu.VMEM((rows, ATTN_OUT), BF16),
                        pltpu.SemaphoreType.DMA((2, 2))],
        compiler_params=_params("arbitrary"),
        name="attn_merge_out",
    )(o0, l0, h, w_o.astype(BF16), by_class(o1), by_class(l1), by_class(o2), by_class(l2))


def kernel(x, norm_mix_g, norm_ffn_g, conv_w_in, conv_b_in, conv_w_dw, conv_b_dw, conv_ln_g, conv_ln_b,
           conv_w_out, conv_b_out, norm_kv_g, w_kv, attn_w_q, attn_w_o, router_w_group, router_b_group,
           router_w_expert, router_b_expert, expert_w_gate, expert_w_up, expert_w_down, norm_final_g):
    b, s, d = x.shape
    n = b * s
    xf = x.reshape(n, d)

    u = _glu_in(xf, norm_mix_g[0], conv_w_in[0], conv_b_in[0])
    y = _dwconv(u.reshape(b, s, -1), conv_w_dw[0], conv_b_dw[0])
    h = _ln_out(y.reshape(n, -1), conv_ln_g[0], conv_ln_b[0], conv_w_out[0], conv_b_out[0], xf)
    h = _moe(h, norm_ffn_g[0], router_w_group[0], router_b_group[0], router_w_expert[0],
             router_b_expert[0], expert_w_gate, expert_w_up, expert_w_down, 0,
             norm_final_g, False)

    qkv0 = _qkv(h, b, norm_mix_g[1], norm_kv_g, attn_w_q[0], w_kv, dilated=False)
    qkvd = _qkv(h, b, norm_mix_g[1], norm_kv_g, attn_w_q[0], w_kv, dilated=True)
    g0 = _attn(qkv0, (0, 1, 2), s // BAND, BF16, "attn_g0")
    g1 = _attn_skew(qkvd, (0, 2, 4), F32, "attn_g1")
    g2 = _attn(qkvd, (1, 3, 5), 1, F32, "attn_g2")
    h = _merge_out(h, s, g0, g1, g2, attn_w_o[0])
    out = _moe(h, norm_ffn_g[1], router_w_group[1], router_b_group[1], router_w_expert[1],
               router_b_expert[1], expert_w_gate, expert_w_up, expert_w_down, 1,
               norm_final_g, True)
    return out.reshape(b, s, d)
```

```python
import functools

import jax
import jax.numpy as jnp
from jax import lax
from jax.experimental import pallas as pl
from jax.experimental.pallas import tpu as pltpu

F32 = jnp.float32
BF16 = jnp.bfloat16
I32 = jnp.int32

EPS = 1e-6
LANES = 128
V7X_VMEM_BYTES = 64 * 1024 * 1024
VMEM_LIMIT = V7X_VMEM_BYTES - 8 * 1024 * 1024

CONV_WIDTH = 31
HALO = 32
ATTN_DILATIONS = (1, 4, 16)
N_ATTN_GROUPS = 3
N_DILATED = 2
HEADS = 8
HEAD_DIM = 128
HEAD_DIM_LOG2 = 7
BAND = 128
BAND_LOG2 = 7
ATTN_OUT = HEADS * HEAD_DIM
N_GROUPS = 4
EXPERTS_PER_GROUP = 8
N_EXPERTS = N_GROUPS * EXPERTS_PER_GROUP

ROW_CHUNK = 32
ROW_UNROLL = 4
SUBLANES = 8
BULK_DMA_PRIORITY = 1


def _params(*sem):
    return pltpu.CompilerParams(dimension_semantics=sem, vmem_limit_bytes=VMEM_LIMIT)


def _row_loop(n_rows, body, chunk=ROW_CHUNK, unroll=ROW_UNROLL):
    def step(c, carry):
        body(pl.multiple_of(c * chunk, chunk))
        return carry
    lax.fori_loop(0, n_rows // chunk, step, 0, unroll=unroll)


def _rms(x, g):
    ms = jnp.mean(x * x, axis=-1, keepdims=True)
    return x * lax.rsqrt(ms + EPS) * g


def _bdot(a, b):
    return jnp.dot(a, b, preferred_element_type=F32)


def _prefetch_tile(i, n_tiles, fetch):
    buf = i % 2

    def start(copy):
        copy.start(priority=BULK_DMA_PRIORITY)

    @pl.when(i == 0)
    def _():
        fetch(0, 0, start)

    @pl.when(i + 1 < n_tiles)
    def _():
        fetch(i + 1, 1 - buf, start)

    fetch(i, buf, lambda copy: copy.wait())


def _glu_in_kernel(x_hbm, g_ref, wv_ref, wg_ref, bv_ref, bg_ref, o_ref, x_ref, xn_ref, sem):
    tm = x_ref.shape[1]
    i = pl.program_id(0)

    @pl.when(pl.program_id(1) == 0)
    def _():
        def fetch(tile, buf, fn):
            fn(pltpu.make_async_copy(x_hbm.at[pl.ds(tile * tm, tm), :], x_ref.at[buf], sem.at[buf]))
        _prefetch_tile(i, pl.num_programs(0), fetch)

        def norm(r0):
            rows = pl.ds(r0, ROW_CHUNK)
            xn_ref[rows, :] = _rms(x_ref[i % 2, rows, :], g_ref[...]).astype(BF16)
        _row_loop(tm, norm)

    xn = xn_ref[...]
    val = _bdot(xn, wv_ref[...].astype(BF16)) + bv_ref[...]
    gate = _bdot(xn, wg_ref[...].astype(BF16)) + bg_ref[...]
    o_ref[...] = (val * jax.nn.sigmoid(gate)).astype(o_ref.dtype)


def _glu_in(x, g, w_in, b_in, tm=1024, tn=512):
    n, d = x.shape
    c = w_in.shape[1] // 2
    nj = c // tn
    b2 = b_in.reshape(1, 2 * c)
    return pl.pallas_call(
        _glu_in_kernel,
        grid=(n // tm, nj),
        in_specs=[
            pl.BlockSpec(memory_space=pl.ANY),
            pl.BlockSpec((1, d), lambda i, j: (0, 0)),
            pl.BlockSpec((d, tn), lambda i, j: (0, j)),
            pl.BlockSpec((d, tn), lambda i, j: (0, j + nj)),
            pl.BlockSpec((1, tn), lambda i, j: (0, j)),
            pl.BlockSpec((1, tn), lambda i, j: (0, j + nj)),
        ],
        out_specs=pl.BlockSpec((tm, tn), lambda i, j: (i, j)),
        out_shape=jax.ShapeDtypeStruct((n, c), BF16),
        scratch_shapes=[pltpu.VMEM((2, tm, d), F32), pltpu.VMEM((tm, d), BF16), pltpu.SemaphoreType.DMA((2,))],
        compiler_params=_params("arbitrary", "arbitrary"),
        name="glu_in",
    )(x, g.reshape(1, d), w_in, w_in, b2, b2)


def _dwconv_kernel(cur_ref, halo_ref, w_ref, b_ref, o_ref, buf_ref):
    ts, cw = cur_ref.shape[1], cur_ref.shape[2]
    rw = 64
    keep = jnp.where(pl.program_id(1) > 0, 1.0, 0.0)
    buf_ref[0, 0:HALO, :] = halo_ref[0].astype(F32) * keep
    buf_ref[0, HALO:, :] = cur_ref[0].astype(F32)
    shifted_rows = ts + HALO - SUBLANES
    for s in range(1, SUBLANES):
        buf_ref[s, 0:shifted_rows, :] = buf_ref[0, s:s + shifted_rows, :]
    first = HALO - (CONV_WIDTH - 1)
    for c0 in range(0, cw, LANES):
        cols = slice(c0, c0 + LANES)
        for r0 in range(0, ts, rw):
            acc = jnp.broadcast_to(b_ref[:, cols], (rw, LANES))
            for k in range(CONV_WIDTH):
                s = (first + k) % SUBLANES
                start = r0 + first + k - s
                acc = acc + w_ref[k:k + 1, cols] * buf_ref[s, start:start + rw, cols]
            o_ref[0, r0:r0 + rw, cols] = acc.astype(o_ref.dtype)


def _dwconv(u, w_dw, b_dw, ts=256, cw=512):
    b, s, c = u.shape
    hb = ts // HALO
    return pl.pallas_call(
        _dwconv_kernel,
        grid=(b, s // ts, c // cw),
        in_specs=[
            pl.BlockSpec((1, ts, cw), lambda bi, si, ci: (bi, si, ci)),
            pl.BlockSpec((1, HALO, cw), lambda bi, si, ci: (bi, jnp.maximum(si * hb - 1, 0), ci)),
            pl.BlockSpec((CONV_WIDTH, cw), lambda bi, si, ci: (0, ci)),
            pl.BlockSpec((1, cw), lambda bi, si, ci: (0, ci)),
        ],
        out_specs=pl.BlockSpec((1, ts, cw), lambda bi, si, ci: (bi, si, ci)),
        out_shape=jax.ShapeDtypeStruct((b, s, c), BF16),
        scratch_shapes=[pltpu.VMEM((SUBLANES, ts + HALO, cw), F32)],
        compiler_params=_params("arbitrary", "arbitrary", "arbitrary"),
        name="dwconv",
    )(u, u, w_dw, b_dw.reshape(1, c))


def _ln_out_kernel(y_hbm, lg_ref, lb_ref, w_ref, b_ref, res_ref, o_ref, y_ref, a_ref, sem):
    tm = y_ref.shape[1]
    i = pl.program_id(0)

    @pl.when(pl.program_id(1) == 0)
    def _():
        def fetch(tile, buf, fn):
            fn(pltpu.make_async_copy(y_hbm.at[pl.ds(tile * tm, tm), :], y_ref.at[buf], sem.at[buf]))
        _prefetch_tile(i, pl.num_programs(0), fetch)

        def norm(r0):
            rows = pl.ds(r0, ROW_CHUNK)
            y = y_ref[i % 2, rows, :].astype(F32)
            mu = jnp.mean(y, axis=-1, keepdims=True)
            yc = y - mu
            var = jnp.mean(yc * yc, axis=-1, keepdims=True)
            z = yc * lax.rsqrt(var + EPS) * lg_ref[...] + lb_ref[...]
            a_ref[rows, :] = (z * jax.nn.sigmoid(z)).astype(BF16)
        _row_loop(tm, norm)

    o_ref[...] = res_ref[...] + _bdot(a_ref[...], w_ref[...].astype(BF16)) + b_ref[...]


def _ln_out(y, ln_g, ln_b, w_out, b_out, res, tm=1024, tn=512):
    n, c = y.shape
    d = w_out.shape[1]
    return pl.pallas_call(
        _ln_out_kernel,
        grid=(n // tm, d // tn),
        in_specs=[
            pl.BlockSpec(memory_space=pl.ANY),
            pl.BlockSpec((1, c), lambda i, j: (0, 0)),
            pl.BlockSpec((1, c), lambda i, j: (0, 0)),
            pl.BlockSpec((c, tn), lambda i, j: (0, j)),
            pl.BlockSpec((1, tn), lambda i, j: (0, j)),
            pl.BlockSpec((tm, tn), lambda i, j: (i, j)),
        ],
        out_specs=pl.BlockSpec((tm, tn), lambda i, j: (i, j)),
        out_shape=jax.ShapeDtypeStruct((n, d), F32),
        scratch_shapes=[pltpu.VMEM((2, tm, c), BF16), pltpu.VMEM((tm, c), BF16), pltpu.SemaphoreType.DMA((2,))],
        compiler_params=_params("arbitrary", "arbitrary"),
        name="ln_out",
    )(y, ln_g.reshape(1, c), ln_b.reshape(1, c), w_out, b_out.reshape(1, d), res)


def _split_bf16(a):
    hi = a.astype(BF16)
    lo = (a - hi.astype(F32)).astype(BF16)
    return hi, lo


def _router_kernel(h_ref, g_ref, w_ref, b_ref, ids_ref, wts_ref):
    t = _rms(h_ref[...], g_ref[...])
    t_hi, t_lo = _split_bf16(t)
    w_hi, w_lo = _split_bf16(w_ref[...])
    both = _bdot(t_hi, jnp.concatenate([w_hi, w_lo], axis=1))
    logits = both[:, :LANES] + both[:, LANES:] + _bdot(t_lo, w_hi) + b_ref[...]

    lane = lax.broadcasted_iota(I32, logits.shape, 1)
    lanef = lane.astype(F32)
    neg = -jnp.inf
    is_group = lane < N_GROUPS
    glog = jnp.where(is_group, logits, neg)
    gmax = jnp.max(glog, axis=1, keepdims=True)
    gi = jnp.min(jnp.where(glog == gmax, lanef, float(LANES)), axis=1, keepdims=True)
    gsum = jnp.sum(jnp.where(is_group, jnp.exp(logits - gmax), 0.0), axis=1, keepdims=True)
    gp = 1.0 / gsum

    lo_lane = N_GROUPS + gi * EXPERTS_PER_GROUP
    in_sel = (lanef >= lo_lane) & (lanef < lo_lane + EXPERTS_PER_GROUP)
    sel = jnp.where(in_sel, logits, neg)
    m1 = jnp.max(sel, axis=1, keepdims=True)
    i1 = jnp.min(jnp.where(sel == m1, lanef, float(LANES)), axis=1, keepdims=True)
    sel2 = jnp.where(lanef == i1, neg, sel)
    m2 = jnp.max(sel2, axis=1, keepdims=True)
    i2 = jnp.min(jnp.where(sel2 == m2, lanef, float(LANES)), axis=1, keepdims=True)
    e2 = jnp.exp(m2 - m1)
    w1 = gp / (1.0 + e2)
    w2 = gp * e2 / (1.0 + e2)

    ids = jnp.where(lane == 0, i1 - N_GROUPS, jnp.where(lane == 1, i2 - N_GROUPS, 0.0))
    ids_ref[...] = ids.astype(I32)
    wts_ref[...] = jnp.where(lane == 0, w1, jnp.where(lane == 1, w2, 0.0))


def _router(h, g, w_group, b_group, w_expert, b_expert, tm=512):
    n, d = h.shape
    w_e = jnp.transpose(w_expert, (1, 0, 2)).reshape(d, N_EXPERTS)
    pad = LANES - N_GROUPS - N_EXPERTS
    wr = jnp.concatenate([w_group, w_e, jnp.zeros((d, pad), F32)], axis=1)
    br = jnp.concatenate([b_group, b_expert.reshape(N_EXPERTS), jnp.zeros((pad,), F32)]).reshape(1, LANES)
    return pl.pallas_call(
        _router_kernel,
        grid=(n // tm,),
        in_specs=[
            pl.BlockSpec((tm, d), lambda i: (i, 0)),
            pl.BlockSpec((1, d), lambda i: (0, 0)),
            pl.BlockSpec((d, LANES), lambda i: (0, 0)),
            pl.BlockSpec((1, LANES), lambda i: (0, 0)),
        ],
        out_specs=[pl.BlockSpec((tm, LANES), lambda i: (i, 0)),
                   pl.BlockSpec((tm, LANES), lambda i: (i, 0))],
        out_shape=[jax.ShapeDtypeStruct((n, LANES), I32), jax.ShapeDtypeStruct((n, LANES), F32)],
        compiler_params=_params("arbitrary"),
        name="router",
    )(h, g.reshape(1, d), wr, br)


def _dispatch_plan(ids, tm, n_items):
    none = N_EXPERTS
    experts = jnp.arange(N_EXPERTS, dtype=I32)
    upto = experts[:, None] <= experts[None, :]

    def prefix(v):
        return jnp.sum(jnp.where(upto, v[:, None], 0), axis=0)

    e_flat = jnp.concatenate([ids[:, 0], ids[:, 1]])
    onehot = (e_flat[:, None] == experts[None, :]).astype(I32)
    csum = jnp.cumsum(onehot, axis=0)
    counts = csum[-1]
    ends = prefix(counts)
    starts = ends - counts
    dest = jnp.sum(onehot * (csum - 1 + starts[None, :]), axis=1)

    present = counts > 0
    first_tile = starts // tm
    items = jnp.where(present, (ends - 1) // tm - first_tile + 1, 0)
    item_end = prefix(items)
    item_start = item_end - items
    total = item_end[-1]

    ordinal = prefix(present.astype(I32)) - 1
    later = jnp.where(present[None, :] & (experts[None, :] > experts[:, None]), experts[None, :], none)
    nxt = jnp.min(later, axis=1)
    nxt2 = jnp.min(jnp.where(experts[None, :] == nxt[:, None], nxt[None, :], none), axis=1)
    ahead_e = jnp.where(nxt2 < none, nxt2, -1)
    head0 = jnp.min(jnp.where(present, experts, none))
    head1 = jnp.min(jnp.where(experts == head0, nxt, none))
    head = jnp.stack([head0, jnp.where(head1 < none, head1, -1)])

    w = jnp.arange(n_items, dtype=I32)
    valid = w < total
    wc = jnp.minimum(w, total - 1)
    e_w = jnp.sum((item_end[None, :] <= wc[:, None]).astype(I32), axis=1)
    table = jnp.stack([first_tile, item_start, starts, ends, ordinal, ahead_e], axis=1)
    mine = e_w[:, None] == experts[None, :]
    got = jnp.sum(jnp.where(mine[:, :, None], table[None, :, :], 0), axis=1)
    tile_w = got[:, 0] + (wc - got[:, 1])
    lo = jnp.where(valid, jnp.maximum(got[:, 2], tile_w * tm) - tile_w * tm, 0)
    hi = jnp.where(valid, jnp.minimum(got[:, 3], (tile_w + 1) * tm) - tile_w * tm, 0)
    prev_tile = jnp.concatenate([jnp.full((1,), -1, I32), tile_w[:-1]])
    prev_e = jnp.concatenate([jnp.full((1,), -1, I32), e_w[:-1]])
    first = valid & (tile_w != prev_tile)
    new_e = valid & (e_w != prev_e)
    plan = (tile_w, e_w, lo, hi, first, new_e, valid, got[:, 4] % 2, got[:, 5], head)
    return dest.astype(I32), tuple(p.astype(I32) for p in plan)


def _scatter_kernel(dest_ref, h_ref, g_ref, xs_ref, t_ref, sem):
    tm = h_ref.shape[0]
    i = pl.program_id(0)
    steps = pl.num_programs(0)
    n = steps * tm
    buf = i % 2
    groups = tm // SUBLANES

    def row_copy(step, b, grp, j, pick):
        d = dest_ref[pick * n + step * tm + grp * SUBLANES + j]
        return pltpu.make_async_copy(t_ref.at[b, grp, pl.ds(j, 1), :], xs_ref.at[pl.ds(d, 1), :], sem.at[b])

    def for_group(step, b, grp, fn):
        for j in range(SUBLANES):
            fn(row_copy(step, b, grp, j, 0), 0)
            fn(row_copy(step, b, grp, j, 1), 1)

    def for_rows(step, b, fn):
        def body(grp, carry):
            for_group(step, b, grp, fn)
            return carry
        lax.fori_loop(0, groups, body, 0)

    @pl.when(i >= 2)
    def _():
        for_rows(i - 2, buf, lambda copy, pick: copy.wait())

    per = ROW_CHUNK // SUBLANES
    chunks = groups // per

    def norm(c):
        rows = pl.ds(pl.multiple_of(c * ROW_CHUNK, ROW_CHUNK), ROW_CHUNK)
        t = _rms(h_ref[rows, :], g_ref[...])
        t_ref[buf, pl.ds(pl.multiple_of(c * per, per), per)] = t.reshape(per, SUBLANES, t.shape[-1])

    def start(c):
        for k in range(per):
            for_group(i, buf, c * per + k, lambda copy, pick: copy.start(priority=pick))

    norm(0)

    def body(c, carry):
        start(c)
        norm(c + 1)
        return carry
    lax.fori_loop(0, chunks - 1, body, 0)
    start(chunks - 1)

    @pl.when(i == steps - 1)
    def _():
        @pl.when(i >= 1)
        def _():
            for_rows(i - 1, 1 - buf, lambda copy, pick: copy.wait())
        for_rows(i, buf, lambda copy, pick: copy.wait())


def _scatter(h, g, dest, tm=256):
    n, d = h.shape
    return pl.pallas_call(
        _scatter_kernel,
        grid_spec=pltpu.PrefetchScalarGridSpec(
            num_scalar_prefetch=1,
            grid=(n // tm,),
            in_specs=[pl.BlockSpec((tm, d), lambda i, dest: (i, 0)),
                      pl.BlockSpec((1, d), lambda i, dest: (0, 0))],
            out_specs=pl.BlockSpec(memory_space=pl.ANY),
            scratch_shapes=[pltpu.VMEM((2, tm // SUBLANES, SUBLANES, d), F32),
                            pltpu.SemaphoreType.DMA((2,))],
        ),
        out_shape=jax.ShapeDtypeStruct((2 * n, d), F32),
        compiler_params=_params("arbitrary"),
        name="moe_scatter",
    )(dest, h, g.reshape(1, d))


def _experts_kernel(tile_ref, exp_ref, lo_ref, hi_ref, first_ref, new_ref, valid_ref, slot_ref,
                    ahead_ref, head_ref, xs_ref, wg_hbm, wu_hbm, wd_hbm, o_ref,
                    wg_st, wu_st, wd_st, wg_bf, wu_bf, wd_bf, sem, *, layer):
    w = pl.program_id(0)
    tm = xs_ref.shape[0]
    chunk = 256

    def fetch(e, s):
        return (pltpu.make_async_copy(wg_hbm.at[layer, e], wg_st.at[s], sem.at[s, 0]),
                pltpu.make_async_copy(wu_hbm.at[layer, e], wu_st.at[s], sem.at[s, 1]),
                pltpu.make_async_copy(wd_hbm.at[layer, e], wd_st.at[s], sem.at[s, 2]))

    @pl.when(w == 0)
    def _():
        for copy in fetch(head_ref[0], 0):
            copy.start(priority=BULK_DMA_PRIORITY)

        @pl.when(head_ref[1] >= 0)
        def _():
            for copy in fetch(head_ref[1], 1):
                copy.start(priority=BULK_DMA_PRIORITY)

    @pl.when(new_ref[w] == 1)
    def _():
        s = slot_ref[w]
        for copy in fetch(exp_ref[w], s):
            copy.wait()

        def cast_in(r0):
            rows = pl.ds(r0, chunk)
            wg_bf[rows, :] = wg_st[s, rows, :].astype(BF16)
            wu_bf[rows, :] = wu_st[s, rows, :].astype(BF16)
        _row_loop(wg_bf.shape[0], cast_in, chunk)

        def cast_down(r0):
            rows = pl.ds(r0, chunk)
            wd_bf[rows, :] = wd_st[s, rows, :].astype(BF16)
        _row_loop(wd_bf.shape[0], cast_down, chunk)

        @pl.when(ahead_ref[w] >= 0)
        def _():
            for copy in fetch(ahead_ref[w], s):
                copy.start(priority=BULK_DMA_PRIORITY)

    @pl.when(valid_ref[w] == 1)
    def _():
        row = lax.broadcasted_iota(I32, (tm, 1), 0)
        mine = (row >= lo_ref[w]) & (row < hi_ref[w])
        x = jnp.where(mine, xs_ref[...], 0.0).astype(BF16)
        hg = _bdot(x, wg_bf[...])
        hu = _bdot(x, wu_bf[...])
        act = (hg * jax.nn.sigmoid(hg) * hu).astype(BF16)
        y = _bdot(act, wd_bf[...])

        @pl.when(first_ref[w] == 1)
        def _():
            o_ref[...] = y

        @pl.when(first_ref[w] == 0)
        def _():
            o_ref[...] += y


def _experts(xs, plan, w_gate, w_up, w_down, layer, tm):
    p, d = xs.shape
    f = w_gate.shape[3]
    n_items = plan[0].shape[0]
    return pl.pallas_call(
        functools.partial(_experts_kernel, layer=layer),
        grid_spec=pltpu.PrefetchScalarGridSpec(
            num_scalar_prefetch=len(plan),
            grid=(n_items,),
            in_specs=[
                pl.BlockSpec((tm, d), lambda w, tile, *_: (tile[w], 0)),
                pl.BlockSpec(memory_space=pl.ANY),
                pl.BlockSpec(memory_space=pl.ANY),
                pl.BlockSpec(memory_space=pl.ANY),
            ],
            out_specs=pl.BlockSpec((tm, d), lambda w, tile, *_: (tile[w], 0)),
            scratch_shapes=[
                pltpu.VMEM((2, d, f), F32), pltpu.VMEM((2, d, f), F32), pltpu.VMEM((2, f, d), F32),
                pltpu.VMEM((d, f), BF16), pltpu.VMEM((d, f), BF16), pltpu.VMEM((f, d), BF16),
                pltpu.SemaphoreType.DMA((2, 3)),
            ],
        ),
        out_shape=jax.ShapeDtypeStruct((p, d), F32),
        compiler_params=_params("arbitrary"),
        name="moe_experts",
    )(*plan, xs, w_gate, w_up, w_down)


def _combine_kernel(dest_ref, h_ref, wts_ref, g_ref, ys_ref, o_ref, y_ref, sem, *, final_norm):
    tm = h_ref.shape[0]
    i = pl.program_id(0)
    steps = pl.num_programs(0)
    n = steps * tm
    buf = i % 2
    groups = tm // SUBLANES

    def row_copy(step, b, grp, j, pick):
        d = dest_ref[pick * n + step * tm + grp * SUBLANES + j]
        return pltpu.make_async_copy(ys_ref.at[pl.ds(d, 1), :], y_ref.at[b, pick, grp, pl.ds(j, 1), :],
                                     sem.at[b])

    def for_group(step, b, grp, fn):
        for j in range(SUBLANES):
            fn(row_copy(step, b, grp, j, 0), 0)
            fn(row_copy(step, b, grp, j, 1), 1)

    def start(copy, pick):
        copy.start(priority=pick)

    def for_rows(step, b, fn):
        def body(grp, carry):
            for_group(step, b, grp, fn)
            return carry
        lax.fori_loop(0, groups, body, 0)

    per = ROW_CHUNK // SUBLANES
    chunks = groups // per

    def mix(c):
        rows = pl.ds(pl.multiple_of(c * ROW_CHUNK, ROW_CHUNK), ROW_CHUNK)
        grps = pl.ds(pl.multiple_of(c * per, per), per)
        wts = wts_ref[rows, :]
        y0 = y_ref[buf, 0, grps].reshape(ROW_CHUNK, -1)
        y1 = y_ref[buf, 1, grps].reshape(ROW_CHUNK, -1)
        out = h_ref[rows, :] + wts[:, 0:1] * y0 + wts[:, 1:2] * y1
        if final_norm:
            out = _rms(out, g_ref[...])
        o_ref[rows, :] = out

    @pl.when(i == 0)
    def _():
        for_rows(0, 0, start)

    for_rows(i, buf, lambda copy, pick: copy.wait())

    @pl.when(i + 1 < steps)
    def _():
        def body(c, carry):
            for k in range(per):
                for_group(i + 1, 1 - buf, c * per + k, start)
            mix(c)
            return carry
        lax.fori_loop(0, chunks, body, 0)

    @pl.when(i + 1 == steps)
    def _():
        def body(c, carry):
            mix(c)
            return carry
        lax.fori_loop(0, chunks, body, 0, unroll=ROW_UNROLL)


def _combine(h, wts, ys, dest, g_final, final_norm, tm=256):
    n, d = h.shape
    return pl.pallas_call(
        functools.partial(_combine_kernel, final_norm=final_norm),
        grid_spec=pltpu.PrefetchScalarGridSpec(
            num_scalar_prefetch=1,
            grid=(n // tm,),
            in_specs=[pl.BlockSpec((tm, d), lambda i, dest: (i, 0)),
                      pl.BlockSpec((tm, LANES), lambda i, dest: (i, 0)),
                      pl.BlockSpec((1, d), lambda i, dest: (0, 0)),
                      pl.BlockSpec(memory_space=pl.ANY)],
            out_specs=pl.BlockSpec((tm, d), lambda i, dest: (i, 0)),
            scratch_shapes=[pltpu.VMEM((2, 2, tm // SUBLANES, SUBLANES, d), F32),
                            pltpu.SemaphoreType.DMA((2,))],
        ),
        out_shape=jax.ShapeDtypeStruct((n, d), F32),
        compiler_params=_params("arbitrary"),
        name="moe_combine",
    )(dest, h, wts, g_final.reshape(1, d), ys)


def _moe(h, g, w_group, b_group, w_expert, b_expert, w_gate, w_up, w_down, layer, g_final, final_norm,
         tm=256):
    n = h.shape[0]
    ids, wts = _router(h, g, w_group, b_group, w_expert, b_expert)
    n_items = 2 * n // tm + N_EXPERTS
    dest, plan = _dispatch_plan(ids, tm, n_items)
    xs = _scatter(h, g, dest)
    ys = _experts(xs, plan, w_gate, w_up, w_down, layer, tm)
    return _combine(h, wts, ys, dest, g_final, final_norm)


def _dot_t(a, b):
    return lax.dot_general(a, b, (((1,), (1,)), ((), ())), preferred_element_type=F32)


def _lane_slab(columns):
    lane = lax.broadcasted_iota(I32, (columns[0].shape[0], LANES), 1)
    slab = jnp.zeros(lane.shape, F32)
    for h, col in enumerate(columns):
        slab = jnp.where(lane == h, col, slab)
    return slab


def _attn_kernel(q_ref, kc_ref, vc_ref, kp_ref, vp_ref, o_ref, lse_ref, *, seg_blocks):
    qblocks = q_ref.shape[0] // BAND
    step = pl.program_id(0)
    banded = seg_blocks > 1
    assert not banded or seg_blocks % qblocks == 0
    n_keys = 2 * BAND if banded else BAND
    qi = lax.broadcasted_iota(I32, (BAND, n_keys), 0)
    kj = lax.broadcasted_iota(I32, (BAND, n_keys), 1)
    mask = (kj >= qi) & (kj <= qi + BAND) if banded else kj <= qi
    scale = HEAD_DIM ** -0.5
    neg = -jnp.inf
    heads = [slice(h * HEAD_DIM, (h + 1) * HEAD_DIM) for h in range(HEADS)]

    for sb in range(qblocks):
        rows = slice(sb * BAND, (sb + 1) * BAND)
        if not banded:
            scores = [_dot_t(q_ref[rows, c], kc_ref[rows, c]) for c in heads]
        elif sb > 0:
            krows = slice((sb - 1) * BAND, (sb + 1) * BAND)
            scores = [_dot_t(q_ref[rows, c], kc_ref[krows, c]) for c in heads]
        else:
            scores = [jnp.concatenate([_dot_t(q_ref[rows, c], kp_ref[:, c]),
                                       _dot_t(q_ref[rows, c], kc_ref[rows, c])], axis=1) for c in heads]
        scores = [jnp.where(mask, s * scale, neg) for s in scores]
        if banded and sb == 0:
            at_start = (step * qblocks) % seg_blocks == 0
            drop = jnp.where(kj < BAND, jnp.where(at_start, neg, 0.0), 0.0)
            scores = [s + drop for s in scores]
        ms = [jnp.max(s, axis=1, keepdims=True) for s in scores]
        ps = [jnp.exp(s - m) for s, m in zip(scores, ms)]
        ls = [jnp.sum(p, axis=1, keepdims=True) for p in ps]
        pb = [p.astype(BF16) for p in ps]
        if not banded:
            accs = [_bdot(p, vc_ref[rows, c]) for p, c in zip(pb, heads)]
        elif sb > 0:
            accs = [_bdot(p, vc_ref[krows, c]) for p, c in zip(pb, heads)]
        else:
            accs = [_bdot(p[:, :BAND], vp_ref[:, c]) + _bdot(p[:, BAND:], vc_ref[rows, c])
                    for p, c in zip(pb, heads)]
        for acc, l, c in zip(accs, ls, heads):
            o_ref[rows, c] = (acc / l).astype(o_ref.dtype)
        lse_ref[rows, :] = _lane_slab([m + jnp.log(l) for m, l in zip(ms, ls)])


def _attn(qkv, cols, seg_blocks, out_dtype, name, rows_per_step=512):
    n = qkv.shape[0]
    cq, ck, cv = cols
    qblocks = rows_per_step // BAND
    return pl.pallas_call(
        functools.partial(_attn_kernel, seg_blocks=seg_blocks),
        grid=(n // rows_per_step,),
        in_specs=[
            pl.BlockSpec((rows_per_step, ATTN_OUT), lambda i: (i, cq)),
            pl.BlockSpec((rows_per_step, ATTN_OUT), lambda i: (i, ck)),
            pl.BlockSpec((rows_per_step, ATTN_OUT), lambda i: (i, cv)),
            pl.BlockSpec((BAND, ATTN_OUT), lambda i: (jnp.maximum(i * qblocks - 1, 0), ck)),
            pl.BlockSpec((BAND, ATTN_OUT), lambda i: (jnp.maximum(i * qblocks - 1, 0), cv)),
        ],
        out_specs=[pl.BlockSpec((rows_per_step, ATTN_OUT), lambda i: (i, 0)),
                   pl.BlockSpec((rows_per_step, LANES), lambda i: (i, 0))],
        out_shape=[jax.ShapeDtypeStruct((n, ATTN_OUT), out_dtype), jax.ShapeDtypeStruct((n, LANES), F32)],
        compiler_params=_params("arbitrary"),
        name=name,
    )(qkv, qkv, qkv, qkv, qkv)


def _attn_skew_kernel(q_ref, k_ref, v_ref, o_ref, lse_ref):
    subs = q_ref.shape[0] // BAND
    keys = k_ref.shape[0]
    u = lax.broadcasted_iota(I32, (BAND, keys), 0)
    kc = lax.broadcasted_iota(I32, (BAND, keys), 1)
    base = subs * (u - (kc & (BAND - 1))) - lax.shift_right_logical(kc, BAND_LOG2)
    scale = HEAD_DIM ** -0.5
    neg = -jnp.inf
    heads = [slice(h * HEAD_DIM, (h + 1) * HEAD_DIM) for h in range(HEADS)]
    for s in range(subs):
        rows = slice(s * BAND, (s + 1) * BAND)
        delta = base + s
        mask = (delta >= 0) & (delta <= BAND)
        scores = [jnp.where(mask, _dot_t(q_ref[rows, c], k_ref[:, c]) * scale, neg) for c in heads]
        ms = [jnp.max(sc, axis=1, keepdims=True) for sc in scores]
        ps = [jnp.exp(sc - m) for sc, m in zip(scores, ms)]
        ls = [jnp.sum(p, axis=1, keepdims=True) for p in ps]
        accs = [_bdot(p.astype(BF16), v_ref[:, c]) for p, c in zip(ps, heads)]
        for acc, l, c in zip(accs, ls, heads):
            o_ref[rows, c] = (acc / l).astype(o_ref.dtype)
        lse_ref[rows, :] = _lane_slab([m + jnp.log(l) for m, l in zip(ms, ls)])


def _attn_skew(qkv, cols, out_dtype, name):
    n = qkv.shape[0]
    cq, ck, cv = cols
    rows = (ATTN_DILATIONS[2] // ATTN_DILATIONS[1]) * BAND
    return pl.pallas_call(
        _attn_skew_kernel,
        grid=(n // rows,),
        in_specs=[pl.BlockSpec((rows, ATTN_OUT), lambda i: (i, cq)),
                  pl.BlockSpec((rows, ATTN_OUT), lambda i: (i, ck)),
                  pl.BlockSpec((rows, ATTN_OUT), lambda i: (i, cv))],
        out_specs=[pl.BlockSpec((rows, ATTN_OUT), lambda i: (i, 0)),
                   pl.BlockSpec((rows, LANES), lambda i: (i, 0))],
        out_shape=[jax.ShapeDtypeStruct((n, ATTN_OUT), out_dtype), jax.ShapeDtypeStruct((n, LANES), F32)],
        compiler_params=_params("arbitrary"),
        name=name,
    )(qkv, qkv, qkv)


def _slot_class(c):
    return (c % 4) * 4 + c // 4


def _qkv_kernel(h_hbm, gq_ref, gkv_ref, wq_ref, wkv_ref, o_ref, x_ref, xq_ref, xkv_ref, sem,
                *, tiles_per_batch, n_groups):
    i, j = pl.program_id(0), pl.program_id(1)
    tiles = pl.num_programs(0)
    tm = x_ref.shape[1]
    q_blocks = n_groups * ATTN_OUT // o_ref.shape[1]

    def fetch(tile, buf, fn):
        if n_groups == 1:
            fn(pltpu.make_async_copy(h_hbm.at[pl.ds(tile * tm, tm), :], x_ref.at[buf], sem.at[buf]))
            return
        per_class = h_hbm.shape[0] // (tiles // tiles_per_batch)
        slots = tm // per_class
        bi, ti = tile // tiles_per_batch, tile % tiles_per_batch
        for c in range(slots):
            cls = _slot_class(ti * slots + c)
            fn(pltpu.make_async_copy(h_hbm.at[pl.ds(bi * per_class, per_class), cls, :],
                                     x_ref.at[buf, pl.ds(c * per_class, per_class), :],
                                     sem.at[buf]))

    @pl.when(j == 0)
    def _():
        buf = i % 2
        _prefetch_tile(i, tiles, fetch)

        def norm(r0):
            x = x_ref[buf, pl.ds(r0, ROW_CHUNK), :]
            xhat = x * lax.rsqrt(jnp.mean(x * x, axis=-1, keepdims=True) + EPS)
            rows = pl.ds(r0, ROW_CHUNK)
            xq_ref[rows, :] = (xhat * gq_ref[...]).astype(BF16)
            xkv_ref[rows, :] = (xhat * gkv_ref[...]).astype(BF16)
        _row_loop(tm, norm)

    @pl.when(j < q_blocks)
    def _():
        o_ref[...] = _bdot(xq_ref[...], wq_ref[...].astype(BF16)).astype(o_ref.dtype)

    @pl.when(j >= q_blocks)
    def _():
        o_ref[...] = _bdot(xkv_ref[...], wkv_ref[...].astype(BF16)).astype(o_ref.dtype)


def _qkv(h, batch, g_q, g_kv, w_q, w_kv, dilated, tm=1024, tn=512):
    n, d = h.shape
    first, n_groups = (1, N_DILATED) if dilated else (0, 1)
    r2 = ATTN_DILATIONS[2]
    hv = h.reshape(n // r2, r2, d) if dilated else h
    per = ATTN_OUT // tn
    q_blocks = n_groups * per
    kv_blocks = 2 * n_groups * per

    def wq_map(i, j):
        return (0, first * per + jnp.minimum(j, q_blocks - 1))

    def wkv_map(i, j):
        jj = jnp.clip(j - q_blocks, 0, kv_blocks - 1)
        return (0, first * per + jj + jnp.where(jj >= q_blocks, (N_ATTN_GROUPS - n_groups) * per, 0))

    return pl.pallas_call(
        functools.partial(_qkv_kernel, tiles_per_batch=(n // batch) // tm, n_groups=n_groups),
        grid=(n // tm, q_blocks + kv_blocks),
        in_specs=[
            pl.BlockSpec(memory_space=pl.ANY),
            pl.BlockSpec((1, d), lambda i, j: (0, 0)),
            pl.BlockSpec((1, d), lambda i, j: (0, 0)),
            pl.BlockSpec((d, tn), wq_map),
            pl.BlockSpec((d, tn), wkv_map),
        ],
        out_specs=pl.BlockSpec((tm, tn), lambda i, j: (i, j)),
        out_shape=jax.ShapeDtypeStruct((n, 3 * n_groups * ATTN_OUT), BF16),
        scratch_shapes=[pltpu.VMEM((2, tm, d), F32), pltpu.VMEM((tm, d), BF16), pltpu.VMEM((tm, d), BF16),
                        pltpu.SemaphoreType.DMA((2,))],
        compiler_params=_params("arbitrary", "arbitrary"),
        name="qkv_dilated" if dilated else "qkv_g0",
    )(hv, g_q.reshape(1, d), g_kv.reshape(1, d), w_q, w_kv)


def _merge_out_kernel(o0_ref, l0_ref, h_ref, w_ref, o1_hbm, l1_hbm, o2_hbm, l2_hbm, out_ref,
                      ob_ref, lb_ref, m_ref, sem, *, tiles_per_batch):
    t = pl.program_id(0)
    tiles = pl.num_programs(0)
    buf = t % 2
    per = ob_ref.shape[2]
    n_cls = ob_ref.shape[3]
    rows = per * n_cls

    def fetch(tile, b, fn):
        bi, ti = tile // tiles_per_batch, tile % tiles_per_batch
        for c in range(n_cls):
            cls = _slot_class(c)
            src_rows = pl.ds(ti * per, per)
            for g, (o_hbm, l_hbm) in enumerate(((o1_hbm, l1_hbm), (o2_hbm, l2_hbm))):
                fn(pltpu.make_async_copy(o_hbm.at[bi * n_cls + c, src_rows, :],
                                         ob_ref.at[b, g, :, cls, :], sem.at[b, 0]))
                fn(pltpu.make_async_copy(l_hbm.at[bi * n_cls + c, src_rows, :],
                                         lb_ref.at[b, g, :, cls, :], sem.at[b, 1]))

    _prefetch_tile(t, tiles, fetch)

    head_of_col = lax.shift_right_logical(lax.broadcasted_iota(I32, (LANES, ATTN_OUT), 1), HEAD_DIM_LOG2)
    spread = jnp.where(head_of_col == lax.broadcasted_iota(I32, (LANES, ATTN_OUT), 0), 1.0, 0.0).astype(BF16)

    chunk = BAND
    for r0 in range(0, rows, chunk):
        rs = slice(r0, r0 + chunk)
        gs = slice(r0 // n_cls, (r0 + chunk) // n_cls)
        lses = (l0_ref[rs, :],
                lb_ref[buf, 0, gs, :, :].reshape(chunk, LANES),
                lb_ref[buf, 1, gs, :, :].reshape(chunk, LANES))
        o1 = ob_ref[buf, 0, gs, :, :].reshape(chunk, ATTN_OUT)
        o2 = ob_ref[buf, 1, gs, :, :].reshape(chunk, ATTN_OUT)
        mx = jnp.maximum(jnp.maximum(lses[0], lses[1]), lses[2])
        ex = [jnp.exp(l - mx) for l in lses]
        inv = 1.0 / (ex[0] + ex[1] + ex[2])
        wide = []
        for e in ex:
            hi, lo = _split_bf16(e * inv)
            wide.append(_bdot(hi, spread) + _bdot(lo, spread))
        merged = wide[0] * o0_ref[rs, :].astype(F32) + wide[1] * o1 + wide[2] * o2
        m_ref[rs, :] = merged.astype(BF16)
    out_ref[...] = h_ref[...] + _bdot(m_ref[...], w_ref[...])


def _merge_out(h, seq, g0, g1, g2, w_o, rows=512):
    n, d = h.shape
    n_cls = ATTN_DILATIONS[2]
    per = rows // n_cls
    cls_len = seq // n_cls
    (o0, l0), (o1, l1), (o2, l2) = g0, g1, g2

    def by_class(a):
        return a.reshape(n // cls_len, cls_len, a.shape[-1])

    return pl.pallas_call(
        functools.partial(_merge_out_kernel, tiles_per_batch=seq // rows),
        grid=(n // rows,),
        in_specs=[
            pl.BlockSpec((rows, ATTN_OUT), lambda t: (t, 0)),
            pl.BlockSpec((rows, LANES), lambda t: (t, 0)),
            pl.BlockSpec((rows, d), lambda t: (t, 0)),
            pl.BlockSpec((ATTN_OUT, d), lambda t: (0, 0)),
            pl.BlockSpec(memory_space=pl.ANY),
            pl.BlockSpec(memory_space=pl.ANY),
            pl.BlockSpec(memory_space=pl.ANY),
            pl.BlockSpec(memory_space=pl.ANY),
        ],
        out_specs=pl.BlockSpec((rows, d), lambda t: (t, 0)),
        out_shape=jax.ShapeDtypeStruct((n, d), F32),
        scratch_shapes=[pltpu.VMEM((2, N_DILATED, per, n_cls, ATTN_OUT), F32),
                        pltpu.VMEM((2, N_DILATED, per, n_cls, LANES), F32),
                        pltpu.VMEM((rows, ATTN_OUT), BF16),
                        pltpu.SemaphoreType.DMA((2, 2))],
        compiler_params=_params("arbitrary"),
        name="attn_merge_out",
    )(o0, l0, h, w_o.astype(BF16), by_class(o1), by_class(l1), by_class(o2), by_class(l2))


def kernel(x, norm_mix_g, norm_ffn_g, conv_w_in, conv_b_in, conv_w_dw, conv_b_dw, conv_ln_g, conv_ln_b,
           conv_w_out, conv_b_out, norm_kv_g, w_kv, attn_w_q, attn_w_o, router_w_group, router_b_group,
           router_w_expert, router_b_expert, expert_w_gate, expert_w_up, expert_w_down, norm_final_g):
    b, s, d = x.shape
    n = b * s
    xf = x.reshape(n, d)

    u = _glu_in(xf, norm_mix_g[0], conv_w_in[0], conv_b_in[0])
    y = _dwconv(u.reshape(b, s, -1), conv_w_dw[0], conv_b_dw[0])
    h = _ln_out(y.reshape(n, -1), conv_ln_g[0], conv_ln_b[0], conv_w_out[0], conv_b_out[0], xf)
    h = _moe(h, norm_ffn_g[0], router_w_group[0], router_b_group[0], router_w_expert[0],
             router_b_expert[0], expert_w_gate, expert_w_up, expert_w_down, 0,
             norm_final_g, False)

    qkv0 = _qkv(h, b, norm_mix_g[1], norm_kv_g, attn_w_q[0], w_kv, dilated=False)
    qkvd = _qkv(h, b, norm_mix_g[1], norm_kv_g, attn_w_q[0], w_kv, dilated=True)
    g0 = _attn(qkv0, (0, 1, 2), s // BAND, BF16, "attn_g0")
    g1 = _attn_skew(qkvd, (0, 2, 4), F32, "attn_g1")
    g2 = _attn(qkvd, (1, 3, 5), 1, F32, "attn_g2")
    h = _merge_out(h, s, g0, g1, g2, attn_w_o[0])
    out = _moe(h, norm_ffn_g[1], router_w_group[1], router_b_group[1], router_w_expert[1],
               router_b_expert[1], expert_w_gate, expert_w_up, expert_w_down, 1,
               norm_final_g, True)
    return out.reshape(b, s, d)
```

```python
import functools

import jax
import jax.numpy as jnp
from jax import lax
from jax.experimental import pallas as pl
from jax.experimental.pallas import tpu as pltpu

F32 = jnp.float32
BF16 = jnp.bfloat16
I32 = jnp.int32

EPS = 1e-6
LANES = 128
V7X_VMEM_BYTES = 64 * 1024 * 1024
VMEM_LIMIT = V7X_VMEM_BYTES - 8 * 1024 * 1024

CONV_WIDTH = 31
HALO = 32
ATTN_DILATIONS = (1, 4, 16)
N_ATTN_GROUPS = 3
N_DILATED = 2
HEADS = 8
HEAD_DIM = 128
HEAD_DIM_LOG2 = 7
BAND = 128
BAND_LOG2 = 7
ATTN_OUT = HEADS * HEAD_DIM
N_GROUPS = 4
EXPERTS_PER_GROUP = 8
N_EXPERTS = N_GROUPS * EXPERTS_PER_GROUP

ROW_CHUNK = 32
ROW_UNROLL = 4
SUBLANES = 8
BULK_DMA_PRIORITY = 1


def _params(*sem):
    return pltpu.CompilerParams(dimension_semantics=sem, vmem_limit_bytes=VMEM_LIMIT)


def _row_loop(n_rows, body, chunk=ROW_CHUNK, unroll=ROW_UNROLL):
    def step(c, carry):
        body(pl.multiple_of(c * chunk, chunk))
        return carry
    lax.fori_loop(0, n_rows // chunk, step, 0, unroll=unroll)


def _rms(x, g):
    ms = jnp.mean(x * x, axis=-1, keepdims=True)
    return x * lax.rsqrt(ms + EPS) * g


def _bdot(a, b):
    return jnp.dot(a, b, preferred_element_type=F32)


def _prefetch_tile(i, n_tiles, fetch, priority=BULK_DMA_PRIORITY):
    buf = i % 2

    def start(copy):
        copy.start(priority=priority)

    @pl.when(i == 0)
    def _():
        fetch(0, 0, start)

    @pl.when(i + 1 < n_tiles)
    def _():
        fetch(i + 1, 1 - buf, start)

    fetch(i, buf, lambda copy: copy.wait())


def _glu_in_kernel(x_hbm, g_ref, wv_ref, wg_ref, bv_ref, bg_ref, o_ref, x_ref, xn_ref, sem):
    tm = x_ref.shape[1]
    i = pl.program_id(0)

    @pl.when(pl.program_id(1) == 0)
    def _():
        def fetch(tile, buf, fn):
            fn(pltpu.make_async_copy(x_hbm.at[pl.ds(tile * tm, tm), :], x_ref.at[buf], sem.at[buf]))
        _prefetch_tile(i, pl.num_programs(0), fetch)

        def norm(r0):
            rows = pl.ds(r0, ROW_CHUNK)
            xn_ref[rows, :] = _rms(x_ref[i % 2, rows, :], g_ref[...]).astype(BF16)
        _row_loop(tm, norm)

    xn = xn_ref[...]
    val = _bdot(xn, wv_ref[...].astype(BF16)) + bv_ref[...]
    gate = _bdot(xn, wg_ref[...].astype(BF16)) + bg_ref[...]
    o_ref[...] = (val * jax.nn.sigmoid(gate)).astype(o_ref.dtype)


def _glu_in(x, g, w_in, b_in, tm=1024, tn=512):
    n, d = x.shape
    c = w_in.shape[1] // 2
    nj = c // tn
    b2 = b_in.reshape(1, 2 * c)
    return pl.pallas_call(
        _glu_in_kernel,
        grid=(n // tm, nj),
        in_specs=[
            pl.BlockSpec(memory_space=pl.ANY),
            pl.BlockSpec((1, d), lambda i, j: (0, 0)),
            pl.BlockSpec((d, tn), lambda i, j: (0, j)),
            pl.BlockSpec((d, tn), lambda i, j: (0, j + nj)),
            pl.BlockSpec((1, tn), lambda i, j: (0, j)),
            pl.BlockSpec((1, tn), lambda i, j: (0, j + nj)),
        ],
        out_specs=pl.BlockSpec((tm, tn), lambda i, j: (i, j)),
        out_shape=jax.ShapeDtypeStruct((n, c), BF16),
        scratch_shapes=[pltpu.VMEM((2, tm, d), F32), pltpu.VMEM((tm, d), BF16), pltpu.SemaphoreType.DMA((2,))],
        compiler_params=_params("arbitrary", "arbitrary"),
        name="glu_in",
    )(x, g.reshape(1, d), w_in, w_in, b2, b2)


def _dwconv_kernel(cur_ref, halo_ref, w_ref, b_ref, o_ref, buf_ref):
    ts, cw = cur_ref.shape[1], cur_ref.shape[2]
    rw = 64
    keep = jnp.where(pl.program_id(1) > 0, 1.0, 0.0)
    buf_ref[0, 0:HALO, :] = halo_ref[0].astype(F32) * keep
    buf_ref[0, HALO:, :] = cur_ref[0].astype(F32)
    shifted_rows = ts + HALO - SUBLANES
    for s in range(1, SUBLANES):
        buf_ref[s, 0:shifted_rows, :] = buf_ref[0, s:s + shifted_rows, :]
    first = HALO - (CONV_WIDTH - 1)
    for c0 in range(0, cw, LANES):
        cols = slice(c0, c0 + LANES)
        for r0 in range(0, ts, rw):
            acc = jnp.broadcast_to(b_ref[:, cols], (rw, LANES))
            for k in range(CONV_WIDTH):
                s = (first + k) % SUBLANES
                start = r0 + first + k - s
                acc = acc + w_ref[k:k + 1, cols] * buf_ref[s, start:start + rw, cols]
            o_ref[0, r0:r0 + rw, cols] = acc.astype(o_ref.dtype)


def _dwconv(u, w_dw, b_dw, ts=256, cw=512):
    b, s, c = u.shape
    hb = ts // HALO
    return pl.pallas_call(
        _dwconv_kernel,
        grid=(b, s // ts, c // cw),
        in_specs=[
            pl.BlockSpec((1, ts, cw), lambda bi, si, ci: (bi, si, ci)),
            pl.BlockSpec((1, HALO, cw), lambda bi, si, ci: (bi, jnp.maximum(si * hb - 1, 0), ci)),
            pl.BlockSpec((CONV_WIDTH, cw), lambda bi, si, ci: (0, ci)),
            pl.BlockSpec((1, cw), lambda bi, si, ci: (0, ci)),
        ],
        out_specs=pl.BlockSpec((1, ts, cw), lambda bi, si, ci: (bi, si, ci)),
        out_shape=jax.ShapeDtypeStruct((b, s, c), BF16),
        scratch_shapes=[pltpu.VMEM((SUBLANES, ts + HALO, cw), F32)],
        compiler_params=_params("arbitrary", "arbitrary", "arbitrary"),
        name="dwconv",
    )(u, u, w_dw, b_dw.reshape(1, c))


def _ln_out_kernel(y_hbm, lg_ref, lb_ref, w_ref, b_ref, res_ref, o_ref, y_ref, a_ref, sem):
    tm = y_ref.shape[1]
    i = pl.program_id(0)

    @pl.when(pl.program_id(1) == 0)
    def _():
        def fetch(tile, buf, fn):
            fn(pltpu.make_async_copy(y_hbm.at[pl.ds(tile * tm, tm), :], y_ref.at[buf], sem.at[buf]))
        _prefetch_tile(i, pl.num_programs(0), fetch)

        def norm(r0):
            rows = pl.ds(r0, ROW_CHUNK)
            y = y_ref[i % 2, rows, :].astype(F32)
            mu = jnp.mean(y, axis=-1, keepdims=True)
            yc = y - mu
            var = jnp.mean(yc * yc, axis=-1, keepdims=True)
            z = yc * lax.rsqrt(var + EPS) * lg_ref[...] + lb_ref[...]
            a_ref[rows, :] = (z * jax.nn.sigmoid(z)).astype(BF16)
        _row_loop(tm, norm)

    o_ref[...] = res_ref[...] + _bdot(a_ref[...], w_ref[...].astype(BF16)) + b_ref[...]


def _ln_out(y, ln_g, ln_b, w_out, b_out, res, tm=1024, tn=512):
    n, c = y.shape
    d = w_out.shape[1]
    return pl.pallas_call(
        _ln_out_kernel,
        grid=(n // tm, d // tn),
        in_specs=[
            pl.BlockSpec(memory_space=pl.ANY),
            pl.BlockSpec((1, c), lambda i, j: (0, 0)),
            pl.BlockSpec((1, c), lambda i, j: (0, 0)),
            pl.BlockSpec((c, tn), lambda i, j: (0, j)),
            pl.BlockSpec((1, tn), lambda i, j: (0, j)),
            pl.BlockSpec((tm, tn), lambda i, j: (i, j)),
        ],
        out_specs=pl.BlockSpec((tm, tn), lambda i, j: (i, j)),
        out_shape=jax.ShapeDtypeStruct((n, d), F32),
        scratch_shapes=[pltpu.VMEM((2, tm, c), BF16), pltpu.VMEM((tm, c), BF16), pltpu.SemaphoreType.DMA((2,))],
        compiler_params=_params("arbitrary", "arbitrary"),
        name="ln_out",
    )(y, ln_g.reshape(1, c), ln_b.reshape(1, c), w_out, b_out.reshape(1, d), res)


def _split_bf16(a):
    hi = a.astype(BF16)
    lo = (a - hi.astype(F32)).astype(BF16)
    return hi, lo


def _router_kernel(h_ref, g_ref, w_ref, b_ref, ids_ref, wts_ref):
    t = _rms(h_ref[...], g_ref[...])
    t_hi, t_lo = _split_bf16(t)
    w_hi, w_lo = _split_bf16(w_ref[...])
    both = _bdot(t_hi, jnp.concatenate([w_hi, w_lo], axis=1))
    logits = both[:, :LANES] + both[:, LANES:] + _bdot(t_lo, w_hi) + b_ref[...]

    lane = lax.broadcasted_iota(I32, logits.shape, 1)
    lanef = lane.astype(F32)
    neg = -jnp.inf
    is_group = lane < N_GROUPS
    glog = jnp.where(is_group, logits, neg)
    gmax = jnp.max(glog, axis=1, keepdims=True)
    gi = jnp.min(jnp.where(glog == gmax, lanef, float(LANES)), axis=1, keepdims=True)
    gsum = jnp.sum(jnp.where(is_group, jnp.exp(logits - gmax), 0.0), axis=1, keepdims=True)
    gp = 1.0 / gsum

    lo_lane = N_GROUPS + gi * EXPERTS_PER_GROUP
    in_sel = (lanef >= lo_lane) & (lanef < lo_lane + EXPERTS_PER_GROUP)
    sel = jnp.where(in_sel, logits, neg)
    m1 = jnp.max(sel, axis=1, keepdims=True)
    i1 = jnp.min(jnp.where(sel == m1, lanef, float(LANES)), axis=1, keepdims=True)
    sel2 = jnp.where(lanef == i1, neg, sel)
    m2 = jnp.max(sel2, axis=1, keepdims=True)
    i2 = jnp.min(jnp.where(sel2 == m2, lanef, float(LANES)), axis=1, keepdims=True)
    e2 = jnp.exp(m2 - m1)
    w1 = gp / (1.0 + e2)
    w2 = gp * e2 / (1.0 + e2)

    ids = jnp.where(lane == 0, i1 - N_GROUPS, jnp.where(lane == 1, i2 - N_GROUPS, 0.0))
    ids_ref[...] = ids.astype(I32)
    wts_ref[...] = jnp.where(lane == 0, w1, jnp.where(lane == 1, w2, 0.0))


def _router(h, g, w_group, b_group, w_expert, b_expert, tm=512):
    n, d = h.shape
    w_e = jnp.transpose(w_expert, (1, 0, 2)).reshape(d, N_EXPERTS)
    pad = LANES - N_GROUPS - N_EXPERTS
    wr = jnp.concatenate([w_group, w_e, jnp.zeros((d, pad), F32)], axis=1)
    br = jnp.concatenate([b_group, b_expert.reshape(N_EXPERTS), jnp.zeros((pad,), F32)]).reshape(1, LANES)
    return pl.pallas_call(
        _router_kernel,
        grid=(n // tm,),
        in_specs=[
            pl.BlockSpec((tm, d), lambda i: (i, 0)),
            pl.BlockSpec((1, d), lambda i: (0, 0)),
            pl.BlockSpec((d, LANES), lambda i: (0, 0)),
            pl.BlockSpec((1, LANES), lambda i: (0, 0)),
        ],
        out_specs=[pl.BlockSpec((tm, LANES), lambda i: (i, 0)),
                   pl.BlockSpec((tm, LANES), lambda i: (i, 0))],
        out_shape=[jax.ShapeDtypeStruct((n, LANES), I32), jax.ShapeDtypeStruct((n, LANES), F32)],
        compiler_params=_params("arbitrary"),
        name="router",
    )(h, g.reshape(1, d), wr, br)


def _sorted_rows(n_assign, tm):
    return -(-(n_assign + N_EXPERTS * SUBLANES) // tm) * tm


def _dispatch_plan(ids, tm, n_items):
    none = N_EXPERTS
    experts = jnp.arange(N_EXPERTS, dtype=I32)
    upto = experts[:, None] <= experts[None, :]

    def prefix(v):
        return jnp.sum(jnp.where(upto, v[:, None], 0), axis=0)

    e_flat = jnp.concatenate([ids[:, 0], ids[:, 1]])
    onehot = (e_flat[:, None] == experts[None, :]).astype(I32)
    csum = jnp.cumsum(onehot, axis=0)
    counts = csum[-1]
    present = counts > 0
    n_rows = _sorted_rows(e_flat.shape[0], tm)
    aligned = (counts + SUBLANES - 1) // SUBLANES * SUBLANES
    last = jnp.max(jnp.where(present, experts, 0))
    padded = aligned + jnp.where(experts == last, n_rows - jnp.sum(aligned), 0)
    seg_ends = prefix(padded)
    starts = seg_ends - padded
    ends = starts + counts
    dest = jnp.sum(onehot * (csum - 1 + starts[None, :]), axis=1)

    gaps = padded - counts
    gap_ends = prefix(gaps)
    j = jnp.arange(n_rows - e_flat.shape[0], dtype=I32)
    gap_e = jnp.sum((gap_ends[None, :] <= j[:, None]).astype(I32), axis=1)
    first_gap = ends - (gap_ends - gaps)
    fill = j + jnp.sum(jnp.where(gap_e[:, None] == experts[None, :], first_gap[None, :], 0), axis=1)

    items = (padded + tm - 1) // tm
    item_end = prefix(items)
    item_start = item_end - items
    total = item_end[-1]

    ordinal = prefix(present.astype(I32)) - 1
    later = jnp.where(present[None, :] & (experts[None, :] > experts[:, None]), experts[None, :], none)
    nxt = jnp.min(later, axis=1)
    nxt2 = jnp.min(jnp.where(experts[None, :] == nxt[:, None], nxt[None, :], none), axis=1)
    ahead_e = jnp.where(nxt2 < none, nxt2, -1)
    head0 = jnp.min(jnp.where(present, experts, none))
    head1 = jnp.min(jnp.where(experts == head0, nxt, none))
    head = jnp.stack([head0, jnp.where(head1 < none, head1, -1)])

    w = jnp.arange(n_items, dtype=I32)
    valid = w < total
    wc = jnp.minimum(w, total - 1)
    e_w = jnp.sum((item_end[None, :] <= wc[:, None]).astype(I32), axis=1)
    table = jnp.stack([item_start, starts, seg_ends, ordinal, ahead_e], axis=1)
    mine = e_w[:, None] == experts[None, :]
    got = jnp.sum(jnp.where(mine[:, :, None], table[None, :, :], 0), axis=1)
    want = got[:, 1] + (wc - got[:, 0]) * tm
    start = jnp.minimum(want, n_rows - tm)
    lo = want - start
    rows = jnp.where(valid, jnp.minimum(got[:, 2] - want, tm), 0)
    prev_e = jnp.concatenate([jnp.full((1,), -1, I32), e_w[:-1]])
    new_e = valid & (e_w != prev_e)
    plan = (start, e_w, lo, rows, new_e, valid, got[:, 3] % 2, got[:, 4], head)
    return dest.astype(I32), tuple(p.astype(I32) for p in plan), fill.astype(I32)


def _scatter_kernel(dest_ref, fill_ref, h_ref, g_ref, xs_ref, t_ref, sem):
    tm = h_ref.shape[0]
    i = pl.program_id(0)
    steps = pl.num_programs(0)
    n = steps * tm
    buf = i % 2
    groups = tm // SUBLANES

    def row_copy(step, b, grp, j, pick):
        d = dest_ref[pick * n + step * tm + grp * SUBLANES + j]
        return pltpu.make_async_copy(t_ref.at[b, grp, pl.ds(j, 1), :], xs_ref.at[pl.ds(d, 1), :], sem.at[b])

    def for_group(step, b, grp, fn):
        for j in range(SUBLANES):
            fn(row_copy(step, b, grp, j, 0), 0)
            fn(row_copy(step, b, grp, j, 1), 1)

    def for_rows(step, b, fn):
        def body(grp, carry):
            for_group(step, b, grp, fn)
            return carry
        lax.fori_loop(0, groups, body, 0)

    @pl.when(i >= 2)
    def _():
        for_rows(i - 2, buf, lambda copy, pick: copy.wait())

    per = ROW_CHUNK // SUBLANES
    chunks = groups // per

    def norm(c):
        rows = pl.ds(pl.multiple_of(c * ROW_CHUNK, ROW_CHUNK), ROW_CHUNK)
        t = _rms(h_ref[rows, :], g_ref[...])
        t_ref[buf, pl.ds(pl.multiple_of(c * per, per), per)] = t.reshape(per, SUBLANES, t.shape[-1])

    def start(c):
        for k in range(per):
            for_group(i, buf, c * per + k, lambda copy, pick: copy.start(priority=pick))

    norm(0)

    def body(c, carry):
        start(c)
        norm(c + 1)
        return carry
    lax.fori_loop(0, chunks - 1, body, 0)
    start(chunks - 1)

    @pl.when(i == steps - 1)
    def _():
        def fill_copy(k):
            return pltpu.make_async_copy(t_ref.at[buf, 0, pl.ds(0, 1), :],
                                         xs_ref.at[pl.ds(fill_ref[k], 1), :], sem.at[2])

        def fill_start(k, carry):
            fill_copy(k).start()
            return carry
        lax.fori_loop(0, fill_ref.shape[0], fill_start, 0)

        def fill_wait(k, carry):
            fill_copy(k).wait()
            return carry
        lax.fori_loop(0, fill_ref.shape[0], fill_wait, 0)

        @pl.when(i >= 1)
        def _():
            for_rows(i - 1, 1 - buf, lambda copy, pick: copy.wait())
        for_rows(i, buf, lambda copy, pick: copy.wait())


def _scatter(h, g, dest, fill, n_rows, tm=256):
    n, d = h.shape
    return pl.pallas_call(
        _scatter_kernel,
        grid_spec=pltpu.PrefetchScalarGridSpec(
            num_scalar_prefetch=2,
            grid=(n // tm,),
            in_specs=[pl.BlockSpec((tm, d), lambda i, dest, fill: (i, 0)),
                      pl.BlockSpec((1, d), lambda i, dest, fill: (0, 0))],
            out_specs=pl.BlockSpec(memory_space=pl.ANY),
            scratch_shapes=[pltpu.VMEM((2, tm // SUBLANES, SUBLANES, d), F32),
                            pltpu.SemaphoreType.DMA((3,))],
        ),
        out_shape=jax.ShapeDtypeStruct((n_rows, d), F32),
        compiler_params=_params("arbitrary"),
        name="moe_scatter",
    )(dest, fill, h, g.reshape(1, d))


def _experts_kernel(start_ref, exp_ref, lo_ref, rows_ref, new_ref, valid_ref, slot_ref, ahead_ref, head_ref,
                    xs_hbm, wg_hbm, wu_hbm, wd_hbm, ys_hbm, x_ref, y_ref,
                    wg_st, wu_st, wd_st, wg_bf, wu_bf, wd_bf, sem_w, sem_x, sem_y, *, layer):
    w = pl.program_id(0)
    n_items = pl.num_programs(0)
    tm = x_ref.shape[1]
    buf = w % 2
    chunk = 256

    def window(item, b, fn):
        rows = pl.ds(pl.multiple_of(start_ref[item], SUBLANES), tm)
        fn(pltpu.make_async_copy(xs_hbm.at[rows, :], x_ref.at[b], sem_x.at[b]))
    _prefetch_tile(w, n_items, window, priority=0)

    def fetch(e, s):
        return (pltpu.make_async_copy(wg_hbm.at[layer, e], wg_st.at[s], sem_w.at[s, 0]),
                pltpu.make_async_copy(wu_hbm.at[layer, e], wu_st.at[s], sem_w.at[s, 1]),
                pltpu.make_async_copy(wd_hbm.at[layer, e], wd_st.at[s], sem_w.at[s, 2]))

    @pl.when(w == 0)
    def _():
        for copy in fetch(head_ref[0], 0):
            copy.start(priority=BULK_DMA_PRIORITY)

        @pl.when(head_ref[1] >= 0)
        def _():
            for copy in fetch(head_ref[1], 1):
                copy.start(priority=BULK_DMA_PRIORITY)

    @pl.when(new_ref[w] == 1)
    def _():
        s = slot_ref[w]
        for copy in fetch(exp_ref[w], s):
            copy.wait()

        def cast_in(r0):
            rows = pl.ds(r0, chunk)
            wg_bf[rows, :] = wg_st[s, rows, :].astype(BF16)
            wu_bf[rows, :] = wu_st[s, rows, :].astype(BF16)
        _row_loop(wg_bf.shape[0], cast_in, chunk)

        def cast_down(r0):
            rows = pl.ds(r0, chunk)
            wd_bf[rows, :] = wd_st[s, rows, :].astype(BF16)
        _row_loop(wd_bf.shape[0], cast_down, chunk)

        @pl.when(ahead_ref[w] >= 0)
        def _():
            for copy in fetch(ahead_ref[w], s):
                copy.start(priority=BULK_DMA_PRIORITY)

    def put(item, b, fn):
        rows, lo, start = rows_ref[item], lo_ref[item], start_ref[item]
        size = tm
        while size >= SUBLANES:
            off = lo + (rows & ~(2 * size - 1))

            @pl.when((rows & size) != 0)
            def _(size=size, off=off):
                src = y_ref.at[b, pl.ds(pl.multiple_of(off, SUBLANES), size), :]
                dst = ys_hbm.at[pl.ds(pl.multiple_of(start + off, SUBLANES), size), :]
                fn(pltpu.make_async_copy(src, dst, sem_y.at[b]))
            size //= 2

    @pl.when(w >= 2)
    def _():
        put(w - 2, buf, lambda copy: copy.wait())

    @pl.when(valid_ref[w] == 1)
    def _():
        x = x_ref[buf].astype(BF16)
        hg = _bdot(x, wg_bf[...])
        hu = _bdot(x, wu_bf[...])
        act = (hg * jax.nn.sigmoid(hg) * hu).astype(BF16)
        y_ref[buf] = _bdot(act, wd_bf[...])
        put(w, buf, lambda copy: copy.start())

    @pl.when(w == n_items - 1)
    def _():
        @pl.when(w >= 1)
        def _():
            put(w - 1, 1 - buf, lambda copy: copy.wait())
        put(w, buf, lambda copy: copy.wait())


def _experts(xs, plan, w_gate, w_up, w_down, layer, n_rows, tm):
    d = xs.shape[1]
    f = w_gate.shape[3]
    n_items = plan[0].shape[0]
    any_space = pl.BlockSpec(memory_space=pl.ANY)
    return pl.pallas_call(
        functools.partial(_experts_kernel, layer=layer),
        grid_spec=pltpu.PrefetchScalarGridSpec(
            num_scalar_prefetch=len(plan),
            grid=(n_items,),
            in_specs=[any_space, any_space, any_space, any_space],
            out_specs=any_space,
            scratch_shapes=[
                pltpu.VMEM((2, tm, d), F32), pltpu.VMEM((2, tm, d), F32),
                pltpu.VMEM((2, d, f), F32), pltpu.VMEM((2, d, f), F32), pltpu.VMEM((2, f, d), F32),
                pltpu.VMEM((d, f), BF16), pltpu.VMEM((d, f), BF16), pltpu.VMEM((f, d), BF16),
                pltpu.SemaphoreType.DMA((2, 3)), pltpu.SemaphoreType.DMA((2,)), pltpu.SemaphoreType.DMA((2,)),
            ],
        ),
        out_shape=jax.ShapeDtypeStruct((n_rows, d), F32),
        compiler_params=_params("arbitrary"),
        name="moe_experts",
    )(*plan, xs, w_gate, w_up, w_down)


def _combine_kernel(dest_ref, h_ref, wts_ref, g_ref, ys_ref, o_ref, y_ref, sem, *, final_norm):
    tm = h_ref.shape[0]
    i = pl.program_id(0)
    steps = pl.num_programs(0)
    n = steps * tm
    buf = i % 2
    groups = tm // SUBLANES

    def row_copy(step, b, grp, j, pick):
        d = dest_ref[pick * n + step * tm + grp * SUBLANES + j]
        return pltpu.make_async_copy(ys_ref.at[pl.ds(d, 1), :], y_ref.at[b, pick, grp, pl.ds(j, 1), :],
                                     sem.at[b])

    def for_group(step, b, grp, fn):
        for j in range(SUBLANES):
            fn(row_copy(step, b, grp, j, 0), 0)
            fn(row_copy(step, b, grp, j, 1), 1)

    def start(copy, pick):
        copy.start(priority=pick)

    def for_rows(step, b, fn):
        def body(grp, carry):
            for_group(step, b, grp, fn)
            return carry
        lax.fori_loop(0, groups, body, 0)

    per = ROW_CHUNK // SUBLANES
    chunks = groups // per

    def mix(c):
        rows = pl.ds(pl.multiple_of(c * ROW_CHUNK, ROW_CHUNK), ROW_CHUNK)
        grps = pl.ds(pl.multiple_of(c * per, per), per)
        wts = wts_ref[rows, :]
        y0 = y_ref[buf, 0, grps].reshape(ROW_CHUNK, -1)
        y1 = y_ref[buf, 1, grps].reshape(ROW_CHUNK, -1)
        out = h_ref[rows, :] + wts[:, 0:1] * y0 + wts[:, 1:2] * y1
        if final_norm:
            out = _rms(out, g_ref[...])
        o_ref[rows, :] = out

    @pl.when(i == 0)
    def _():
        for_rows(0, 0, start)

    for_rows(i, buf, lambda copy, pick: copy.wait())

    @pl.when(i + 1 < steps)
    def _():
        def body(c, carry):
            for k in range(per):
                for_group(i + 1, 1 - buf, c * per + k, start)
            mix(c)
            return carry
        lax.fori_loop(0, chunks, body, 0)

    @pl.when(i + 1 == steps)
    def _():
        def body(c, carry):
            mix(c)
            return carry
        lax.fori_loop(0, chunks, body, 0, unroll=ROW_UNROLL)


def _combine(h, wts, ys, dest, g_final, final_norm, tm=256):
    n, d = h.shape
    return pl.pallas_call(
        functools.partial(_combine_kernel, final_norm=final_norm),
        grid_spec=pltpu.PrefetchScalarGridSpec(
            num_scalar_prefetch=1,
            grid=(n // tm,),
            in_specs=[pl.BlockSpec((tm, d), lambda i, dest: (i, 0)),
                      pl.BlockSpec((tm, LANES), lambda i, dest: (i, 0)),
                      pl.BlockSpec((1, d), lambda i, dest: (0, 0)),
                      pl.BlockSpec(memory_space=pl.ANY)],
            out_specs=pl.BlockSpec((tm, d), lambda i, dest: (i, 0)),
            scratch_shapes=[pltpu.VMEM((2, 2, tm // SUBLANES, SUBLANES, d), F32),
                            pltpu.SemaphoreType.DMA((2,))],
        ),
        out_shape=jax.ShapeDtypeStruct((n, d), F32),
        compiler_params=_params("arbitrary"),
        name="moe_combine",
    )(dest, h, wts, g_final.reshape(1, d), ys)


def _moe(h, g, w_group, b_group, w_expert, b_expert, w_gate, w_up, w_down, layer, g_final, final_norm,
         tm=256):
    n = h.shape[0]
    ids, wts = _router(h, g, w_group, b_group, w_expert, b_expert)
    n_rows = _sorted_rows(2 * n, tm)
    n_items = n_rows // tm + N_EXPERTS
    dest, plan, fill = _dispatch_plan(ids, tm, n_items)
    xs = _scatter(h, g, dest, fill, n_rows)
    ys = _experts(xs, plan, w_gate, w_up, w_down, layer, n_rows, tm)
    return _combine(h, wts, ys, dest, g_final, final_norm)


def _dot_t(a, b):
    return lax.dot_general(a, b, (((1,), (1,)), ((), ())), preferred_element_type=F32)


def _lane_slab(columns):
    lane = lax.broadcasted_iota(I32, (columns[0].shape[0], LANES), 1)
    slab = jnp.zeros(lane.shape, F32)
    for h, col in enumerate(columns):
        slab = jnp.where(lane == h, col, slab)
    return slab


def _attn_kernel(q_ref, kc_ref, vc_ref, kp_ref, vp_ref, o_ref, lse_ref, *, seg_blocks):
    qblocks = q_ref.shape[0] // BAND
    step = pl.program_id(0)
    banded = seg_blocks > 1
    assert not banded or seg_blocks % qblocks == 0
    n_keys = 2 * BAND if banded else BAND
    qi = lax.broadcasted_iota(I32, (BAND, n_keys), 0)
    kj = lax.broadcasted_iota(I32, (BAND, n_keys), 1)
    mask = (kj >= qi) & (kj <= qi + BAND) if banded else kj <= qi
    scale = HEAD_DIM ** -0.5
    neg = -jnp.inf
    heads = [slice(h * HEAD_DIM, (h + 1) * HEAD_DIM) for h in range(HEADS)]

    for sb in range(qblocks):
        rows = slice(sb * BAND, (sb + 1) * BAND)
        if not banded:
            scores = [_dot_t(q_ref[rows, c], kc_ref[rows, c]) for c in heads]
        elif sb > 0:
            krows = slice((sb - 1) * BAND, (sb + 1) * BAND)
            scores = [_dot_t(q_ref[rows, c], kc_ref[krows, c]) for c in heads]
        else:
            scores = [jnp.concatenate([_dot_t(q_ref[rows, c], kp_ref[:, c]),
                                       _dot_t(q_ref[rows, c], kc_ref[rows, c])], axis=1) for c in heads]
        scores = [jnp.where(mask, s * scale, neg) for s in scores]
        if banded and sb == 0:
            at_start = (step * qblocks) % seg_blocks == 0
            drop = jnp.where(kj < BAND, jnp.where(at_start, neg, 0.0), 0.0)
            scores = [s + drop for s in scores]
        ms = [jnp.max(s, axis=1, keepdims=True) for s in scores]
        ps = [jnp.exp(s - m) for s, m in zip(scores, ms)]
        ls = [jnp.sum(p, axis=1, keepdims=True) for p in ps]
        pb = [p.astype(BF16) for p in ps]
        if not banded:
            accs = [_bdot(p, vc_ref[rows, c]) for p, c in zip(pb, heads)]
        elif sb > 0:
            accs = [_bdot(p, vc_ref[krows, c]) for p, c in zip(pb, heads)]
        else:
            accs = [_bdot(p[:, :BAND], vp_ref[:, c]) + _bdot(p[:, BAND:], vc_ref[rows, c])
                    for p, c in zip(pb, heads)]
        for acc, l, c in zip(accs, ls, heads):
            o_ref[rows, c] = (acc / l).astype(o_ref.dtype)
        lse_ref[rows, :] = _lane_slab([m + jnp.log(l) for m, l in zip(ms, ls)])


def _attn(qkv, cols, seg_blocks, out_dtype, name, rows_per_step=512):
    n = qkv.shape[0]
    cq, ck, cv = cols
    qblocks = rows_per_step // BAND
    return pl.pallas_call(
        functools.partial(_attn_kernel, seg_blocks=seg_blocks),
        grid=(n // rows_per_step,),
        in_specs=[
            pl.BlockSpec((rows_per_step, ATTN_OUT), lambda i: (i, cq)),
            pl.BlockSpec((rows_per_step, ATTN_OUT), lambda i: (i, ck)),
            pl.BlockSpec((rows_per_step, ATTN_OUT), lambda i: (i, cv)),
            pl.BlockSpec((BAND, ATTN_OUT), lambda i: (jnp.maximum(i * qblocks - 1, 0), ck)),
            pl.BlockSpec((BAND, ATTN_OUT), lambda i: (jnp.maximum(i * qblocks - 1, 0), cv)),
        ],
        out_specs=[pl.BlockSpec((rows_per_step, ATTN_OUT), lambda i: (i, 0)),
                   pl.BlockSpec((rows_per_step, LANES), lambda i: (i, 0))],
        out_shape=[jax.ShapeDtypeStruct((n, ATTN_OUT), out_dtype), jax.ShapeDtypeStruct((n, LANES), F32)],
        compiler_params=_params("arbitrary"),
        name=name,
    )(qkv, qkv, qkv, qkv, qkv)


def _attn_skew_kernel(q_ref, k_ref, v_ref, o_ref, lse_ref):
    subs = q_ref.shape[0] // BAND
    keys = k_ref.shape[0]
    u = lax.broadcasted_iota(I32, (BAND, keys), 0)
    kc = lax.broadcasted_iota(I32, (BAND, keys), 1)
    base = subs * (u - (kc & (BAND - 1))) - lax.shift_right_logical(kc, BAND_LOG2)
    scale = HEAD_DIM ** -0.5
    neg = -jnp.inf
    heads = [slice(h * HEAD_DIM, (h + 1) * HEAD_DIM) for h in range(HEADS)]
    for s in range(subs):
        rows = slice(s * BAND, (s + 1) * BAND)
        delta = base + s
        mask = (delta >= 0) & (delta <= BAND)
        scores = [jnp.where(mask, _dot_t(q_ref[rows, c], k_ref[:, c]) * scale, neg) for c in heads]
        ms = [jnp.max(sc, axis=1, keepdims=True) for sc in scores]
        ps = [jnp.exp(sc - m) for sc, m in zip(scores, ms)]
        ls = [jnp.sum(p, axis=1, keepdims=True) for p in ps]
        accs = [_bdot(p.astype(BF16), v_ref[:, c]) for p, c in zip(ps, heads)]
        for acc, l, c in zip(accs, ls, heads):
            o_ref[rows, c] = (acc / l).astype(o_ref.dtype)
        lse_ref[rows, :] = _lane_slab([m + jnp.log(l) for m, l in zip(ms, ls)])


def _attn_skew(qkv, cols, out_dtype, name):
    n = qkv.shape[0]
    cq, ck, cv = cols
    rows = (ATTN_DILATIONS[2] // ATTN_DILATIONS[1]) * BAND
    return pl.pallas_call(
        _attn_skew_kernel,
        grid=(n // rows,),
        in_specs=[pl.BlockSpec((rows, ATTN_OUT), lambda i: (i, cq)),
                  pl.BlockSpec((rows, ATTN_OUT), lambda i: (i, ck)),
                  pl.BlockSpec((rows, ATTN_OUT), lambda i: (i, cv))],
        out_specs=[pl.BlockSpec((rows, ATTN_OUT), lambda i: (i, 0)),
                   pl.BlockSpec((rows, LANES), lambda i: (i, 0))],
        out_shape=[jax.ShapeDtypeStruct((n, ATTN_OUT), out_dtype), jax.ShapeDtypeStruct((n, LANES), F32)],
        compiler_params=_params("arbitrary"),
        name=name,
    )(qkv, qkv, qkv)


def _slot_class(c):
    return (c % 4) * 4 + c // 4


def _qkv_kernel(h_hbm, gq_ref, gkv_ref, wq_ref, wkv_ref, o_ref, x_ref, xq_ref, xkv_ref, sem,
                *, tiles_per_batch, n_groups):
    i, j = pl.program_id(0), pl.program_id(1)
    tiles = pl.num_programs(0)
    tm = x_ref.shape[1]
    q_blocks = n_groups * ATTN_OUT // o_ref.shape[1]

    def fetch(tile, buf, fn):
        if n_groups == 1:
            fn(pltpu.make_async_copy(h_hbm.at[pl.ds(tile * tm, tm), :], x_ref.at[buf], sem.at[buf]))
            return
        per_class = h_hbm.shape[0] // (tiles // tiles_per_batch)
        slots = tm // per_class
        bi, ti = tile // tiles_per_batch, tile % tiles_per_batch
        for c in range(slots):
            cls = _slot_class(ti * slots + c)
            fn(pltpu.make_async_copy(h_hbm.at[pl.ds(bi * per_class, per_class), cls, :],
                                     x_ref.at[buf, pl.ds(c * per_class, per_class), :],
                                     sem.at[buf]))

    @pl.when(j == 0)
    def _():
        buf = i % 2
        _prefetch_tile(i, tiles, fetch)

        def norm(r0):
            x = x_ref[buf, pl.ds(r0, ROW_CHUNK), :]
            xhat = x * lax.rsqrt(jnp.mean(x * x, axis=-1, keepdims=True) + EPS)
            rows = pl.ds(r0, ROW_CHUNK)
            xq_ref[rows, :] = (xhat * gq_ref[...]).astype(BF16)
            xkv_ref[rows, :] = (xhat * gkv_ref[...]).astype(BF16)
        _row_loop(tm, norm)

    @pl.when(j < q_blocks)
    def _():
        o_ref[...] = _bdot(xq_ref[...], wq_ref[...].astype(BF16)).astype(o_ref.dtype)

    @pl.when(j >= q_blocks)
    def _():
        o_ref[...] = _bdot(xkv_ref[...], wkv_ref[...].astype(BF16)).astype(o_ref.dtype)


def _qkv(h, batch, g_q, g_kv, w_q, w_kv, dilated, tm=1024, tn=512):
    n, d = h.shape
    first, n_groups = (1, N_DILATED) if dilated else (0, 1)
    r2 = ATTN_DILATIONS[2]
    hv = h.reshape(n // r2, r2, d) if dilated else h
    per = ATTN_OUT // tn
    q_blocks = n_groups * per
    kv_blocks = 2 * n_groups * per

    def wq_map(i, j):
        return (0, first * per + jnp.minimum(j, q_blocks - 1))

    def wkv_map(i, j):
        jj = jnp.clip(j - q_blocks, 0, kv_blocks - 1)
        return (0, first * per + jj + jnp.where(jj >= q_blocks, (N_ATTN_GROUPS - n_groups) * per, 0))

    return pl.pallas_call(
        functools.partial(_qkv_kernel, tiles_per_batch=(n // batch) // tm, n_groups=n_groups),
        grid=(n // tm, q_blocks + kv_blocks),
        in_specs=[
            pl.BlockSpec(memory_space=pl.ANY),
            pl.BlockSpec((1, d), lambda i, j: (0, 0)),
            pl.BlockSpec((1, d), lambda i, j: (0, 0)),
            pl.BlockSpec((d, tn), wq_map),
            pl.BlockSpec((d, tn), wkv_map),
        ],
        out_specs=pl.BlockSpec((tm, tn), lambda i, j: (i, j)),
        out_shape=jax.ShapeDtypeStruct((n, 3 * n_groups * ATTN_OUT), BF16),
        scratch_shapes=[pltpu.VMEM((2, tm, d), F32), pltpu.VMEM((tm, d), BF16), pltpu.VMEM((tm, d), BF16),
                        pltpu.SemaphoreType.DMA((2,))],
        compiler_params=_params("arbitrary", "arbitrary"),
        name="qkv_dilated" if dilated else "qkv_g0",
    )(hv, g_q.reshape(1, d), g_kv.reshape(1, d), w_q, w_kv)


def _merge_out_kernel(o0_ref, l0_ref, h_ref, w_ref, o1_hbm, l1_hbm, o2_hbm, l2_hbm, out_ref,
                      ob_ref, lb_ref, m_ref, sem, *, tiles_per_batch):
    t = pl.program_id(0)
    tiles = pl.num_programs(0)
    buf = t % 2
    per = ob_ref.shape[2]
    n_cls = ob_ref.shape[3]
    rows = per * n_cls

    def fetch(tile, b, fn):
        bi, ti = tile // tiles_per_batch, tile % tiles_per_batch
        for c in range(n_cls):
            cls = _slot_class(c)
            src_rows = pl.ds(ti * per, per)
            for g, (o_hbm, l_hbm) in enumerate(((o1_hbm, l1_hbm), (o2_hbm, l2_hbm))):
                fn(pltpu.make_async_copy(o_hbm.at[bi * n_cls + c, src_rows, :],
                                         ob_ref.at[b, g, :, cls, :], sem.at[b, 0]))
                fn(pltpu.make_async_copy(l_hbm.at[bi * n_cls + c, src_rows, :],
                                         lb_ref.at[b, g, :, cls, :], sem.at[b, 1]))

    _prefetch_tile(t, tiles, fetch)

    head_of_col = lax.shift_right_logical(lax.broadcasted_iota(I32, (LANES, ATTN_OUT), 1), HEAD_DIM_LOG2)
    spread = jnp.where(head_of_col == lax.broadcasted_iota(I32, (LANES, ATTN_OUT), 0), 1.0, 0.0).astype(BF16)

    chunk = BAND
    for r0 in range(0, rows, chunk):
        rs = slice(r0, r0 + chunk)
        gs = slice(r0 // n_cls, (r0 + chunk) // n_cls)
        lses = (l0_ref[rs, :],
                lb_ref[buf, 0, gs, :, :].reshape(chunk, LANES),
                lb_ref[buf, 1, gs, :, :].reshape(chunk, LANES))
        o1 = ob_ref[buf, 0, gs, :, :].reshape(chunk, ATTN_OUT)
        o2 = ob_ref[buf, 1, gs, :, :].reshape(chunk, ATTN_OUT)
        mx = jnp.maximum(jnp.maximum(lses[0], lses[1]), lses[2])
        ex = [jnp.exp(l - mx) for l in lses]
        inv = 1.0 / (ex[0] + ex[1] + ex[2])
        wide = []
        for e in ex:
            hi, lo = _split_bf16(e * inv)
            wide.append(_bdot(hi, spread) + _bdot(lo, spread))
        merged = wide[0] * o0_ref[rs, :].astype(F32) + wide[1] * o1 + wide[2] * o2
        m_ref[rs, :] = merged.astype(BF16)
    out_ref[...] = h_ref[...] + _bdot(m_ref[...], w_ref[...])


def _merge_out(h, seq, g0, g1, g2, w_o, rows=512):
    n, d = h.shape
    n_cls = ATTN_DILATIONS[2]
    per = rows // n_cls
    cls_len = seq // n_cls
    (o0, l0), (o1, l1), (o2, l2) = g0, g1, g2

    def by_class(a):
        return a.reshape(n // cls_len, cls_len, a.shape[-1])

    return pl.pallas_call(
        functools.partial(_merge_out_kernel, tiles_per_batch=seq // rows),
        grid=(n // rows,),
        in_specs=[
            pl.BlockSpec((rows, ATTN_OUT), lambda t: (t, 0)),
            pl.BlockSpec((rows, LANES), lambda t: (t, 0)),
            pl.BlockSpec((rows, d), lambda t: (t, 0)),
            pl.BlockSpec((ATTN_OUT, d), lambda t: (0, 0)),
            pl.BlockSpec(memory_space=pl.ANY),
            pl.BlockSpec(memory_space=pl.ANY),
            pl.BlockSpec(memory_space=pl.ANY),
            pl.BlockSpec(memory_space=pl.ANY),
        ],
        out_specs=pl.BlockSpec((rows, d), lambda t: (t, 0)),
        out_shape=jax.ShapeDtypeStruct((n, d), F32),
        scratch_shapes=[pltpu.VMEM((2, N_DILATED, per, n_cls, ATTN_OUT), F32),
                        pltpu.VMEM((2, N_DILATED, per, n_cls, LANES), F32),
                        pltpu.VMEM((rows, ATTN_OUT), BF16),
                        pltpu.SemaphoreType.DMA((2, 2))],
        compiler_params=_params("arbitrary"),
        name="attn_merge_out",
    )(o0, l0, h, w_o.astype(BF16), by_class(o1), by_class(l1), by_class(o2), by_class(l2))


def kernel(x, norm_mix_g, norm_ffn_g, conv_w_in, conv_b_in, conv_w_dw, conv_b_dw, conv_ln_g, conv_ln_b,
           conv_w_out, conv_b_out, norm_kv_g, w_kv, attn_w_q, attn_w_o, router_w_group, router_b_group,
           router_w_expert, router_b_expert, expert_w_gate, expert_w_up, expert_w_down, norm_final_g):
    b, s, d = x.shape
    n = b * s
    xf = x.reshape(n, d)

    u = _glu_in(xf, norm_mix_g[0], conv_w_in[0], conv_b_in[0])
    y = _dwconv(u.reshape(b, s, -1), conv_w_dw[0], conv_b_dw[0])
    h = _ln_out(y.reshape(n, -1), conv_ln_g[0], conv_ln_b[0], conv_w_out[0], conv_b_out[0], xf)
    h = _moe(h, norm_ffn_g[0], router_w_group[0], router_b_group[0], router_w_expert[0],
             router_b_expert[0], expert_w_gate, expert_w_up, expert_w_down, 0,
             norm_final_g, False)

    qkv0 = _qkv(h, b, norm_mix_g[1], norm_kv_g, attn_w_q[0], w_kv, dilated=False)
    qkvd = _qkv(h, b, norm_mix_g[1], norm_kv_g, attn_w_q[0], w_kv, dilated=True)
    g0 = _attn(qkv0, (0, 1, 2), s // BAND, BF16, "attn_g0")
    g1 = _attn_skew(qkvd, (0, 2, 4), F32, "attn_g1")
    g2 = _attn(qkvd, (1, 3, 5), 1, F32, "attn_g2")
    h = _merge_out(h, s, g0, g1, g2, attn_w_o[0])
    out = _moe(h, norm_ffn_g[1], router_w_group[1], router_b_group[1], router_w_expert[1],
               router_b_expert[1], expert_w_gate, expert_w_up, expert_w_down, 1,
               norm_final_g, True)
    return out.reshape(b, s, d)
```

```python
import functools

import jax
import jax.numpy as jnp
from jax import lax
from jax.experimental import pallas as pl
from jax.experimental.pallas import tpu as pltpu

F32 = jnp.float32
BF16 = jnp.bfloat16
I32 = jnp.int32

EPS = 1e-6
LANES = 128
V7X_VMEM_BYTES = 64 * 1024 * 1024
VMEM_LIMIT = V7X_VMEM_BYTES - 8 * 1024 * 1024

CONV_WIDTH = 31
HALO = 32
ATTN_DILATIONS = (1, 4, 16)
N_ATTN_GROUPS = 3
N_DILATED = 2
HEADS = 8
HEAD_DIM = 128
HEAD_DIM_LOG2 = 7
BAND = 128
BAND_LOG2 = 7
ATTN_OUT = HEADS * HEAD_DIM
N_GROUPS = 4
EXPERTS_PER_GROUP = 8
N_EXPERTS = N_GROUPS * EXPERTS_PER_GROUP

ROW_CHUNK = 32
ROW_UNROLL = 4
SUBLANES = 8
BULK_DMA_PRIORITY = 1


def _params(*sem):
    return pltpu.CompilerParams(dimension_semantics=sem, vmem_limit_bytes=VMEM_LIMIT)


def _row_loop(n_rows, body, chunk=ROW_CHUNK, unroll=ROW_UNROLL):
    def step(c, carry):
        body(pl.multiple_of(c * chunk, chunk))
        return carry
    lax.fori_loop(0, n_rows // chunk, step, 0, unroll=unroll)


def _rms(x, g):
    ms = jnp.mean(x * x, axis=-1, keepdims=True)
    return x * lax.rsqrt(ms + EPS) * g


def _bdot(a, b):
    return jnp.dot(a, b, preferred_element_type=F32)


def _prefetch_tile(i, n_tiles, fetch, priority=BULK_DMA_PRIORITY):
    buf = i % 2

    def start(copy):
        copy.start(priority=priority)

    @pl.when(i == 0)
    def _():
        fetch(0, 0, start)

    @pl.when(i + 1 < n_tiles)
    def _():
        fetch(i + 1, 1 - buf, start)

    fetch(i, buf, lambda copy: copy.wait())


def _glu_in_kernel(x_hbm, g_ref, wv_ref, wg_ref, bv_ref, bg_ref, o_ref, x_ref, xn_ref, sem):
    tm = x_ref.shape[1]
    i = pl.program_id(0)

    @pl.when(pl.program_id(1) == 0)
    def _():
        def fetch(tile, buf, fn):
            fn(pltpu.make_async_copy(x_hbm.at[pl.ds(tile * tm, tm), :], x_ref.at[buf], sem.at[buf]))
        _prefetch_tile(i, pl.num_programs(0), fetch)

        def norm(r0):
            rows = pl.ds(r0, ROW_CHUNK)
            xn_ref[rows, :] = _rms(x_ref[i % 2, rows, :], g_ref[...]).astype(BF16)
        _row_loop(tm, norm)

    xn = xn_ref[...]
    val = _bdot(xn, wv_ref[...].astype(BF16)) + bv_ref[...]
    gate = _bdot(xn, wg_ref[...].astype(BF16)) + bg_ref[...]
    o_ref[...] = (val * jax.nn.sigmoid(gate)).astype(o_ref.dtype)


def _glu_in(x, g, w_in, b_in, tm=1024, tn=512):
    n, d = x.shape
    c = w_in.shape[1] // 2
    nj = c // tn
    b2 = b_in.reshape(1, 2 * c)
    return pl.pallas_call(
        _glu_in_kernel,
        grid=(n // tm, nj),
        in_specs=[
            pl.BlockSpec(memory_space=pl.ANY),
            pl.BlockSpec((1, d), lambda i, j: (0, 0)),
            pl.BlockSpec((d, tn), lambda i, j: (0, j)),
            pl.BlockSpec((d, tn), lambda i, j: (0, j + nj)),
            pl.BlockSpec((1, tn), lambda i, j: (0, j)),
            pl.BlockSpec((1, tn), lambda i, j: (0, j + nj)),
        ],
        out_specs=pl.BlockSpec((tm, tn), lambda i, j: (i, j)),
        out_shape=jax.ShapeDtypeStruct((n, c), BF16),
        scratch_shapes=[pltpu.VMEM((2, tm, d), F32), pltpu.VMEM((tm, d), BF16), pltpu.SemaphoreType.DMA((2,))],
        compiler_params=_params("arbitrary", "arbitrary"),
        name="glu_in",
    )(x, g.reshape(1, d), w_in, w_in, b2, b2)


def _dwconv_kernel(cur_ref, halo_ref, w_ref, b_ref, o_ref, buf_ref):
    ts, cw = cur_ref.shape[1], cur_ref.shape[2]
    rw = 64
    keep = jnp.where(pl.program_id(1) > 0, 1.0, 0.0)
    buf_ref[0, 0:HALO, :] = halo_ref[0].astype(F32) * keep
    buf_ref[0, HALO:, :] = cur_ref[0].astype(F32)
    shifted_rows = ts + HALO - SUBLANES
    for s in range(1, SUBLANES):
        buf_ref[s, 0:shifted_rows, :] = buf_ref[0, s:s + shifted_rows, :]
    first = HALO - (CONV_WIDTH - 1)
    for c0 in range(0, cw, LANES):
        cols = slice(c0, c0 + LANES)
        for r0 in range(0, ts, rw):
            acc = jnp.broadcast_to(b_ref[:, cols], (rw, LANES))
            for k in range(CONV_WIDTH):
                s = (first + k) % SUBLANES
                start = r0 + first + k - s
                acc = acc + w_ref[k:k + 1, cols] * buf_ref[s, start:start + rw, cols]
            o_ref[0, r0:r0 + rw, cols] = acc.astype(o_ref.dtype)


def _dwconv(u, w_dw, b_dw, ts=256, cw=512):
    b, s, c = u.shape
    hb = ts // HALO
    return pl.pallas_call(
        _dwconv_kernel,
        grid=(b, s // ts, c // cw),
        in_specs=[
            pl.BlockSpec((1, ts, cw), lambda bi, si, ci: (bi, si, ci)),
            pl.BlockSpec((1, HALO, cw), lambda bi, si, ci: (bi, jnp.maximum(si * hb - 1, 0), ci)),
            pl.BlockSpec((CONV_WIDTH, cw), lambda bi, si, ci: (0, ci)),
            pl.BlockSpec((1, cw), lambda bi, si, ci: (0, ci)),
        ],
        out_specs=pl.BlockSpec((1, ts, cw), lambda bi, si, ci: (bi, si, ci)),
        out_shape=jax.ShapeDtypeStruct((b, s, c), BF16),
        scratch_shapes=[pltpu.VMEM((SUBLANES, ts + HALO, cw), F32)],
        compiler_params=_params("arbitrary", "arbitrary", "arbitrary"),
        name="dwconv",
    )(u, u, w_dw, b_dw.reshape(1, c))


def _ln_out_kernel(y_hbm, lg_ref, lb_ref, w_ref, b_ref, res_ref, o_ref, y_ref, a_ref, sem):
    tm = y_ref.shape[1]
    i = pl.program_id(0)

    @pl.when(pl.program_id(1) == 0)
    def _():
        def fetch(tile, buf, fn):
            fn(pltpu.make_async_copy(y_hbm.at[pl.ds(tile * tm, tm), :], y_ref.at[buf], sem.at[buf]))
        _prefetch_tile(i, pl.num_programs(0), fetch)

        def norm(r0):
            rows = pl.ds(r0, ROW_CHUNK)
            y = y_ref[i % 2, rows, :].astype(F32)
            mu = jnp.mean(y, axis=-1, keepdims=True)
            yc = y - mu
            var = jnp.mean(yc * yc, axis=-1, keepdims=True)
            z = yc * lax.rsqrt(var + EPS) * lg_ref[...] + lb_ref[...]
            a_ref[rows, :] = (z * jax.nn.sigmoid(z)).astype(BF16)
        _row_loop(tm, norm)

    o_ref[...] = res_ref[...] + _bdot(a_ref[...], w_ref[...].astype(BF16)) + b_ref[...]


def _ln_out(y, ln_g, ln_b, w_out, b_out, res, tm=1024, tn=512):
    n, c = y.shape
    d = w_out.shape[1]
    return pl.pallas_call(
        _ln_out_kernel,
        grid=(n // tm, d // tn),
        in_specs=[
            pl.BlockSpec(memory_space=pl.ANY),
            pl.BlockSpec((1, c), lambda i, j: (0, 0)),
            pl.BlockSpec((1, c), lambda i, j: (0, 0)),
            pl.BlockSpec((c, tn), lambda i, j: (0, j)),
            pl.BlockSpec((1, tn), lambda i, j: (0, j)),
            pl.BlockSpec((tm, tn), lambda i, j: (i, j)),
        ],
        out_specs=pl.BlockSpec((tm, tn), lambda i, j: (i, j)),
        out_shape=jax.ShapeDtypeStruct((n, d), F32),
        scratch_shapes=[pltpu.VMEM((2, tm, c), BF16), pltpu.VMEM((tm, c), BF16), pltpu.SemaphoreType.DMA((2,))],
        compiler_params=_params("arbitrary", "arbitrary"),
        name="ln_out",
    )(y, ln_g.reshape(1, c), ln_b.reshape(1, c), w_out, b_out.reshape(1, d), res)


def _split_bf16(a):
    hi = a.astype(BF16)
    lo = (a - hi.astype(F32)).astype(BF16)
    return hi, lo


def _router_kernel(h_ref, g_ref, w_ref, b_ref, ids_ref, wts_ref):
    t = _rms(h_ref[...], g_ref[...])
    t_hi, t_lo = _split_bf16(t)
    w_hi, w_lo = _split_bf16(w_ref[...])
    both = _bdot(t_hi, jnp.concatenate([w_hi, w_lo], axis=1))
    logits = both[:, :LANES] + both[:, LANES:] + _bdot(t_lo, w_hi) + b_ref[...]

    lane = lax.broadcasted_iota(I32, logits.shape, 1)
    lanef = lane.astype(F32)
    neg = -jnp.inf
    is_group = lane < N_GROUPS
    glog = jnp.where(is_group, logits, neg)
    gmax = jnp.max(glog, axis=1, keepdims=True)
    gi = jnp.min(jnp.where(glog == gmax, lanef, float(LANES)), axis=1, keepdims=True)
    gsum = jnp.sum(jnp.where(is_group, jnp.exp(logits - gmax), 0.0), axis=1, keepdims=True)
    gp = 1.0 / gsum

    lo_lane = N_GROUPS + gi * EXPERTS_PER_GROUP
    in_sel = (lanef >= lo_lane) & (lanef < lo_lane + EXPERTS_PER_GROUP)
    sel = jnp.where(in_sel, logits, neg)
    m1 = jnp.max(sel, axis=1, keepdims=True)
    i1 = jnp.min(jnp.where(sel == m1, lanef, float(LANES)), axis=1, keepdims=True)
    sel2 = jnp.where(lanef == i1, neg, sel)
    m2 = jnp.max(sel2, axis=1, keepdims=True)
    i2 = jnp.min(jnp.where(sel2 == m2, lanef, float(LANES)), axis=1, keepdims=True)
    e2 = jnp.exp(m2 - m1)
    w1 = gp / (1.0 + e2)
    w2 = gp * e2 / (1.0 + e2)

    ids = jnp.where(lane == 0, i1 - N_GROUPS, jnp.where(lane == 1, i2 - N_GROUPS, 0.0))
    ids_ref[...] = ids.astype(I32)
    wts_ref[...] = jnp.where(lane == 0, w1, jnp.where(lane == 1, w2, 0.0))


def _router(h, g, w_group, b_group, w_expert, b_expert, tm=512):
    n, d = h.shape
    w_e = jnp.transpose(w_expert, (1, 0, 2)).reshape(d, N_EXPERTS)
    pad = LANES - N_GROUPS - N_EXPERTS
    wr = jnp.concatenate([w_group, w_e, jnp.zeros((d, pad), F32)], axis=1)
    br = jnp.concatenate([b_group, b_expert.reshape(N_EXPERTS), jnp.zeros((pad,), F32)]).reshape(1, LANES)
    return pl.pallas_call(
        _router_kernel,
        grid=(n // tm,),
        in_specs=[
            pl.BlockSpec((tm, d), lambda i: (i, 0)),
            pl.BlockSpec((1, d), lambda i: (0, 0)),
            pl.BlockSpec((d, LANES), lambda i: (0, 0)),
            pl.BlockSpec((1, LANES), lambda i: (0, 0)),
        ],
        out_specs=[pl.BlockSpec((tm, LANES), lambda i: (i, 0)),
                   pl.BlockSpec((tm, LANES), lambda i: (i, 0))],
        out_shape=[jax.ShapeDtypeStruct((n, LANES), I32), jax.ShapeDtypeStruct((n, LANES), F32)],
        compiler_params=_params("arbitrary"),
        name="router",
    )(h, g.reshape(1, d), wr, br)


def _sorted_rows(n_assign, tm):
    return -(-(n_assign + N_EXPERTS * SUBLANES) // tm) * tm


def _dispatch_plan(ids, tm, n_items):
    none = N_EXPERTS
    experts = jnp.arange(N_EXPERTS, dtype=I32)
    upto = experts[:, None] <= experts[None, :]

    def prefix(v):
        return jnp.sum(jnp.where(upto, v[:, None], 0), axis=0)

    e_flat = jnp.concatenate([ids[:, 0], ids[:, 1]])
    onehot = (e_flat[:, None] == experts[None, :]).astype(I32)
    csum = jnp.cumsum(onehot, axis=0)
    counts = csum[-1]
    present = counts > 0
    n_rows = _sorted_rows(e_flat.shape[0], tm)
    aligned = (counts + SUBLANES - 1) // SUBLANES * SUBLANES
    last = jnp.max(jnp.where(present, experts, 0))
    padded = aligned + jnp.where(experts == last, n_rows - jnp.sum(aligned), 0)
    seg_ends = prefix(padded)
    starts = seg_ends - padded
    ends = starts + counts
    dest = jnp.sum(onehot * (csum - 1 + starts[None, :]), axis=1)

    tok = jnp.arange(e_flat.shape[0], dtype=I32) % ids.shape[0]
    src = jnp.zeros((n_rows,), I32).at[dest].set(tok, unique_indices=True)

    items = (padded + tm - 1) // tm
    item_end = prefix(items)
    item_start = item_end - items
    total = item_end[-1]

    ordinal = prefix(present.astype(I32)) - 1
    later = jnp.where(present[None, :] & (experts[None, :] > experts[:, None]), experts[None, :], none)
    nxt = jnp.min(later, axis=1)
    nxt2 = jnp.min(jnp.where(experts[None, :] == nxt[:, None], nxt[None, :], none), axis=1)
    ahead_e = jnp.where(nxt2 < none, nxt2, -1)
    head0 = jnp.min(jnp.where(present, experts, none))
    head1 = jnp.min(jnp.where(experts == head0, nxt, none))
    head = jnp.stack([head0, jnp.where(head1 < none, head1, -1)])

    w = jnp.arange(n_items, dtype=I32)
    valid = w < total
    wc = jnp.minimum(w, total - 1)
    e_w = jnp.sum((item_end[None, :] <= wc[:, None]).astype(I32), axis=1)
    table = jnp.stack([item_start, starts, seg_ends, ordinal, ahead_e], axis=1)
    mine = e_w[:, None] == experts[None, :]
    got = jnp.sum(jnp.where(mine[:, :, None], table[None, :, :], 0), axis=1)
    want = got[:, 1] + (wc - got[:, 0]) * tm
    start = jnp.minimum(want, n_rows - tm)
    lo = want - start
    rows = jnp.where(valid, jnp.minimum(got[:, 2] - want, tm), 0)
    prev_e = jnp.concatenate([jnp.full((1,), -1, I32), e_w[:-1]])
    new_e = valid & (e_w != prev_e)
    plan = (start, e_w, lo, rows, new_e, valid, got[:, 3] % 2, got[:, 4], head)
    return dest.astype(I32), tuple(p.astype(I32) for p in plan), src


def _experts_kernel(start_ref, exp_ref, lo_ref, rows_ref, new_ref, valid_ref, slot_ref, ahead_ref, head_ref,
                    src_ref, h_hbm, g_ref, wg_hbm, wu_hbm, wd_hbm, ys_hbm, x_ref, xn_ref, y_ref,
                    wg_st, wu_st, wd_st, wg_bf, wu_bf, wd_bf, sem_w, sem_x, sem_y, *, layer):
    w = pl.program_id(0)
    n_items = pl.num_programs(0)
    groups = x_ref.shape[1]
    tm = groups * SUBLANES
    buf = w % 2
    chunk = 256
    per = ROW_CHUNK // SUBLANES

    def row_copy(item, b, grp, j):
        tok = src_ref[start_ref[item] + grp * SUBLANES + j]
        return pltpu.make_async_copy(h_hbm.at[pl.ds(tok, 1), :], x_ref.at[b, grp, pl.ds(j, 1), :], sem_x.at[b])

    def gather_chunk(item, b, c, fn):
        for k in range(per):
            for j in range(SUBLANES):
                fn(row_copy(item, b, c * per + k, j))

    def gather(item, b, fn):
        def body(c, carry):
            gather_chunk(item, b, c, fn)
            return carry
        lax.fori_loop(0, groups // per, body, 0)

    def norm_chunk(c):
        rows = pl.ds(pl.multiple_of(c * ROW_CHUNK, ROW_CHUNK), ROW_CHUNK)
        x = x_ref[buf, pl.ds(pl.multiple_of(c * per, per), per)].reshape(ROW_CHUNK, -1)
        xn_ref[rows, :] = _rms(x, g_ref[...]).astype(BF16)

    @pl.when(w == 0)
    def _():
        gather(0, 0, lambda copy: copy.start())

    @pl.when(valid_ref[w] == 1)
    def _():
        gather(w, buf, lambda copy: copy.wait())

    nxt = jnp.minimum(w + 1, n_items - 1)

    @pl.when((w + 1 < n_items) & (valid_ref[nxt] == 1))
    def _():
        def body(cc, carry):
            for k in range(ROW_UNROLL):
                gather_chunk(w + 1, 1 - buf, cc * ROW_UNROLL + k, lambda copy: copy.start())
            for k in range(ROW_UNROLL):
                norm_chunk(cc * ROW_UNROLL + k)
            return carry
        lax.fori_loop(0, groups // per // ROW_UNROLL, body, 0)

    @pl.when((valid_ref[w] == 1) & jnp.logical_not((w + 1 < n_items) & (valid_ref[nxt] == 1)))
    def _():
        def body(c, carry):
            norm_chunk(c)
            return carry
        lax.fori_loop(0, groups // per, body, 0, unroll=ROW_UNROLL)

    def fetch(e, s):
        return (pltpu.make_async_copy(wg_hbm.at[layer, e], wg_st.at[s], sem_w.at[s, 0]),
                pltpu.make_async_copy(wu_hbm.at[layer, e], wu_st.at[s], sem_w.at[s, 1]),
                pltpu.make_async_copy(wd_hbm.at[layer, e], wd_st.at[s], sem_w.at[s, 2]))

    @pl.when(w == 0)
    def _():
        for copy in fetch(head_ref[0], 0):
            copy.start(priority=BULK_DMA_PRIORITY)

        @pl.when(head_ref[1] >= 0)
        def _():
            for copy in fetch(head_ref[1], 1):
                copy.start(priority=BULK_DMA_PRIORITY)

    @pl.when(new_ref[w] == 1)
    def _():
        s = slot_ref[w]
        for copy in fetch(exp_ref[w], s):
            copy.wait()

        def cast_in(r0):
            rows = pl.ds(r0, chunk)
            wg_bf[rows, :] = wg_st[s, rows, :].astype(BF16)
            wu_bf[rows, :] = wu_st[s, rows, :].astype(BF16)
        _row_loop(wg_bf.shape[0], cast_in, chunk)

        def cast_down(r0):
            rows = pl.ds(r0, chunk)
            wd_bf[rows, :] = wd_st[s, rows, :].astype(BF16)
        _row_loop(wd_bf.shape[0], cast_down, chunk)

        @pl.when(ahead_ref[w] >= 0)
        def _():
            for copy in fetch(ahead_ref[w], s):
                copy.start(priority=BULK_DMA_PRIORITY)

    def put(item, b, fn):
        rows, lo, start = rows_ref[item], lo_ref[item], start_ref[item]
        size = tm
        while size >= SUBLANES:
            off = lo + (rows & ~(2 * size - 1))

            @pl.when((rows & size) != 0)
            def _(size=size, off=off):
                src = y_ref.at[b, pl.ds(pl.multiple_of(off, SUBLANES), size), :]
                dst = ys_hbm.at[pl.ds(pl.multiple_of(start + off, SUBLANES), size), :]
                fn(pltpu.make_async_copy(src, dst, sem_y.at[b]))
            size //= 2

    @pl.when(w >= 2)
    def _():
        put(w - 2, buf, lambda copy: copy.wait())

    @pl.when(valid_ref[w] == 1)
    def _():
        x = xn_ref[...]
        hg = _bdot(x, wg_bf[...])
        hu = _bdot(x, wu_bf[...])
        act = (hg * jax.nn.sigmoid(hg) * hu).astype(BF16)
        y_ref[buf] = _bdot(act, wd_bf[...])
        put(w, buf, lambda copy: copy.start())

    @pl.when(w == n_items - 1)
    def _():
        @pl.when(w >= 1)
        def _():
            put(w - 1, 1 - buf, lambda copy: copy.wait())
        put(w, buf, lambda copy: copy.wait())


def _experts(h, g, plan, src, w_gate, w_up, w_down, layer, n_rows, tm):
    d = h.shape[1]
    f = w_gate.shape[3]
    n_items = plan[0].shape[0]
    any_space = pl.BlockSpec(memory_space=pl.ANY)
    return pl.pallas_call(
        functools.partial(_experts_kernel, layer=layer),
        grid_spec=pltpu.PrefetchScalarGridSpec(
            num_scalar_prefetch=len(plan) + 1,
            grid=(n_items,),
            in_specs=[any_space, pl.BlockSpec((1, d), lambda w, *_: (0, 0)), any_space, any_space, any_space],
            out_specs=any_space,
            scratch_shapes=[
                pltpu.VMEM((2, tm // SUBLANES, SUBLANES, d), F32), pltpu.VMEM((tm, d), BF16),
                pltpu.VMEM((2, tm, d), F32),
                pltpu.VMEM((2, d, f), F32), pltpu.VMEM((2, d, f), F32), pltpu.VMEM((2, f, d), F32),
                pltpu.VMEM((d, f), BF16), pltpu.VMEM((d, f), BF16), pltpu.VMEM((f, d), BF16),
                pltpu.SemaphoreType.DMA((2, 3)), pltpu.SemaphoreType.DMA((2,)), pltpu.SemaphoreType.DMA((2,)),
            ],
        ),
        out_shape=jax.ShapeDtypeStruct((n_rows, d), F32),
        compiler_params=_params("arbitrary"),
        name="moe_experts",
    )(*plan, src, h, g.reshape(1, d), w_gate, w_up, w_down)


def _combine_kernel(dest_ref, h_ref, wts_ref, g_ref, ys_ref, o_ref, y_ref, sem, *, final_norm):
    tm = h_ref.shape[0]
    i = pl.program_id(0)
    steps = pl.num_programs(0)
    n = steps * tm
    buf = i % 2
    groups = tm // SUBLANES

    def row_copy(step, b, grp, j, pick):
        d = dest_ref[pick * n + step * tm + grp * SUBLANES + j]
        return pltpu.make_async_copy(ys_ref.at[pl.ds(d, 1), :], y_ref.at[b, pick, grp, pl.ds(j, 1), :],
                                     sem.at[b])

    def for_group(step, b, grp, fn):
        for j in range(SUBLANES):
            fn(row_copy(step, b, grp, j, 0), 0)
            fn(row_copy(step, b, grp, j, 1), 1)

    def start(copy, pick):
        copy.start(priority=pick)

    def for_rows(step, b, fn):
        def body(grp, carry):
            for_group(step, b, grp, fn)
            return carry
        lax.fori_loop(0, groups, body, 0)

    per = ROW_CHUNK // SUBLANES
    chunks = groups // per

    def mix(c):
        rows = pl.ds(pl.multiple_of(c * ROW_CHUNK, ROW_CHUNK), ROW_CHUNK)
        grps = pl.ds(pl.multiple_of(c * per, per), per)
        wts = wts_ref[rows, :]
        y0 = y_ref[buf, 0, grps].reshape(ROW_CHUNK, -1)
        y1 = y_ref[buf, 1, grps].reshape(ROW_CHUNK, -1)
        out = h_ref[rows, :] + wts[:, 0:1] * y0 + wts[:, 1:2] * y1
        if final_norm:
            out = _rms(out, g_ref[...])
        o_ref[rows, :] = out

    @pl.when(i == 0)
    def _():
        for_rows(0, 0, start)

    for_rows(i, buf, lambda copy, pick: copy.wait())

    @pl.when(i + 1 < steps)
    def _():
        def body(c, carry):
            for k in range(per):
                for_group(i + 1, 1 - buf, c * per + k, start)
            mix(c)
            return carry
        lax.fori_loop(0, chunks, body, 0)

    @pl.when(i + 1 == steps)
    def _():
        def body(c, carry):
            mix(c)
            return carry
        lax.fori_loop(0, chunks, body, 0, unroll=ROW_UNROLL)


def _combine(h, wts, ys, dest, g_final, final_norm, tm=256):
    n, d = h.shape
    return pl.pallas_call(
        functools.partial(_combine_kernel, final_norm=final_norm),
        grid_spec=pltpu.PrefetchScalarGridSpec(
            num_scalar_prefetch=1,
            grid=(n // tm,),
            in_specs=[pl.BlockSpec((tm, d), lambda i, dest: (i, 0)),
                      pl.BlockSpec((tm, LANES), lambda i, dest: (i, 0)),
                      pl.BlockSpec((1, d), lambda i, dest: (0, 0)),
                      pl.BlockSpec(memory_space=pl.ANY)],
            out_specs=pl.BlockSpec((tm, d), lambda i, dest: (i, 0)),
            scratch_shapes=[pltpu.VMEM((2, 2, tm // SUBLANES, SUBLANES, d), F32),
                            pltpu.SemaphoreType.DMA((2,))],
        ),
        out_shape=jax.ShapeDtypeStruct((n, d), F32),
        compiler_params=_params("arbitrary"),
        name="moe_combine",
    )(dest, h, wts, g_final.reshape(1, d), ys)


def _moe(h, g, w_group, b_group, w_expert, b_expert, w_gate, w_up, w_down, layer, g_final, final_norm,
         tm=256):
    n = h.shape[0]
    ids, wts = _router(h, g, w_group, b_group, w_expert, b_expert)
    n_rows = _sorted_rows(2 * n, tm)
    n_items = n_rows // tm + N_EXPERTS
    dest, plan, src = _dispatch_plan(ids, tm, n_items)
    ys = _experts(h, g, plan, src, w_gate, w_up, w_down, layer, n_rows, tm)
    return _combine(h, wts, ys, dest, g_final, final_norm)


def _dot_t(a, b):
    return lax.dot_general(a, b, (((1,), (1,)), ((), ())), preferred_element_type=F32)


def _lane_slab(columns):
    lane = lax.broadcasted_iota(I32, (columns[0].shape[0], LANES), 1)
    slab = jnp.zeros(lane.shape, F32)
    for h, col in enumerate(columns):
        slab = jnp.where(lane == h, col, slab)
    return slab


def _attn_kernel(q_ref, kc_ref, vc_ref, kp_ref, vp_ref, o_ref, lse_ref, *, seg_blocks):
    qblocks = q_ref.shape[0] // BAND
    step = pl.program_id(0)
    banded = seg_blocks > 1
    assert not banded or seg_blocks % qblocks == 0
    n_keys = 2 * BAND if banded else BAND
    qi = lax.broadcasted_iota(I32, (BAND, n_keys), 0)
    kj = lax.broadcasted_iota(I32, (BAND, n_keys), 1)
    mask = (kj >= qi) & (kj <= qi + BAND) if banded else kj <= qi
    scale = HEAD_DIM ** -0.5
    neg = -jnp.inf
    heads = [slice(h * HEAD_DIM, (h + 1) * HEAD_DIM) for h in range(HEADS)]

    for sb in range(qblocks):
        rows = slice(sb * BAND, (sb + 1) * BAND)
        if not banded:
            scores = [_dot_t(q_ref[rows, c], kc_ref[rows, c]) for c in heads]
        elif sb > 0:
            krows = slice((sb - 1) * BAND, (sb + 1) * BAND)
            scores = [_dot_t(q_ref[rows, c], kc_ref[krows, c]) for c in heads]
        else:
            scores = [jnp.concatenate([_dot_t(q_ref[rows, c], kp_ref[:, c]),
                                       _dot_t(q_ref[rows, c], kc_ref[rows, c])], axis=1) for c in heads]
        scores = [jnp.where(mask, s * scale, neg) for s in scores]
        if banded and sb == 0:
            at_start = (step * qblocks) % seg_blocks == 0
            drop = jnp.where(kj < BAND, jnp.where(at_start, neg, 0.0), 0.0)
            scores = [s + drop for s in scores]
        ms = [jnp.max(s, axis=1, keepdims=True) for s in scores]
        ps = [jnp.exp(s - m) for s, m in zip(scores, ms)]
        ls = [jnp.sum(p, axis=1, keepdims=True) for p in ps]
        pb = [p.astype(BF16) for p in ps]
        if not banded:
            accs = [_bdot(p, vc_ref[rows, c]) for p, c in zip(pb, heads)]
        elif sb > 0:
            accs = [_bdot(p, vc_ref[krows, c]) for p, c in zip(pb, heads)]
        else:
            accs = [_bdot(p[:, :BAND], vp_ref[:, c]) + _bdot(p[:, BAND:], vc_ref[rows, c])
                    for p, c in zip(pb, heads)]
        for acc, l, c in zip(accs, ls, heads):
            o_ref[rows, c] = (acc / l).astype(o_ref.dtype)
        lse_ref[rows, :] = _lane_slab([m + jnp.log(l) for m, l in zip(ms, ls)])


def _attn(qkv, cols, seg_blocks, out_dtype, name, rows_per_step=512):
    n = qkv.shape[0]
    cq, ck, cv = cols
    qblocks = rows_per_step // BAND
    return pl.pallas_call(
        functools.partial(_attn_kernel, seg_blocks=seg_blocks),
        grid=(n // rows_per_step,),
        in_specs=[
            pl.BlockSpec((rows_per_step, ATTN_OUT), lambda i: (i, cq)),
            pl.BlockSpec((rows_per_step, ATTN_OUT), lambda i: (i, ck)),
            pl.BlockSpec((rows_per_step, ATTN_OUT), lambda i: (i, cv)),
            pl.BlockSpec((BAND, ATTN_OUT), lambda i: (jnp.maximum(i * qblocks - 1, 0), ck)),
            pl.BlockSpec((BAND, ATTN_OUT), lambda i: (jnp.maximum(i * qblocks - 1, 0), cv)),
        ],
        out_specs=[pl.BlockSpec((rows_per_step, ATTN_OUT), lambda i: (i, 0)),
                   pl.BlockSpec((rows_per_step, LANES), lambda i: (i, 0))],
        out_shape=[jax.ShapeDtypeStruct((n, ATTN_OUT), out_dtype), jax.ShapeDtypeStruct((n, LANES), F32)],
        compiler_params=_params("arbitrary"),
        name=name,
    )(qkv, qkv, qkv, qkv, qkv)


def _attn_skew_kernel(q_ref, k_ref, v_ref, o_ref, lse_ref):
    subs = q_ref.shape[0] // BAND
    keys = k_ref.shape[0]
    u = lax.broadcasted_iota(I32, (BAND, keys), 0)
    kc = lax.broadcasted_iota(I32, (BAND, keys), 1)
    base = subs * (u - (kc & (BAND - 1))) - lax.shift_right_logical(kc, BAND_LOG2)
    scale = HEAD_DIM ** -0.5
    neg = -jnp.inf
    heads = [slice(h * HEAD_DIM, (h + 1) * HEAD_DIM) for h in range(HEADS)]
    for s in range(subs):
        rows = slice(s * BAND, (s + 1) * BAND)
        delta = base + s
        mask = (delta >= 0) & (delta <= BAND)
        scores = [jnp.where(mask, _dot_t(q_ref[rows, c], k_ref[:, c]) * scale, neg) for c in heads]
        ms = [jnp.max(sc, axis=1, keepdims=True) for sc in scores]
        ps = [jnp.exp(sc - m) for sc, m in zip(scores, ms)]
        ls = [jnp.sum(p, axis=1, keepdims=True) for p in ps]
        accs = [_bdot(p.astype(BF16), v_ref[:, c]) for p, c in zip(ps, heads)]
        for acc, l, c in zip(accs, ls, heads):
            o_ref[rows, c] = (acc / l).astype(o_ref.dtype)
        lse_ref[rows, :] = _lane_slab([m + jnp.log(l) for m, l in zip(ms, ls)])


def _attn_skew(qkv, cols, out_dtype, name):
    n = qkv.shape[0]
    cq, ck, cv = cols
    rows = (ATTN_DILATIONS[2] // ATTN_DILATIONS[1]) * BAND
    return pl.pallas_call(
        _attn_skew_kernel,
        grid=(n // rows,),
        in_specs=[pl.BlockSpec((rows, ATTN_OUT), lambda i: (i, cq)),
                  pl.BlockSpec((rows, ATTN_OUT), lambda i: (i, ck)),
                  pl.BlockSpec((rows, ATTN_OUT), lambda i: (i, cv))],
        out_specs=[pl.BlockSpec((rows, ATTN_OUT), lambda i: (i, 0)),
                   pl.BlockSpec((rows, LANES), lambda i: (i, 0))],
        out_shape=[jax.ShapeDtypeStruct((n, ATTN_OUT), out_dtype), jax.ShapeDtypeStruct((n, LANES), F32)],
        compiler_params=_params("arbitrary"),
        name=name,
    )(qkv, qkv, qkv)


def _slot_class(c):
    return (c % 4) * 4 + c // 4


def _qkv_kernel(h_hbm, gq_ref, gkv_ref, wq_ref, wkv_ref, o_ref, x_ref, xq_ref, xkv_ref, sem,
                *, tiles_per_batch, n_groups):
    i, j = pl.program_id(0), pl.program_id(1)
    tiles = pl.num_programs(0)
    tm = x_ref.shape[1]
    q_blocks = n_groups * ATTN_OUT // o_ref.shape[1]

    def fetch(tile, buf, fn):
        if n_groups == 1:
            fn(pltpu.make_async_copy(h_hbm.at[pl.ds(tile * tm, tm), :], x_ref.at[buf], sem.at[buf]))
            return
        per_class = h_hbm.shape[0] // (tiles // tiles_per_batch)
        slots = tm // per_class
        bi, ti = tile // tiles_per_batch, tile % tiles_per_batch
        for c in range(slots):
            cls = _slot_class(ti * slots + c)
            fn(pltpu.make_async_copy(h_hbm.at[pl.ds(bi * per_class, per_class), cls, :],
                                     x_ref.at[buf, pl.ds(c * per_class, per_class), :],
                                     sem.at[buf]))

    @pl.when(j == 0)
    def _():
        buf = i % 2
        _prefetch_tile(i, tiles, fetch)

        def norm(r0):
            x = x_ref[buf, pl.ds(r0, ROW_CHUNK), :]
            xhat = x * lax.rsqrt(jnp.mean(x * x, axis=-1, keepdims=True) + EPS)
            rows = pl.ds(r0, ROW_CHUNK)
            xq_ref[rows, :] = (xhat * gq_ref[...]).astype(BF16)
            xkv_ref[rows, :] = (xhat * gkv_ref[...]).astype(BF16)
        _row_loop(tm, norm)

    @pl.when(j < q_blocks)
    def _():
        o_ref[...] = _bdot(xq_ref[...], wq_ref[...].astype(BF16)).astype(o_ref.dtype)

    @pl.when(j >= q_blocks)
    def _():
        o_ref[...] = _bdot(xkv_ref[...], wkv_ref[...].astype(BF16)).astype(o_ref.dtype)


def _qkv(h, batch, g_q, g_kv, w_q, w_kv, dilated, tm=1024, tn=512):
    n, d = h.shape
    first, n_groups = (1, N_DILATED) if dilated else (0, 1)
    r2 = ATTN_DILATIONS[2]
    hv = h.reshape(n // r2, r2, d) if dilated else h
    per = ATTN_OUT // tn
    q_blocks = n_groups * per
    kv_blocks = 2 * n_groups * per

    def wq_map(i, j):
        return (0, first * per + jnp.minimum(j, q_blocks - 1))

    def wkv_map(i, j):
        jj = jnp.clip(j - q_blocks, 0, kv_blocks - 1)
        return (0, first * per + jj + jnp.where(jj >= q_blocks, (N_ATTN_GROUPS - n_groups) * per, 0))

    return pl.pallas_call(
        functools.partial(_qkv_kernel, tiles_per_batch=(n // batch) // tm, n_groups=n_groups),
        grid=(n // tm, q_blocks + kv_blocks),
        in_specs=[
            pl.BlockSpec(memory_space=pl.ANY),
            pl.BlockSpec((1, d), lambda i, j: (0, 0)),
            pl.BlockSpec((1, d), lambda i, j: (0, 0)),
            pl.BlockSpec((d, tn), wq_map),
            pl.BlockSpec((d, tn), wkv_map),
        ],
        out_specs=pl.BlockSpec((tm, tn), lambda i, j: (i, j)),
        out_shape=jax.ShapeDtypeStruct((n, 3 * n_groups * ATTN_OUT), BF16),
        scratch_shapes=[pltpu.VMEM((2, tm, d), F32), pltpu.VMEM((tm, d), BF16), pltpu.VMEM((tm, d), BF16),
                        pltpu.SemaphoreType.DMA((2,))],
        compiler_params=_params("arbitrary", "arbitrary"),
        name="qkv_dilated" if dilated else "qkv_g0",
    )(hv, g_q.reshape(1, d), g_kv.reshape(1, d), w_q, w_kv)


def _merge_out_kernel(o0_ref, l0_ref, h_ref, w_ref, o1_hbm, l1_hbm, o2_hbm, l2_hbm, out_ref,
                      ob_ref, lb_ref, m_ref, sem, *, tiles_per_batch):
    t = pl.program_id(0)
    tiles = pl.num_programs(0)
    buf = t % 2
    per = ob_ref.shape[2]
    n_cls = ob_ref.shape[3]
    rows = per * n_cls

    def fetch(tile, b, fn):
        bi, ti = tile // tiles_per_batch, tile % tiles_per_batch
        for c in range(n_cls):
            cls = _slot_class(c)
            src_rows = pl.ds(ti * per, per)
            for g, (o_hbm, l_hbm) in enumerate(((o1_hbm, l1_hbm), (o2_hbm, l2_hbm))):
                fn(pltpu.make_async_copy(o_hbm.at[bi * n_cls + c, src_rows, :],
                                         ob_ref.at[b, g, :, cls, :], sem.at[b, 0]))
                fn(pltpu.make_async_copy(l_hbm.at[bi * n_cls + c, src_rows, :],
                                         lb_ref.at[b, g, :, cls, :], sem.at[b, 1]))

    _prefetch_tile(t, tiles, fetch)

    head_of_col = lax.shift_right_logical(lax.broadcasted_iota(I32, (LANES, ATTN_OUT), 1), HEAD_DIM_LOG2)
    spread = jnp.where(head_of_col == lax.broadcasted_iota(I32, (LANES, ATTN_OUT), 0), 1.0, 0.0).astype(BF16)

    chunk = BAND
    for r0 in range(0, rows, chunk):
        rs = slice(r0, r0 + chunk)
        gs = slice(r0 // n_cls, (r0 + chunk) // n_cls)
        lses = (l0_ref[rs, :],
                lb_ref[buf, 0, gs, :, :].reshape(chunk, LANES),
                lb_ref[buf, 1, gs, :, :].reshape(chunk, LANES))
        o1 = ob_ref[buf, 0, gs, :, :].reshape(chunk, ATTN_OUT)
        o2 = ob_ref[buf, 1, gs, :, :].reshape(chunk, ATTN_OUT)
        mx = jnp.maximum(jnp.maximum(lses[0], lses[1]), lses[2])
        ex = [jnp.exp(l - mx) for l in lses]
        inv = 1.0 / (ex[0] + ex[1] + ex[2])
        wide = []
        for e in ex:
            hi, lo = _split_bf16(e * inv)
            wide.append(_bdot(hi, spread) + _bdot(lo, spread))
        merged = wide[0] * o0_ref[rs, :].astype(F32) + wide[1] * o1 + wide[2] * o2
        m_ref[rs, :] = merged.astype(BF16)
    out_ref[...] = h_ref[...] + _bdot(m_ref[...], w_ref[...])


def _merge_out(h, seq, g0, g1, g2, w_o, rows=512):
    n, d = h.shape
    n_cls = ATTN_DILATIONS[2]
    per = rows // n_cls
    cls_len = seq // n_cls
    (o0, l0), (o1, l1), (o2, l2) = g0, g1, g2

    def by_class(a):
        return a.reshape(n // cls_len, cls_len, a.shape[-1])

    return pl.pallas_call(
        functools.partial(_merge_out_kernel, tiles_per_batch=seq // rows),
        grid=(n // rows,),
        in_specs=[
            pl.BlockSpec((rows, ATTN_OUT), lambda t: (t, 0)),
            pl.BlockSpec((rows, LANES), lambda t: (t, 0)),
            pl.BlockSpec((rows, d), lambda t: (t, 0)),
            pl.BlockSpec((ATTN_OUT, d), lambda t: (0, 0)),
            pl.BlockSpec(memory_space=pl.ANY),
            pl.BlockSpec(memory_space=pl.ANY),
            pl.BlockSpec(memory_space=pl.ANY),
            pl.BlockSpec(memory_space=pl.ANY),
        ],
        out_specs=pl.BlockSpec((rows, d), lambda t: (t, 0)),
        out_shape=jax.ShapeDtypeStruct((n, d), F32),
        scratch_shapes=[pltpu.VMEM((2, N_DILATED, per, n_cls, ATTN_OUT), F32),
                        pltpu.VMEM((2, N_DILATED, per, n_cls, LANES), F32),
                        pltpu.VMEM((rows, ATTN_OUT), BF16),
                        pltpu.SemaphoreType.DMA((2, 2))],
        compiler_params=_params("arbitrary"),
        name="attn_merge_out",
    )(o0, l0, h, w_o.astype(BF16), by_class(o1), by_class(l1), by_class(o2), by_class(l2))


def kernel(x, norm_mix_g, norm_ffn_g, conv_w_in, conv_b_in, conv_w_dw, conv_b_dw, conv_ln_g, conv_ln_b,
           conv_w_out, conv_b_out, norm_kv_g, w_kv, attn_w_q, attn_w_o, router_w_group, router_b_group,
           router_w_expert, router_b_expert, expert_w_gate, expert_w_up, expert_w_down, norm_final_g):
    b, s, d = x.shape
    n = b * s
    xf = x.reshape(n, d)

    u = _glu_in(xf, norm_mix_g[0], conv_w_in[0], conv_b_in[0])
    y = _dwconv(u.reshape(b, s, -1), conv_w_dw[0], conv_b_dw[0])
    h = _ln_out(y.reshape(n, -1), conv_ln_g[0], conv_ln_b[0], conv_w_out[0], conv_b_out[0], xf)
    h = _moe(h, norm_ffn_g[0], router_w_group[0], router_b_group[0], router_w_expert[0],
             router_b_expert[0], expert_w_gate, expert_w_up, expert_w_down, 0,
             norm_final_g, False)

    qkv0 = _qkv(h, b, norm_mix_g[1], norm_kv_g, attn_w_q[0], w_kv, dilated=False)
    qkvd = _qkv(h, b, norm_mix_g[1], norm_kv_g, attn_w_q[0], w_kv, dilated=True)
    g0 = _attn(qkv0, (0, 1, 2), s // BAND, BF16, "attn_g0")
    g1 = _attn_skew(qkvd, (0, 2, 4), F32, "attn_g1")
    g2 = _attn(qkvd, (1, 3, 5), 1, F32, "attn_g2")
    h = _merge_out(h, s, g0, g1, g2, attn_w_o[0])
    out = _moe(h, norm_ffn_g[1], router_w_group[1], router_b_group[1], router_w_expert[1],
               router_b_expert[1], expert_w_gate, expert_w_up, expert_w_down, 1,
               norm_final_g, True)
    return out.reshape(b, s, d)
```

```python
import functools

import jax
import jax.numpy as jnp
from jax import lax
from jax.experimental import pallas as pl
from jax.experimental.pallas import tpu as pltpu

F32 = jnp.float32
BF16 = jnp.bfloat16
I32 = jnp.int32

EPS = 1e-6
LANES = 128
V7X_VMEM_BYTES = 64 * 1024 * 1024
VMEM_LIMIT = V7X_VMEM_BYTES - 8 * 1024 * 1024

CONV_WIDTH = 31
HALO = 32
ATTN_DILATIONS = (1, 4, 16)
N_ATTN_GROUPS = 3
N_DILATED = 2
HEADS = 8
HEAD_DIM = 128
HEAD_DIM_LOG2 = 7
BAND = 128
BAND_LOG2 = 7
ATTN_OUT = HEADS * HEAD_DIM
N_GROUPS = 4
EXPERTS_PER_GROUP = 8
N_EXPERTS = N_GROUPS * EXPERTS_PER_GROUP

ROW_CHUNK = 32
ROW_UNROLL = 4
SUBLANES = 8
BULK_DMA_PRIORITY = 1


def _params(*sem):
    return pltpu.CompilerParams(dimension_semantics=sem, vmem_limit_bytes=VMEM_LIMIT)


def _row_loop(n_rows, body, chunk=ROW_CHUNK, unroll=ROW_UNROLL):
    def step(c, carry):
        body(pl.multiple_of(c * chunk, chunk))
        return carry
    lax.fori_loop(0, n_rows // chunk, step, 0, unroll=unroll)


def _rms(x, g):
    ms = jnp.mean(x * x, axis=-1, keepdims=True)
    return x * lax.rsqrt(ms + EPS) * g


def _bdot(a, b):
    return jnp.dot(a, b, preferred_element_type=F32)


def _prefetch_tile(i, n_tiles, fetch, priority=BULK_DMA_PRIORITY):
    buf = i % 2

    def start(copy):
        copy.start(priority=priority)

    @pl.when(i == 0)
    def _():
        fetch(0, 0, start)

    @pl.when(i + 1 < n_tiles)
    def _():
        fetch(i + 1, 1 - buf, start)

    fetch(i, buf, lambda copy: copy.wait())


def _glu_in_kernel(x_hbm, g_ref, wv_ref, wg_ref, bv_ref, bg_ref, o_ref, x_ref, xn_ref, sem):
    tm = x_ref.shape[1]
    i = pl.program_id(0)

    @pl.when(pl.program_id(1) == 0)
    def _():
        def fetch(tile, buf, fn):
            fn(pltpu.make_async_copy(x_hbm.at[pl.ds(tile * tm, tm), :], x_ref.at[buf], sem.at[buf]))
        _prefetch_tile(i, pl.num_programs(0), fetch)

        def norm(r0):
            rows = pl.ds(r0, ROW_CHUNK)
            xn_ref[rows, :] = _rms(x_ref[i % 2, rows, :], g_ref[...]).astype(BF16)
        _row_loop(tm, norm)

    xn = xn_ref[...]
    val = _bdot(xn, wv_ref[...].astype(BF16)) + bv_ref[...]
    gate = _bdot(xn, wg_ref[...].astype(BF16)) + bg_ref[...]
    o_ref[...] = (val * jax.nn.sigmoid(gate)).astype(o_ref.dtype)


def _glu_in(x, g, w_in, b_in, tm=1024, tn=512):
    n, d = x.shape
    c = w_in.shape[1] // 2
    nj = c // tn
    b2 = b_in.reshape(1, 2 * c)
    return pl.pallas_call(
        _glu_in_kernel,
        grid=(n // tm, nj),
        in_specs=[
            pl.BlockSpec(memory_space=pl.ANY),
            pl.BlockSpec((1, d), lambda i, j: (0, 0)),
            pl.BlockSpec((d, tn), lambda i, j: (0, j)),
            pl.BlockSpec((d, tn), lambda i, j: (0, j + nj)),
            pl.BlockSpec((1, tn), lambda i, j: (0, j)),
            pl.BlockSpec((1, tn), lambda i, j: (0, j + nj)),
        ],
        out_specs=pl.BlockSpec((tm, tn), lambda i, j: (i, j)),
        out_shape=jax.ShapeDtypeStruct((n, c), BF16),
        scratch_shapes=[pltpu.VMEM((2, tm, d), F32), pltpu.VMEM((tm, d), BF16), pltpu.SemaphoreType.DMA((2,))],
        compiler_params=_params("arbitrary", "arbitrary"),
        name="glu_in",
    )(x, g.reshape(1, d), w_in, w_in, b2, b2)


def _dwconv_kernel(cur_ref, halo_ref, w_ref, b_ref, o_ref, buf_ref):
    ts, cw = cur_ref.shape[1], cur_ref.shape[2]
    rw = 64
    keep = jnp.where(pl.program_id(1) > 0, 1.0, 0.0)
    buf_ref[0, 0:HALO, :] = halo_ref[0].astype(F32) * keep
    buf_ref[0, HALO:, :] = cur_ref[0].astype(F32)
    shifted_rows = ts + HALO - SUBLANES
    for s in range(1, SUBLANES):
        buf_ref[s, 0:shifted_rows, :] = buf_ref[0, s:s + shifted_rows, :]
    first = HALO - (CONV_WIDTH - 1)
    for c0 in range(0, cw, LANES):
        cols = slice(c0, c0 + LANES)
        for r0 in range(0, ts, rw):
            acc = jnp.broadcast_to(b_ref[:, cols], (rw, LANES))
            for k in range(CONV_WIDTH):
                s = (first + k) % SUBLANES
                start = r0 + first + k - s
                acc = acc + w_ref[k:k + 1, cols] * buf_ref[s, start:start + rw, cols]
            o_ref[0, r0:r0 + rw, cols] = acc.astype(o_ref.dtype)


def _dwconv(u, w_dw, b_dw, ts=256, cw=512):
    b, s, c = u.shape
    hb = ts // HALO
    return pl.pallas_call(
        _dwconv_kernel,
        grid=(b, s // ts, c // cw),
        in_specs=[
            pl.BlockSpec((1, ts, cw), lambda bi, si, ci: (bi, si, ci)),
            pl.BlockSpec((1, HALO, cw), lambda bi, si, ci: (bi, jnp.maximum(si * hb - 1, 0), ci)),
            pl.BlockSpec((CONV_WIDTH, cw), lambda bi, si, ci: (0, ci)),
            pl.BlockSpec((1, cw), lambda bi, si, ci: (0, ci)),
        ],
        out_specs=pl.BlockSpec((1, ts, cw), lambda bi, si, ci: (bi, si, ci)),
        out_shape=jax.ShapeDtypeStruct((b, s, c), BF16),
        scratch_shapes=[pltpu.VMEM((SUBLANES, ts + HALO, cw), F32)],
        compiler_params=_params("arbitrary", "arbitrary", "arbitrary"),
        name="dwconv",
    )(u, u, w_dw, b_dw.reshape(1, c))


def _ln_out_kernel(y_hbm, lg_ref, lb_ref, w_ref, b_ref, res_ref, o_ref, y_ref, a_ref, sem):
    tm = y_ref.shape[1]
    i = pl.program_id(0)

    @pl.when(pl.program_id(1) == 0)
    def _():
        def fetch(tile, buf, fn):
            fn(pltpu.make_async_copy(y_hbm.at[pl.ds(tile * tm, tm), :], y_ref.at[buf], sem.at[buf]))
        _prefetch_tile(i, pl.num_programs(0), fetch)

        def norm(r0):
            rows = pl.ds(r0, ROW_CHUNK)
            y = y_ref[i % 2, rows, :].astype(F32)
            mu = jnp.mean(y, axis=-1, keepdims=True)
            yc = y - mu
            var = jnp.mean(yc * yc, axis=-1, keepdims=True)
            z = yc * lax.rsqrt(var + EPS) * lg_ref[...] + lb_ref[...]
            a_ref[rows, :] = (z * jax.nn.sigmoid(z)).astype(BF16)
        _row_loop(tm, norm)

    o_ref[...] = res_ref[...] + _bdot(a_ref[...], w_ref[...].astype(BF16)) + b_ref[...]


def _ln_out(y, ln_g, ln_b, w_out, b_out, res, tm=1024, tn=1024):
    n, c = y.shape
    d = w_out.shape[1]
    return pl.pallas_call(
        _ln_out_kernel,
        grid=(n // tm, d // tn),
        in_specs=[
            pl.BlockSpec(memory_space=pl.ANY),
            pl.BlockSpec((1, c), lambda i, j: (0, 0)),
            pl.BlockSpec((1, c), lambda i, j: (0, 0)),
            pl.BlockSpec((c, tn), lambda i, j: (0, j)),
            pl.BlockSpec((1, tn), lambda i, j: (0, j)),
            pl.BlockSpec((tm, tn), lambda i, j: (i, j)),
        ],
        out_specs=pl.BlockSpec((tm, tn), lambda i, j: (i, j)),
        out_shape=jax.ShapeDtypeStruct((n, d), F32),
        scratch_shapes=[pltpu.VMEM((2, tm, c), BF16), pltpu.VMEM((tm, c), BF16), pltpu.SemaphoreType.DMA((2,))],
        compiler_params=_params("arbitrary", "arbitrary"),
        name="ln_out",
    )(y, ln_g.reshape(1, c), ln_b.reshape(1, c), w_out, b_out.reshape(1, d), res)


def _split_bf16(a):
    hi = a.astype(BF16)
    lo = (a - hi.astype(F32)).astype(BF16)
    return hi, lo


def _router_kernel(h_ref, g_ref, w_ref, b_ref, ids_ref, wts_ref):
    t = _rms(h_ref[...], g_ref[...])
    t_hi, t_lo = _split_bf16(t)
    w_hi, w_lo = _split_bf16(w_ref[...])
    both = _bdot(t_hi, jnp.concatenate([w_hi, w_lo], axis=1))
    logits = both[:, :LANES] + both[:, LANES:] + _bdot(t_lo, w_hi) + b_ref[...]

    lane = lax.broadcasted_iota(I32, logits.shape, 1)
    lanef = lane.astype(F32)
    neg = -jnp.inf
    is_group = lane < N_GROUPS
    glog = jnp.where(is_group, logits, neg)
    gmax = jnp.max(glog, axis=1, keepdims=True)
    gi = jnp.min(jnp.where(glog == gmax, lanef, float(LANES)), axis=1, keepdims=True)
    gsum = jnp.sum(jnp.where(is_group, jnp.exp(logits - gmax), 0.0), axis=1, keepdims=True)
    gp = 1.0 / gsum

    lo_lane = N_GROUPS + gi * EXPERTS_PER_GROUP
    in_sel = (lanef >= lo_lane) & (lanef < lo_lane + EXPERTS_PER_GROUP)
    sel = jnp.where(in_sel, logits, neg)
    m1 = jnp.max(sel, axis=1, keepdims=True)
    i1 = jnp.min(jnp.where(sel == m1, lanef, float(LANES)), axis=1, keepdims=True)
    sel2 = jnp.where(lanef == i1, neg, sel)
    m2 = jnp.max(sel2, axis=1, keepdims=True)
    i2 = jnp.min(jnp.where(sel2 == m2, lanef, float(LANES)), axis=1, keepdims=True)
    e2 = jnp.exp(m2 - m1)
    w1 = gp / (1.0 + e2)
    w2 = gp * e2 / (1.0 + e2)

    ids = jnp.where(lane == 0, i1 - N_GROUPS, jnp.where(lane == 1, i2 - N_GROUPS, 0.0))
    ids_ref[...] = ids.astype(I32)
    wts_ref[...] = jnp.where(lane == 0, w1, jnp.where(lane == 1, w2, 0.0))


def _router(h, g, w_group, b_group, w_expert, b_expert, tm=512):
    n, d = h.shape
    w_e = jnp.transpose(w_expert, (1, 0, 2)).reshape(d, N_EXPERTS)
    pad = LANES - N_GROUPS - N_EXPERTS
    wr = jnp.concatenate([w_group, w_e, jnp.zeros((d, pad), F32)], axis=1)
    br = jnp.concatenate([b_group, b_expert.reshape(N_EXPERTS), jnp.zeros((pad,), F32)]).reshape(1, LANES)
    return pl.pallas_call(
        _router_kernel,
        grid=(n // tm,),
        in_specs=[
            pl.BlockSpec((tm, d), lambda i: (i, 0)),
            pl.BlockSpec((1, d), lambda i: (0, 0)),
            pl.BlockSpec((d, LANES), lambda i: (0, 0)),
            pl.BlockSpec((1, LANES), lambda i: (0, 0)),
        ],
        out_specs=[pl.BlockSpec((tm, LANES), lambda i: (i, 0)),
                   pl.BlockSpec((tm, LANES), lambda i: (i, 0))],
        out_shape=[jax.ShapeDtypeStruct((n, LANES), I32), jax.ShapeDtypeStruct((n, LANES), F32)],
        compiler_params=_params("arbitrary"),
        name="router",
    )(h, g.reshape(1, d), wr, br)


def _sorted_rows(n_assign, tm):
    return -(-(n_assign + N_EXPERTS * SUBLANES) // tm) * tm


def _dispatch_plan(ids, tm, n_items):
    none = N_EXPERTS
    experts = jnp.arange(N_EXPERTS, dtype=I32)
    upto = experts[:, None] <= experts[None, :]

    def prefix(v):
        return jnp.sum(jnp.where(upto, v[:, None], 0), axis=0)

    e_flat = jnp.concatenate([ids[:, 0], ids[:, 1]])
    onehot = (e_flat[:, None] == experts[None, :]).astype(I32)
    csum = jnp.cumsum(onehot, axis=0)
    counts = csum[-1]
    present = counts > 0
    n_rows = _sorted_rows(e_flat.shape[0], tm)
    aligned = (counts + SUBLANES - 1) // SUBLANES * SUBLANES
    last = jnp.max(jnp.where(present, experts, 0))
    padded = aligned + jnp.where(experts == last, n_rows - jnp.sum(aligned), 0)
    seg_ends = prefix(padded)
    starts = seg_ends - padded
    ends = starts + counts
    dest = jnp.sum(onehot * (csum - 1 + starts[None, :]), axis=1)

    gaps = padded - counts
    gap_ends = prefix(gaps)
    j = jnp.arange(n_rows - e_flat.shape[0], dtype=I32)
    gap_e = jnp.sum((gap_ends[None, :] <= j[:, None]).astype(I32), axis=1)
    first_gap = ends - (gap_ends - gaps)
    fill = j + jnp.sum(jnp.where(gap_e[:, None] == experts[None, :], first_gap[None, :], 0), axis=1)

    items = (padded + tm - 1) // tm
    item_end = prefix(items)
    item_start = item_end - items
    total = item_end[-1]

    ordinal = prefix(present.astype(I32)) - 1
    later = jnp.where(present[None, :] & (experts[None, :] > experts[:, None]), experts[None, :], none)
    nxt = jnp.min(later, axis=1)
    nxt2 = jnp.min(jnp.where(experts[None, :] == nxt[:, None], nxt[None, :], none), axis=1)
    ahead_e = jnp.where(nxt2 < none, nxt2, -1)
    head0 = jnp.min(jnp.where(present, experts, none))
    head1 = jnp.min(jnp.where(experts == head0, nxt, none))
    head = jnp.stack([head0, jnp.where(head1 < none, head1, -1)])

    w = jnp.arange(n_items, dtype=I32)
    valid = w < total
    wc = jnp.minimum(w, total - 1)
    e_w = jnp.sum((item_end[None, :] <= wc[:, None]).astype(I32), axis=1)
    table = jnp.stack([item_start, starts, seg_ends, ordinal, ahead_e], axis=1)
    mine = e_w[:, None] == experts[None, :]
    got = jnp.sum(jnp.where(mine[:, :, None], table[None, :, :], 0), axis=1)
    want = got[:, 1] + (wc - got[:, 0]) * tm
    start = jnp.minimum(want, n_rows - tm)
    lo = want - start
    rows = jnp.where(valid, jnp.minimum(got[:, 2] - want, tm), 0)
    prev_e = jnp.concatenate([jnp.full((1,), -1, I32), e_w[:-1]])
    new_e = valid & (e_w != prev_e)
    plan = (start, e_w, lo, rows, new_e, valid, got[:, 3] % 2, got[:, 4], head)
    return dest.astype(I32), tuple(p.astype(I32) for p in plan), fill.astype(I32)


def _scatter_kernel(dest_ref, fill_ref, h_ref, g_ref, xs_ref, t_ref, sem):
    tm = h_ref.shape[0]
    i = pl.program_id(0)
    steps = pl.num_programs(0)
    n = steps * tm
    buf = i % 2
    groups = tm // SUBLANES

    def row_copy(step, b, grp, j, pick):
        d = dest_ref[pick * n + step * tm + grp * SUBLANES + j]
        return pltpu.make_async_copy(t_ref.at[b, grp, pl.ds(j, 1), :], xs_ref.at[pl.ds(d, 1), :], sem.at[b])

    def for_group(step, b, grp, fn):
        for j in range(SUBLANES):
            fn(row_copy(step, b, grp, j, 0), 0)
            fn(row_copy(step, b, grp, j, 1), 1)

    def for_rows(step, b, fn):
        def body(grp, carry):
            for_group(step, b, grp, fn)
            return carry
        lax.fori_loop(0, groups, body, 0)

    @pl.when(i >= 2)
    def _():
        for_rows(i - 2, buf, lambda copy, pick: copy.wait())

    per = ROW_CHUNK // SUBLANES
    chunks = groups // per

    def norm(c):
        rows = pl.ds(pl.multiple_of(c * ROW_CHUNK, ROW_CHUNK), ROW_CHUNK)
        t = _rms(h_ref[rows, :], g_ref[...])
        t_ref[buf, pl.ds(pl.multiple_of(c * per, per), per)] = t.reshape(per, SUBLANES, t.shape[-1])

    def start(c):
        for k in range(per):
            for_group(i, buf, c * per + k, lambda copy, pick: copy.start(priority=pick))

    norm(0)

    def body(c, carry):
        start(c)
        norm(c + 1)
        return carry
    lax.fori_loop(0, chunks - 1, body, 0)
    start(chunks - 1)

    @pl.when(i == steps - 1)
    def _():
        def fill_copy(k):
            return pltpu.make_async_copy(t_ref.at[buf, 0, pl.ds(0, 1), :],
                                         xs_ref.at[pl.ds(fill_ref[k], 1), :], sem.at[2])

        def fill_start(k, carry):
            fill_copy(k).start()
            return carry
        lax.fori_loop(0, fill_ref.shape[0], fill_start, 0)

        def fill_wait(k, carry):
            fill_copy(k).wait()
            return carry
        lax.fori_loop(0, fill_ref.shape[0], fill_wait, 0)

        @pl.when(i >= 1)
        def _():
            for_rows(i - 1, 1 - buf, lambda copy, pick: copy.wait())
        for_rows(i, buf, lambda copy, pick: copy.wait())


def _scatter(h, g, dest, fill, n_rows, tm=256):
    n, d = h.shape
    return pl.pallas_call(
        _scatter_kernel,
        grid_spec=pltpu.PrefetchScalarGridSpec(
            num_scalar_prefetch=2,
            grid=(n // tm,),
            in_specs=[pl.BlockSpec((tm, d), lambda i, dest, fill: (i, 0)),
                      pl.BlockSpec((1, d), lambda i, dest, fill: (0, 0))],
            out_specs=pl.BlockSpec(memory_space=pl.ANY),
            scratch_shapes=[pltpu.VMEM((2, tm // SUBLANES, SUBLANES, d), F32),
                            pltpu.SemaphoreType.DMA((3,))],
        ),
        out_shape=jax.ShapeDtypeStruct((n_rows, d), F32),
        compiler_params=_params("arbitrary"),
        name="moe_scatter",
    )(dest, fill, h, g.reshape(1, d))


def _experts_kernel(start_ref, exp_ref, lo_ref, rows_ref, new_ref, valid_ref, slot_ref, ahead_ref, head_ref,
                    xs_hbm, wg_hbm, wu_hbm, wd_hbm, ys_hbm, x_ref, y_ref,
                    wg_st, wu_st, wd_st, wg_bf, wu_bf, wd_bf, sem_w, sem_x, sem_y, *, layer):
    w = pl.program_id(0)
    n_items = pl.num_programs(0)
    tm = x_ref.shape[1]
    buf = w % 2
    chunk = 256

    def window(item, b, fn):
        @pl.when(valid_ref[item] == 1)
        def _():
            rows = pl.ds(pl.multiple_of(start_ref[item], SUBLANES), tm)
            fn(pltpu.make_async_copy(xs_hbm.at[rows, :], x_ref.at[b], sem_x.at[b]))
    _prefetch_tile(w, n_items, window, priority=0)

    def fetch(e, s):
        return (pltpu.make_async_copy(wg_hbm.at[layer, e], wg_st.at[s], sem_w.at[s, 0]),
                pltpu.make_async_copy(wu_hbm.at[layer, e], wu_st.at[s], sem_w.at[s, 1]),
                pltpu.make_async_copy(wd_hbm.at[layer, e], wd_st.at[s], sem_w.at[s, 2]))

    @pl.when(w == 0)
    def _():
        for copy in fetch(head_ref[0], 0):
            copy.start(priority=BULK_DMA_PRIORITY)

        @pl.when(head_ref[1] >= 0)
        def _():
            for copy in fetch(head_ref[1], 1):
                copy.start(priority=BULK_DMA_PRIORITY)

    @pl.when(new_ref[w] == 1)
    def _():
        s = slot_ref[w]
        for copy in fetch(exp_ref[w], s):
            copy.wait()

        def cast_in(r0):
            rows = pl.ds(r0, chunk)
            wg_bf[rows, :] = wg_st[s, rows, :].astype(BF16)
            wu_bf[rows, :] = wu_st[s, rows, :].astype(BF16)
        _row_loop(wg_bf.shape[0], cast_in, chunk)

        def cast_down(r0):
            rows = pl.ds(r0, chunk)
            wd_bf[rows, :] = wd_st[s, rows, :].astype(BF16)
        _row_loop(wd_bf.shape[0], cast_down, chunk)

        @pl.when(ahead_ref[w] >= 0)
        def _():
            for copy in fetch(ahead_ref[w], s):
                copy.start(priority=BULK_DMA_PRIORITY)

    def put(item, b, fn):
        rows, lo, start = rows_ref[item], lo_ref[item], start_ref[item]
        size = tm
        while size >= SUBLANES:
            off = lo + (rows & ~(2 * size - 1))

            @pl.when((rows & size) != 0)
            def _(size=size, off=off):
                src = y_ref.at[b, pl.ds(pl.multiple_of(off, SUBLANES), size), :]
                dst = ys_hbm.at[pl.ds(pl.multiple_of(start + off, SUBLANES), size), :]
                fn(pltpu.make_async_copy(src, dst, sem_y.at[b]))
            size //= 2

    @pl.when(w >= 2)
    def _():
        put(w - 2, buf, lambda copy: copy.wait())

    @pl.when(valid_ref[w] == 1)
    def _():
        x = x_ref[buf].astype(BF16)
        hg = _bdot(x, wg_bf[...])
        hu = _bdot(x, wu_bf[...])
        act = (hg * jax.nn.sigmoid(hg) * hu).astype(BF16)
        y_ref[buf] = _bdot(act, wd_bf[...])
        put(w, buf, lambda copy: copy.start())

    @pl.when(w == n_items - 1)
    def _():
        @pl.when(w >= 1)
        def _():
            put(w - 1, 1 - buf, lambda copy: copy.wait())
        put(w, buf, lambda copy: copy.wait())


def _experts(xs, plan, w_gate, w_up, w_down, layer, n_rows, tm):
    d = xs.shape[1]
    f = w_gate.shape[3]
    n_items = plan[0].shape[0]
    any_space = pl.BlockSpec(memory_space=pl.ANY)
    return pl.pallas_call(
        functools.partial(_experts_kernel, layer=layer),
        grid_spec=pltpu.PrefetchScalarGridSpec(
            num_scalar_prefetch=len(plan),
            grid=(n_items,),
            in_specs=[any_space, any_space, any_space, any_space],
            out_specs=any_space,
            scratch_shapes=[
                pltpu.VMEM((2, tm, d), F32), pltpu.VMEM((2, tm, d), F32),
                pltpu.VMEM((2, d, f), F32), pltpu.VMEM((2, d, f), F32), pltpu.VMEM((2, f, d), F32),
                pltpu.VMEM((d, f), BF16), pltpu.VMEM((d, f), BF16), pltpu.VMEM((f, d), BF16),
                pltpu.SemaphoreType.DMA((2, 3)), pltpu.SemaphoreType.DMA((2,)), pltpu.SemaphoreType.DMA((2,)),
            ],
        ),
        out_shape=jax.ShapeDtypeStruct((n_rows, d), F32),
        compiler_params=_params("arbitrary"),
        name="moe_experts",
    )(*plan, xs, w_gate, w_up, w_down)


def _combine_kernel(dest_ref, h_ref, wts_ref, g_ref, ys_ref, o_ref, y_ref, sem, *, final_norm):
    tm = h_ref.shape[0]
    i = pl.program_id(0)
    steps = pl.num_programs(0)
    n = steps * tm
    buf = i % 2
    groups = tm // SUBLANES

    def row_copy(step, b, grp, j, pick):
        d = dest_ref[pick * n + step * tm + grp * SUBLANES + j]
        return pltpu.make_async_copy(ys_ref.at[pl.ds(d, 1), :], y_ref.at[b, pick, grp, pl.ds(j, 1), :],
                                     sem.at[b])

    def for_group(step, b, grp, fn):
        for j in range(SUBLANES):
            fn(row_copy(step, b, grp, j, 0), 0)
            fn(row_copy(step, b, grp, j, 1), 1)

    def start(copy, pick):
        copy.start(priority=pick)

    def for_rows(step, b, fn):
        def body(grp, carry):
            for_group(step, b, grp, fn)
            return carry
        lax.fori_loop(0, groups, body, 0)

    per = ROW_CHUNK // SUBLANES
    chunks = groups // per

    def mix(c):
        rows = pl.ds(pl.multiple_of(c * ROW_CHUNK, ROW_CHUNK), ROW_CHUNK)
        grps = pl.ds(pl.multiple_of(c * per, per), per)
        wts = wts_ref[rows, :]
        y0 = y_ref[buf, 0, grps].reshape(ROW_CHUNK, -1)
        y1 = y_ref[buf, 1, grps].reshape(ROW_CHUNK, -1)
        out = h_ref[rows, :] + wts[:, 0:1] * y0 + wts[:, 1:2] * y1
        if final_norm:
            out = _rms(out, g_ref[...])
        o_ref[rows, :] = out

    @pl.when(i == 0)
    def _():
        for_rows(0, 0, start)

    for_rows(i, buf, lambda copy, pick: copy.wait())

    @pl.when(i + 1 < steps)
    def _():
        span = 2 if final_norm else 1

        def body(cc, carry):
            for k in range(span * per):
                for_group(i + 1, 1 - buf, cc * span * per + k, start)
            for k in range(span):
                mix(cc * span + k)
            return carry
        lax.fori_loop(0, chunks // span, body, 0)

    @pl.when(i + 1 == steps)
    def _():
        def body(c, carry):
            mix(c)
            return carry
        lax.fori_loop(0, chunks, body, 0, unroll=ROW_UNROLL)


def _combine(h, wts, ys, dest, g_final, final_norm, tm=256):
    n, d = h.shape
    return pl.pallas_call(
        functools.partial(_combine_kernel, final_norm=final_norm),
        grid_spec=pltpu.PrefetchScalarGridSpec(
            num_scalar_prefetch=1,
            grid=(n // tm,),
            in_specs=[pl.BlockSpec((tm, d), lambda i, dest: (i, 0)),
                      pl.BlockSpec((tm, LANES), lambda i, dest: (i, 0)),
                      pl.BlockSpec((1, d), lambda i, dest: (0, 0)),
                      pl.BlockSpec(memory_space=pl.ANY)],
            out_specs=pl.BlockSpec((tm, d), lambda i, dest: (i, 0)),
            scratch_shapes=[pltpu.VMEM((2, 2, tm // SUBLANES, SUBLANES, d), F32),
                            pltpu.SemaphoreType.DMA((2,))],
        ),
        out_shape=jax.ShapeDtypeStruct((n, d), F32),
        compiler_params=_params("arbitrary"),
        name="moe_combine",
    )(dest, h, wts, g_final.reshape(1, d), ys)


def _moe(h, g, w_group, b_group, w_expert, b_expert, w_gate, w_up, w_down, layer, g_final, final_norm,
         tm=256):
    n = h.shape[0]
    ids, wts = _router(h, g, w_group, b_group, w_expert, b_expert)
    n_rows = _sorted_rows(2 * n, tm)
    n_items = n_rows // tm + N_EXPERTS
    dest, plan, fill = _dispatch_plan(ids, tm, n_items)
    xs = _scatter(h, g, dest, fill, n_rows)
    ys = _experts(xs, plan, w_gate, w_up, w_down, layer, n_rows, tm)
    return _combine(h, wts, ys, dest, g_final, final_norm)


def _dot_t(a, b):
    return lax.dot_general(a, b, (((1,), (1,)), ((), ())), preferred_element_type=F32)


def _lane_slab(columns):
    lane = lax.broadcasted_iota(I32, (columns[0].shape[0], LANES), 1)
    slab = jnp.zeros(lane.shape, F32)
    for h, col in enumerate(columns):
        slab = jnp.where(lane == h, col, slab)
    return slab


def _attn_kernel(q_ref, kc_ref, vc_ref, kp_ref, vp_ref, o_ref, lse_ref, *, seg_blocks):
    qblocks = q_ref.shape[0] // BAND
    step = pl.program_id(0)
    banded = seg_blocks > 1
    assert not banded or seg_blocks % qblocks == 0
    n_keys = 2 * BAND if banded else BAND
    qi = lax.broadcasted_iota(I32, (BAND, n_keys), 0)
    kj = lax.broadcasted_iota(I32, (BAND, n_keys), 1)
    mask = (kj >= qi) & (kj <= qi + BAND) if banded else kj <= qi
    scale = HEAD_DIM ** -0.5
    neg = -jnp.inf
    heads = [slice(h * HEAD_DIM, (h + 1) * HEAD_DIM) for h in range(HEADS)]

    for sb in range(qblocks):
        rows = slice(sb * BAND, (sb + 1) * BAND)
        if not banded:
            scores = [_dot_t(q_ref[rows, c], kc_ref[rows, c]) for c in heads]
        elif sb > 0:
            krows = slice((sb - 1) * BAND, (sb + 1) * BAND)
            scores = [_dot_t(q_ref[rows, c], kc_ref[krows, c]) for c in heads]
        else:
            scores = [jnp.concatenate([_dot_t(q_ref[rows, c], kp_ref[:, c]),
                                       _dot_t(q_ref[rows, c], kc_ref[rows, c])], axis=1) for c in heads]
        scores = [jnp.where(mask, s * scale, neg) for s in scores]
        if banded and sb == 0:
            at_start = (step * qblocks) % seg_blocks == 0
            drop = jnp.where(kj < BAND, jnp.where(at_start, neg, 0.0), 0.0)
            scores = [s + drop for s in scores]
        ms = [jnp.max(s, axis=1, keepdims=True) for s in scores]
        ps = [jnp.exp(s - m) for s, m in zip(scores, ms)]
        ls = [jnp.sum(p, axis=1, keepdims=True) for p in ps]
        pb = [p.astype(BF16) for p in ps]
        if not banded:
            accs = [_bdot(p, vc_ref[rows, c]) for p, c in zip(pb, heads)]
        elif sb > 0:
            accs = [_bdot(p, vc_ref[krows, c]) for p, c in zip(pb, heads)]
        else:
            accs = [_bdot(p[:, :BAND], vp_ref[:, c]) + _bdot(p[:, BAND:], vc_ref[rows, c])
                    for p, c in zip(pb, heads)]
        for acc, l, c in zip(accs, ls, heads):
            o_ref[rows, c] = (acc / l).astype(o_ref.dtype)
        lse_ref[rows, :] = _lane_slab([m + jnp.log(l) for m, l in zip(ms, ls)])


def _attn(qkv, cols, seg_blocks, out_dtype, name, rows_per_step=512):
    n = qkv.shape[0]
    cq, ck, cv = cols
    qblocks = rows_per_step // BAND
    return pl.pallas_call(
        functools.partial(_attn_kernel, seg_blocks=seg_blocks),
        grid=(n // rows_per_step,),
        in_specs=[
            pl.BlockSpec((rows_per_step, ATTN_OUT), lambda i: (i, cq)),
            pl.BlockSpec((rows_per_step, ATTN_OUT), lambda i: (i, ck)),
            pl.BlockSpec((rows_per_step, ATTN_OUT), lambda i: (i, cv)),
            pl.BlockSpec((BAND, ATTN_OUT), lambda i: (jnp.maximum(i * qblocks - 1, 0), ck)),
            pl.BlockSpec((BAND, ATTN_OUT), lambda i: (jnp.maximum(i * qblocks - 1, 0), cv)),
        ],
        out_specs=[pl.BlockSpec((rows_per_step, ATTN_OUT), lambda i: (i, 0)),
                   pl.BlockSpec((rows_per_step, LANES), lambda i: (i, 0))],
        out_shape=[jax.ShapeDtypeStruct((n, ATTN_OUT), out_dtype), jax.ShapeDtypeStruct((n, LANES), F32)],
        compiler_params=_params("arbitrary"),
        name=name,
    )(qkv, qkv, qkv, qkv, qkv)


def _attn_skew_kernel(q_ref, k_ref, v_ref, o_ref, lse_ref):
    subs = q_ref.shape[0] // BAND
    keys = k_ref.shape[0]
    u = lax.broadcasted_iota(I32, (BAND, keys), 0)
    kc = lax.broadcasted_iota(I32, (BAND, keys), 1)
    base = subs * (u - (kc & (BAND - 1))) - lax.shift_right_logical(kc, BAND_LOG2)
    scale = HEAD_DIM ** -0.5
    neg = -jnp.inf
    heads = [slice(h * HEAD_DIM, (h + 1) * HEAD_DIM) for h in range(HEADS)]
    for s in range(subs):
        rows = slice(s * BAND, (s + 1) * BAND)
        delta = base + s
        mask = (delta >= 0) & (delta <= BAND)
        scores = [jnp.where(mask, _dot_t(q_ref[rows, c], k_ref[:, c]) * scale, neg) for c in heads]
        ms = [jnp.max(sc, axis=1, keepdims=True) for sc in scores]
        ps = [jnp.exp(sc - m) for sc, m in zip(scores, ms)]
        ls = [jnp.sum(p, axis=1, keepdims=True) for p in ps]
        accs = [_bdot(p.astype(BF16), v_ref[:, c]) for p, c in zip(ps, heads)]
        for acc, l, c in zip(accs, ls, heads):
            o_ref[rows, c] = (acc / l).astype(o_ref.dtype)
        lse_ref[rows, :] = _lane_slab([m + jnp.log(l) for m, l in zip(ms, ls)])


def _attn_skew(qkv, cols, out_dtype, name):
    n = qkv.shape[0]
    cq, ck, cv = cols
    rows = (ATTN_DILATIONS[2] // ATTN_DILATIONS[1]) * BAND
    return pl.pallas_call(
        _attn_skew_kernel,
        grid=(n // rows,),
        in_specs=[pl.BlockSpec((rows, ATTN_OUT), lambda i: (i, cq)),
                  pl.BlockSpec((rows, ATTN_OUT), lambda i: (i, ck)),
                  pl.BlockSpec((rows, ATTN_OUT), lambda i: (i, cv))],
        out_specs=[pl.BlockSpec((rows, ATTN_OUT), lambda i: (i, 0)),
                   pl.BlockSpec((rows, LANES), lambda i: (i, 0))],
        out_shape=[jax.ShapeDtypeStruct((n, ATTN_OUT), out_dtype), jax.ShapeDtypeStruct((n, LANES), F32)],
        compiler_params=_params("arbitrary"),
        name=name,
    )(qkv, qkv, qkv)


def _slot_class(c):
    return (c % 4) * 4 + c // 4


def _qkv_kernel(h_hbm, gq_ref, gkv_ref, wq_ref, wkv_ref, o_ref, x_ref, xq_ref, xkv_ref, sem,
                *, tiles_per_batch, n_groups):
    i, j = pl.program_id(0), pl.program_id(1)
    tiles = pl.num_programs(0)
    tm = x_ref.shape[1]
    q_blocks = n_groups * ATTN_OUT // o_ref.shape[1]

    def fetch(tile, buf, fn):
        if n_groups == 1:
            fn(pltpu.make_async_copy(h_hbm.at[pl.ds(tile * tm, tm), :], x_ref.at[buf], sem.at[buf]))
            return
        per_class = h_hbm.shape[0] // (tiles // tiles_per_batch)
        slots = tm // per_class
        bi, ti = tile // tiles_per_batch, tile % tiles_per_batch
        for c in range(slots):
            cls = _slot_class(ti * slots + c)
            fn(pltpu.make_async_copy(h_hbm.at[pl.ds(bi * per_class, per_class), cls, :],
                                     x_ref.at[buf, pl.ds(c * per_class, per_class), :],
                                     sem.at[buf]))

    @pl.when(j == 0)
    def _():
        buf = i % 2
        _prefetch_tile(i, tiles, fetch)

        def norm(r0):
            x = x_ref[buf, pl.ds(r0, ROW_CHUNK), :]
            xhat = x * lax.rsqrt(jnp.mean(x * x, axis=-1, keepdims=True) + EPS)
            rows = pl.ds(r0, ROW_CHUNK)
            xq_ref[rows, :] = (xhat * gq_ref[...]).astype(BF16)
            xkv_ref[rows, :] = (xhat * gkv_ref[...]).astype(BF16)
        _row_loop(tm, norm)

    @pl.when(j < q_blocks)
    def _():
        o_ref[...] = _bdot(xq_ref[...], wq_ref[...].astype(BF16)).astype(o_ref.dtype)

    @pl.when(j >= q_blocks)
    def _():
        o_ref[...] = _bdot(xkv_ref[...], wkv_ref[...].astype(BF16)).astype(o_ref.dtype)


def _qkv(h, batch, g_q, g_kv, w_q, w_kv, dilated, tm=1024, tn=512):
    n, d = h.shape
    first, n_groups = (1, N_DILATED) if dilated else (0, 1)
    r2 = ATTN_DILATIONS[2]
    hv = h.reshape(n // r2, r2, d) if dilated else h
    per = ATTN_OUT // tn
    q_blocks = n_groups * per
    kv_blocks = 2 * n_groups * per

    def wq_map(i, j):
        return (0, first * per + jnp.minimum(j, q_blocks - 1))

    def wkv_map(i, j):
        jj = jnp.clip(j - q_blocks, 0, kv_blocks - 1)
        return (0, first * per + jj + jnp.where(jj >= q_blocks, (N_ATTN_GROUPS - n_groups) * per, 0))

    return pl.pallas_call(
        functools.partial(_qkv_kernel, tiles_per_batch=(n // batch) // tm, n_groups=n_groups),
        grid=(n // tm, q_blocks + kv_blocks),
        in_specs=[
            pl.BlockSpec(memory_space=pl.ANY),
            pl.BlockSpec((1, d), lambda i, j: (0, 0)),
            pl.BlockSpec((1, d), lambda i, j: (0, 0)),
            pl.BlockSpec((d, tn), wq_map),
            pl.BlockSpec((d, tn), wkv_map),
        ],
        out_specs=pl.BlockSpec((tm, tn), lambda i, j: (i, j)),
        out_shape=jax.ShapeDtypeStruct((n, 3 * n_groups * ATTN_OUT), BF16),
        scratch_shapes=[pltpu.VMEM((2, tm, d), F32), pltpu.VMEM((tm, d), BF16), pltpu.VMEM((tm, d), BF16),
                        pltpu.SemaphoreType.DMA((2,))],
        compiler_params=_params("arbitrary", "arbitrary"),
        name="qkv_dilated" if dilated else "qkv_g0",
    )(hv, g_q.reshape(1, d), g_kv.reshape(1, d), w_q, w_kv)


def _merge_out_kernel(o0_ref, l0_ref, h_ref, w_ref, o1_hbm, l1_hbm, o2_hbm, l2_hbm, out_ref,
                      ob_ref, lb_ref, m_ref, sem, *, tiles_per_batch):
    t = pl.program_id(0)
    tiles = pl.num_programs(0)
    buf = t % 2
    per = ob_ref.shape[2]
    n_cls = ob_ref.shape[3]
    rows = per * n_cls

    def fetch(tile, b, fn):
        bi, ti = tile // tiles_per_batch, tile % tiles_per_batch
        for c in range(n_cls):
            cls = _slot_class(c)
            src_rows = pl.ds(ti * per, per)
            for g, (o_hbm, l_hbm) in enumerate(((o1_hbm, l1_hbm), (o2_hbm, l2_hbm))):
                fn(pltpu.make_async_copy(o_hbm.at[bi * n_cls + c, src_rows, :],
                                         ob_ref.at[b, g, :, cls, :], sem.at[b, 0]))
                fn(pltpu.make_async_copy(l_hbm.at[bi * n_cls + c, src_rows, :],
                                         lb_ref.at[b, g, :, cls, :], sem.at[b, 1]))

    _prefetch_tile(t, tiles, fetch)

    head_of_col = lax.shift_right_logical(lax.broadcasted_iota(I32, (LANES, ATTN_OUT), 1), HEAD_DIM_LOG2)
    spread = jnp.where(head_of_col == lax.broadcasted_iota(I32, (LANES, ATTN_OUT), 0), 1.0, 0.0).astype(BF16)

    chunk = BAND
    for r0 in range(0, rows, chunk):
        rs = slice(r0, r0 + chunk)
        gs = slice(r0 // n_cls, (r0 + chunk) // n_cls)
        lses = (l0_ref[rs, :],
                lb_ref[buf, 0, gs, :, :].reshape(chunk, LANES),
                lb_ref[buf, 1, gs, :, :].reshape(chunk, LANES))
        o1 = ob_ref[buf, 0, gs, :, :].reshape(chunk, ATTN_OUT)
        o2 = ob_ref[buf, 1, gs, :, :].reshape(chunk, ATTN_OUT)
        mx = jnp.maximum(jnp.maximum(lses[0], lses[1]), lses[2])
        ex = [jnp.exp(l - mx) for l in lses]
        inv = 1.0 / (ex[0] + ex[1] + ex[2])
        wide = []
        for e in ex:
            hi, lo = _split_bf16(e * inv)
            wide.append(_bdot(hi, spread) + _bdot(lo, spread))
        merged = wide[0] * o0_ref[rs, :].astype(F32) + wide[1] * o1 + wide[2] * o2
        m_ref[rs, :] = merged.astype(BF16)
    out_ref[...] = h_ref[...] + _bdot(m_ref[...], w_ref[...])


def _merge_out(h, seq, g0, g1, g2, w_o, rows=512):
    n, d = h.shape
    n_cls = ATTN_DILATIONS[2]
    per = rows // n_cls
    cls_len = seq // n_cls
    (o0, l0), (o1, l1), (o2, l2) = g0, g1, g2

    def by_class(a):
        return a.reshape(n // cls_len, cls_len, a.shape[-1])

    return pl.pallas_call(
        functools.partial(_merge_out_kernel, tiles_per_batch=seq // rows),
        grid=(n // rows,),
        in_specs=[
            pl.BlockSpec((rows, ATTN_OUT), lambda t: (t, 0)),
            pl.BlockSpec((rows, LANES), lambda t: (t, 0)),
            pl.BlockSpec((rows, d), lambda t: (t, 0)),
            pl.BlockSpec((ATTN_OUT, d), lambda t: (0, 0)),
            pl.BlockSpec(memory_space=pl.ANY),
            pl.BlockSpec(memory_space=pl.ANY),
            pl.BlockSpec(memory_space=pl.ANY),
            pl.BlockSpec(memory_space=pl.ANY),
        ],
        out_specs=pl.BlockSpec((rows, d), lambda t: (t, 0)),
        out_shape=jax.ShapeDtypeStruct((n, d), F32),
        scratch_shapes=[pltpu.VMEM((2, N_DILATED, per, n_cls, ATTN_OUT), F32),
                        pltpu.VMEM((2, N_DILATED, per, n_cls, LANES), F32),
                        pltpu.VMEM((rows, ATTN_OUT), BF16),
                        pltpu.SemaphoreType.DMA((2, 2))],
        compiler_params=_params("arbitrary"),
        name="attn_merge_out",
    )(o0, l0, h, w_o.astype(BF16), by_class(o1), by_class(l1), by_class(o2), by_class(l2))


def kernel(x, norm_mix_g, norm_ffn_g, conv_w_in, conv_b_in, conv_w_dw, conv_b_dw, conv_ln_g, conv_ln_b,
           conv_w_out, conv_b_out, norm_kv_g, w_kv, attn_w_q, attn_w_o, router_w_group, router_b_group,
           router_w_expert, router_b_expert, expert_w_gate, expert_w_up, expert_w_down, norm_final_g):
    b, s, d = x.shape
    n = b * s
    xf = x.reshape(n, d)

    u = _glu_in(xf, norm_mix_g[0], conv_w_in[0], conv_b_in[0])
    y = _dwconv(u.reshape(b, s, -1), conv_w_dw[0], conv_b_dw[0])
    h = _ln_out(y.reshape(n, -1), conv_ln_g[0], conv_ln_b[0], conv_w_out[0], conv_b_out[0], xf)
    h = _moe(h, norm_ffn_g[0], router_w_group[0], router_b_group[0], router_w_expert[0],
             router_b_expert[0], expert_w_gate, expert_w_up, expert_w_down, 0,
             norm_final_g, False)

    qkv0 = _qkv(h, b, norm_mix_g[1], norm_kv_g, attn_w_q[0], w_kv, dilated=False)
    qkvd = _qkv(h, b, norm_mix_g[1], norm_kv_g, attn_w_q[0], w_kv, dilated=True)
    g0 = _attn(qkv0, (0, 1, 2), s // BAND, BF16, "attn_g0")
    g1 = _attn_skew(qkvd, (0, 2, 4), F32, "attn_g1")
    g2 = _attn(qkvd, (1, 3, 5), 1, F32, "attn_g2")
    h = _merge_out(h, s, g0, g1, g2, attn_w_o[0])
    out = _moe(h, norm_ffn_g[1], router_w_group[1], router_b_group[1], router_w_expert[1],
               router_b_expert[1], expert_w_gate, expert_w_up, expert_w_down, 1,
               norm_final_g, True)
    return out.reshape(b, s, d)
```

```python
import functools

import jax
import jax.numpy as jnp
from jax import lax
from jax.experimental import pallas as pl
from jax.experimental.pallas import tpu as pltpu

F32 = jnp.float32
BF16 = jnp.bfloat16
I32 = jnp.int32

EPS = 1e-6
LANES = 128
V7X_VMEM_BYTES = 64 * 1024 * 1024
VMEM_LIMIT = V7X_VMEM_BYTES - 8 * 1024 * 1024

CONV_WIDTH = 31
HALO = 32
ATTN_DILATIONS = (1, 4, 16)
N_ATTN_GROUPS = 3
N_DILATED = 2
HEADS = 8
HEAD_DIM = 128
HEAD_DIM_LOG2 = 7
BAND = 128
BAND_LOG2 = 7
ATTN_OUT = HEADS * HEAD_DIM
N_GROUPS = 4
EXPERTS_PER_GROUP = 8
N_EXPERTS = N_GROUPS * EXPERTS_PER_GROUP

ROW_CHUNK = 32
ROW_UNROLL = 4
SUBLANES = 8
BULK_DMA_PRIORITY = 1


def _params(*sem):
    return pltpu.CompilerParams(dimension_semantics=sem, vmem_limit_bytes=VMEM_LIMIT)


def _row_loop(n_rows, body, chunk=ROW_CHUNK, unroll=ROW_UNROLL):
    def step(c, carry):
        body(pl.multiple_of(c * chunk, chunk))
        return carry
    lax.fori_loop(0, n_rows // chunk, step, 0, unroll=unroll)


def _rms(x, g):
    ms = jnp.mean(x * x, axis=-1, keepdims=True)
    return x * lax.rsqrt(ms + EPS) * g


def _bdot(a, b):
    return jnp.dot(a, b, preferred_element_type=F32)


def _prefetch_tile(i, n_tiles, fetch, priority=BULK_DMA_PRIORITY):
    buf = i % 2

    def start(copy):
        copy.start(priority=priority)

    @pl.when(i == 0)
    def _():
        fetch(0, 0, start)

    @pl.when(i + 1 < n_tiles)
    def _():
        fetch(i + 1, 1 - buf, start)

    fetch(i, buf, lambda copy: copy.wait())


def _glu_in_kernel(x_hbm, g_ref, wv_ref, wg_ref, bv_ref, bg_ref, o_ref, x_ref, xn_ref, sem):
    tm = x_ref.shape[1]
    i = pl.program_id(0)

    @pl.when(pl.program_id(1) == 0)
    def _():
        def fetch(tile, buf, fn):
            fn(pltpu.make_async_copy(x_hbm.at[pl.ds(tile * tm, tm), :], x_ref.at[buf], sem.at[buf]))
        _prefetch_tile(i, pl.num_programs(0), fetch)

        def norm(r0):
            rows = pl.ds(r0, ROW_CHUNK)
            xn_ref[rows, :] = _rms(x_ref[i % 2, rows, :], g_ref[...]).astype(BF16)
        _row_loop(tm, norm)

    xn = xn_ref[...]
    val = _bdot(xn, wv_ref[...].astype(BF16)) + bv_ref[...]
    gate = _bdot(xn, wg_ref[...].astype(BF16)) + bg_ref[...]
    o_ref[...] = (val * jax.nn.sigmoid(gate)).astype(o_ref.dtype)


def _glu_in(x, g, w_in, b_in, tm=1024, tn=512):
    n, d = x.shape
    c = w_in.shape[1] // 2
    nj = c // tn
    b2 = b_in.reshape(1, 2 * c)
    return pl.pallas_call(
        _glu_in_kernel,
        grid=(n // tm, nj),
        in_specs=[
            pl.BlockSpec(memory_space=pl.ANY),
            pl.BlockSpec((1, d), lambda i, j: (0, 0)),
            pl.BlockSpec((d, tn), lambda i, j: (0, j)),
            pl.BlockSpec((d, tn), lambda i, j: (0, j + nj)),
            pl.BlockSpec((1, tn), lambda i, j: (0, j)),
            pl.BlockSpec((1, tn), lambda i, j: (0, j + nj)),
        ],
        out_specs=pl.BlockSpec((tm, tn), lambda i, j: (i, j)),
        out_shape=jax.ShapeDtypeStruct((n, c), BF16),
        scratch_shapes=[pltpu.VMEM((2, tm, d), F32), pltpu.VMEM((tm, d), BF16), pltpu.SemaphoreType.DMA((2,))],
        compiler_params=_params("arbitrary", "arbitrary"),
        name="glu_in",
    )(x, g.reshape(1, d), w_in, w_in, b2, b2)


def _dwconv_kernel(cur_ref, halo_ref, w_ref, b_ref, o_ref, buf_ref):
    ts, cw = cur_ref.shape[1], cur_ref.shape[2]
    rw = 64
    keep = jnp.where(pl.program_id(1) > 0, 1.0, 0.0)
    buf_ref[0, 0:HALO, :] = halo_ref[0].astype(F32) * keep
    buf_ref[0, HALO:, :] = cur_ref[0].astype(F32)
    shifted_rows = ts + HALO - SUBLANES
    for s in range(1, SUBLANES):
        buf_ref[s, 0:shifted_rows, :] = buf_ref[0, s:s + shifted_rows, :]
    first = HALO - (CONV_WIDTH - 1)
    for c0 in range(0, cw, LANES):
        cols = slice(c0, c0 + LANES)
        for r0 in range(0, ts, rw):
            acc = jnp.broadcast_to(b_ref[:, cols], (rw, LANES))
            for k in range(CONV_WIDTH):
                s = (first + k) % SUBLANES
                start = r0 + first + k - s
                acc = acc + w_ref[k:k + 1, cols] * buf_ref[s, start:start + rw, cols]
            o_ref[0, r0:r0 + rw, cols] = acc.astype(o_ref.dtype)


def _dwconv(u, w_dw, b_dw, ts=256, cw=512):
    b, s, c = u.shape
    hb = ts // HALO
    return pl.pallas_call(
        _dwconv_kernel,
        grid=(b, s // ts, c // cw),
        in_specs=[
            pl.BlockSpec((1, ts, cw), lambda bi, si, ci: (bi, si, ci)),
            pl.BlockSpec((1, HALO, cw), lambda bi, si, ci: (bi, jnp.maximum(si * hb - 1, 0), ci)),
            pl.BlockSpec((CONV_WIDTH, cw), lambda bi, si, ci: (0, ci)),
            pl.BlockSpec((1, cw), lambda bi, si, ci: (0, ci)),
        ],
        out_specs=pl.BlockSpec((1, ts, cw), lambda bi, si, ci: (bi, si, ci)),
        out_shape=jax.ShapeDtypeStruct((b, s, c), BF16),
        scratch_shapes=[pltpu.VMEM((SUBLANES, ts + HALO, cw), F32)],
        compiler_params=_params("arbitrary", "arbitrary", "arbitrary"),
        name="dwconv",
    )(u, u, w_dw, b_dw.reshape(1, c))


def _ln_out_kernel(y_hbm, lg_ref, lb_ref, w_ref, b_ref, res_ref, o_ref, y_ref, a_ref, sem):
    tm = y_ref.shape[1]
    i = pl.program_id(0)

    @pl.when(pl.program_id(1) == 0)
    def _():
        def fetch(tile, buf, fn):
            fn(pltpu.make_async_copy(y_hbm.at[pl.ds(tile * tm, tm), :], y_ref.at[buf], sem.at[buf]))
        _prefetch_tile(i, pl.num_programs(0), fetch)

        def norm(r0):
            rows = pl.ds(r0, ROW_CHUNK)
            y = y_ref[i % 2, rows, :].astype(F32)
            mu = jnp.mean(y, axis=-1, keepdims=True)
            yc = y - mu
            var = jnp.mean(yc * yc, axis=-1, keepdims=True)
            z = yc * lax.rsqrt(var + EPS) * lg_ref[...] + lb_ref[...]
            a_ref[rows, :] = (z * jax.nn.sigmoid(z)).astype(BF16)
        _row_loop(tm, norm)

    o_ref[...] = res_ref[...] + _bdot(a_ref[...], w_ref[...].astype(BF16)) + b_ref[...]


def _ln_out(y, ln_g, ln_b, w_out, b_out, res, tm=1024, tn=1024):
    n, c = y.shape
    d = w_out.shape[1]
    return pl.pallas_call(
        _ln_out_kernel,
        grid=(n // tm, d // tn),
        in_specs=[
            pl.BlockSpec(memory_space=pl.ANY),
            pl.BlockSpec((1, c), lambda i, j: (0, 0)),
            pl.BlockSpec((1, c), lambda i, j: (0, 0)),
            pl.BlockSpec((c, tn), lambda i, j: (0, j)),
            pl.BlockSpec((1, tn), lambda i, j: (0, j)),
            pl.BlockSpec((tm, tn), lambda i, j: (i, j)),
        ],
        out_specs=pl.BlockSpec((tm, tn), lambda i, j: (i, j)),
        out_shape=jax.ShapeDtypeStruct((n, d), F32),
        scratch_shapes=[pltpu.VMEM((2, tm, c), BF16), pltpu.VMEM((tm, c), BF16), pltpu.SemaphoreType.DMA((2,))],
        compiler_params=_params("arbitrary", "arbitrary"),
        name="ln_out",
    )(y, ln_g.reshape(1, c), ln_b.reshape(1, c), w_out, b_out.reshape(1, d), res)


def _split_bf16(a):
    hi = a.astype(BF16)
    lo = (a - hi.astype(F32)).astype(BF16)
    return hi, lo


def _router_kernel(h_ref, g_ref, w_ref, b_ref, ids_ref, wts_ref):
    t = _rms(h_ref[...], g_ref[...])
    t_hi, t_lo = _split_bf16(t)
    w_hi, w_lo = _split_bf16(w_ref[...])
    both = _bdot(t_hi, jnp.concatenate([w_hi, w_lo], axis=1))
    logits = both[:, :LANES] + both[:, LANES:] + _bdot(t_lo, w_hi) + b_ref[...]

    lane = lax.broadcasted_iota(I32, logits.shape, 1)
    lanef = lane.astype(F32)
    neg = -jnp.inf
    is_group = lane < N_GROUPS
    glog = jnp.where(is_group, logits, neg)
    gmax = jnp.max(glog, axis=1, keepdims=True)
    gi = jnp.min(jnp.where(glog == gmax, lanef, float(LANES)), axis=1, keepdims=True)
    gsum = jnp.sum(jnp.where(is_group, jnp.exp(logits - gmax), 0.0), axis=1, keepdims=True)
    gp = 1.0 / gsum

    lo_lane = N_GROUPS + gi * EXPERTS_PER_GROUP
    in_sel = (lanef >= lo_lane) & (lanef < lo_lane + EXPERTS_PER_GROUP)
    sel = jnp.where(in_sel, logits, neg)
    m1 = jnp.max(sel, axis=1, keepdims=True)
    i1 = jnp.min(jnp.where(sel == m1, lanef, float(LANES)), axis=1, keepdims=True)
    sel2 = jnp.where(lanef == i1, neg, sel)
    m2 = jnp.max(sel2, axis=1, keepdims=True)
    i2 = jnp.min(jnp.where(sel2 == m2, lanef, float(LANES)), axis=1, keepdims=True)
    e2 = jnp.exp(m2 - m1)
    w1 = gp / (1.0 + e2)
    w2 = gp * e2 / (1.0 + e2)

    ids = jnp.where(lane == 0, i1 - N_GROUPS, jnp.where(lane == 1, i2 - N_GROUPS, 0.0))
    ids_ref[...] = ids.astype(I32)
    wts_ref[...] = jnp.where(lane == 0, w1, jnp.where(lane == 1, w2, 0.0))


def _router(h, g, w_group, b_group, w_expert, b_expert, tm=512):
    n, d = h.shape
    w_e = jnp.transpose(w_expert, (1, 0, 2)).reshape(d, N_EXPERTS)
    pad = LANES - N_GROUPS - N_EXPERTS
    wr = jnp.concatenate([w_group, w_e, jnp.zeros((d, pad), F32)], axis=1)
    br = jnp.concatenate([b_group, b_expert.reshape(N_EXPERTS), jnp.zeros((pad,), F32)]).reshape(1, LANES)
    return pl.pallas_call(
        _router_kernel,
        grid=(n // tm,),
        in_specs=[
            pl.BlockSpec((tm, d), lambda i: (i, 0)),
            pl.BlockSpec((1, d), lambda i: (0, 0)),
            pl.BlockSpec((d, LANES), lambda i: (0, 0)),
            pl.BlockSpec((1, LANES), lambda i: (0, 0)),
        ],
        out_specs=[pl.BlockSpec((tm, LANES), lambda i: (i, 0)),
                   pl.BlockSpec((tm, LANES), lambda i: (i, 0))],
        out_shape=[jax.ShapeDtypeStruct((n, LANES), I32), jax.ShapeDtypeStruct((n, LANES), F32)],
        compiler_params=_params("arbitrary"),
        name="router",
    )(h, g.reshape(1, d), wr, br)


def _sorted_rows(n_assign, tm):
    return -(-(n_assign + N_EXPERTS * SUBLANES) // tm) * tm


def _dispatch_plan(ids, tm, n_items):
    none = N_EXPERTS
    experts = jnp.arange(N_EXPERTS, dtype=I32)
    upto = experts[:, None] <= experts[None, :]

    def prefix(v):
        return jnp.sum(jnp.where(upto, v[:, None], 0), axis=0)

    e_flat = jnp.concatenate([ids[:, 0], ids[:, 1]])
    onehot = (e_flat[:, None] == experts[None, :]).astype(I32)
    csum = jnp.cumsum(onehot, axis=0)
    counts = csum[-1]
    present = counts > 0
    n_rows = _sorted_rows(e_flat.shape[0], tm)
    aligned = (counts + SUBLANES - 1) // SUBLANES * SUBLANES
    last = jnp.max(jnp.where(present, experts, 0))
    padded = aligned + jnp.where(experts == last, n_rows - jnp.sum(aligned), 0)
    seg_ends = prefix(padded)
    starts = seg_ends - padded
    ends = starts + counts
    dest = jnp.sum(onehot * (csum - 1 + starts[None, :]), axis=1)

    gaps = padded - counts
    gap_ends = prefix(gaps)
    j = jnp.arange(n_rows - e_flat.shape[0], dtype=I32)
    gap_e = jnp.sum((gap_ends[None, :] <= j[:, None]).astype(I32), axis=1)
    first_gap = ends - (gap_ends - gaps)
    fill = j + jnp.sum(jnp.where(gap_e[:, None] == experts[None, :], first_gap[None, :], 0), axis=1)

    items = (padded + tm - 1) // tm
    item_end = prefix(items)
    item_start = item_end - items
    total = item_end[-1]

    ordinal = prefix(present.astype(I32)) - 1
    later = jnp.where(present[None, :] & (experts[None, :] > experts[:, None]), experts[None, :], none)
    nxt = jnp.min(later, axis=1)
    nxt2 = jnp.min(jnp.where(experts[None, :] == nxt[:, None], nxt[None, :], none), axis=1)
    ahead_e = jnp.where(nxt2 < none, nxt2, -1)
    head0 = jnp.min(jnp.where(present, experts, none))
    head1 = jnp.min(jnp.where(experts == head0, nxt, none))
    head = jnp.stack([head0, jnp.where(head1 < none, head1, -1)])

    w = jnp.arange(n_items, dtype=I32)
    valid = w < total
    wc = jnp.minimum(w, total - 1)
    e_w = jnp.sum((item_end[None, :] <= wc[:, None]).astype(I32), axis=1)
    table = jnp.stack([item_start, starts, seg_ends, ordinal, ahead_e], axis=1)
    mine = e_w[:, None] == experts[None, :]
    got = jnp.sum(jnp.where(mine[:, :, None], table[None, :, :], 0), axis=1)
    want = got[:, 1] + (wc - got[:, 0]) * tm
    start = jnp.minimum(want, n_rows - tm)
    lo = want - start
    rows = jnp.where(valid, jnp.minimum(got[:, 2] - want, tm), 0)
    prev_e = jnp.concatenate([jnp.full((1,), -1, I32), e_w[:-1]])
    new_e = valid & (e_w != prev_e)
    plan = (start, e_w, lo, rows, new_e, valid, got[:, 3] % 2, got[:, 4], head)
    return dest.astype(I32), tuple(p.astype(I32) for p in plan), fill.astype(I32)


def _scatter_kernel(dest_ref, fill_ref, h_ref, g_ref, xs_ref, t_ref, sem):
    tm = h_ref.shape[0]
    i = pl.program_id(0)
    steps = pl.num_programs(0)
    n = steps * tm
    buf = i % 2
    groups = tm // SUBLANES

    def row_copy(step, b, grp, j, pick):
        d = dest_ref[pick * n + step * tm + grp * SUBLANES + j]
        return pltpu.make_async_copy(t_ref.at[b, grp, pl.ds(j, 1), :], xs_ref.at[pl.ds(d, 1), :], sem.at[b])

    def for_group(step, b, grp, fn):
        for j in range(SUBLANES):
            fn(row_copy(step, b, grp, j, 0), 0)
            fn(row_copy(step, b, grp, j, 1), 1)

    def for_rows(step, b, fn):
        def body(grp, carry):
            for_group(step, b, grp, fn)
            return carry
        lax.fori_loop(0, groups, body, 0)

    @pl.when(i >= 2)
    def _():
        for_rows(i - 2, buf, lambda copy, pick: copy.wait())

    per = ROW_CHUNK // SUBLANES
    chunks = groups // per

    def norm(c):
        rows = pl.ds(pl.multiple_of(c * ROW_CHUNK, ROW_CHUNK), ROW_CHUNK)
        t = _rms(h_ref[rows, :], g_ref[...])
        t_ref[buf, pl.ds(pl.multiple_of(c * per, per), per)] = t.reshape(per, SUBLANES, t.shape[-1])

    def start(c):
        for k in range(per):
            for_group(i, buf, c * per + k, lambda copy, pick: copy.start(priority=pick))

    norm(0)

    def body(c, carry):
        start(c)
        norm(c + 1)
        return carry
    lax.fori_loop(0, chunks - 1, body, 0)
    start(chunks - 1)

    @pl.when(i == steps - 1)
    def _():
        def fill_copy(k):
            return pltpu.make_async_copy(t_ref.at[buf, 0, pl.ds(0, 1), :],
                                         xs_ref.at[pl.ds(fill_ref[k], 1), :], sem.at[2])

        def fill_start(k, carry):
            fill_copy(k).start()
            return carry
        lax.fori_loop(0, fill_ref.shape[0], fill_start, 0)

        def fill_wait(k, carry):
            fill_copy(k).wait()
            return carry
        lax.fori_loop(0, fill_ref.shape[0], fill_wait, 0)

        @pl.when(i >= 1)
        def _():
            for_rows(i - 1, 1 - buf, lambda copy, pick: copy.wait())
        for_rows(i, buf, lambda copy, pick: copy.wait())


def _scatter(h, g, dest, fill, n_rows, tm=256):
    n, d = h.shape
    return pl.pallas_call(
        _scatter_kernel,
        grid_spec=pltpu.PrefetchScalarGridSpec(
            num_scalar_prefetch=2,
            grid=(n // tm,),
            in_specs=[pl.BlockSpec((tm, d), lambda i, dest, fill: (i, 0)),
                      pl.BlockSpec((1, d), lambda i, dest, fill: (0, 0))],
            out_specs=pl.BlockSpec(memory_space=pl.ANY),
            scratch_shapes=[pltpu.VMEM((2, tm // SUBLANES, SUBLANES, d), F32),
                            pltpu.SemaphoreType.DMA((3,))],
        ),
        out_shape=jax.ShapeDtypeStruct((n_rows, d), F32),
        compiler_params=_params("arbitrary"),
        name="moe_scatter",
    )(dest, fill, h, g.reshape(1, d))


def _experts_kernel(start_ref, exp_ref, lo_ref, rows_ref, new_ref, valid_ref, slot_ref, ahead_ref, head_ref,
                    xs_hbm, wg_hbm, wu_hbm, wd_hbm, ys_hbm, x_ref, y_ref,
                    wg_st, wu_st, wd_st, wg_bf, wu_bf, wd_bf, sem_w, sem_x, sem_y, *, layer):
    w = pl.program_id(0)
    n_items = pl.num_programs(0)
    tm = x_ref.shape[1]
    buf = w % 2
    chunk = 256

    def window(item, b, fn):
        @pl.when(valid_ref[item] == 1)
        def _():
            rows = pl.ds(pl.multiple_of(start_ref[item], SUBLANES), tm)
            fn(pltpu.make_async_copy(xs_hbm.at[rows, :], x_ref.at[b], sem_x.at[b]))
    _prefetch_tile(w, n_items, window, priority=0)

    def fetch(e, s):
        return (pltpu.make_async_copy(wg_hbm.at[layer, e], wg_st.at[s], sem_w.at[s, 0]),
                pltpu.make_async_copy(wu_hbm.at[layer, e], wu_st.at[s], sem_w.at[s, 1]),
                pltpu.make_async_copy(wd_hbm.at[layer, e], wd_st.at[s], sem_w.at[s, 2]))

    @pl.when(w == 0)
    def _():
        for copy in fetch(head_ref[0], 0):
            copy.start(priority=BULK_DMA_PRIORITY)

        @pl.when(head_ref[1] >= 0)
        def _():
            for copy in fetch(head_ref[1], 1):
                copy.start(priority=BULK_DMA_PRIORITY)

    @pl.when(new_ref[w] == 1)
    def _():
        s = slot_ref[w]
        for copy in fetch(exp_ref[w], s):
            copy.wait()

        def cast_in(r0):
            rows = pl.ds(r0, chunk)
            wg_bf[rows, :] = wg_st[s, rows, :].astype(BF16)
            wu_bf[rows, :] = wu_st[s, rows, :].astype(BF16)
        _row_loop(wg_bf.shape[0], cast_in, chunk)

        def cast_down(r0):
            rows = pl.ds(r0, chunk)
            wd_bf[rows, :] = wd_st[s, rows, :].astype(BF16)
        _row_loop(wd_bf.shape[0], cast_down, chunk)

        @pl.when(ahead_ref[w] >= 0)
        def _():
            for copy in fetch(ahead_ref[w], s):
                copy.start(priority=BULK_DMA_PRIORITY)

    def put(item, b, fn):
        rows, lo, start = rows_ref[item], lo_ref[item], start_ref[item]
        size = tm
        while size >= SUBLANES:
            off = lo + (rows & ~(2 * size - 1))

            @pl.when((rows & size) != 0)
            def _(size=size, off=off):
                src = y_ref.at[b, pl.ds(pl.multiple_of(off, SUBLANES), size), :]
                dst = ys_hbm.at[pl.ds(pl.multiple_of(start + off, SUBLANES), size), :]
                fn(pltpu.make_async_copy(src, dst, sem_y.at[b]))
            size //= 2

    @pl.when(w >= 2)
    def _():
        put(w - 2, buf, lambda copy: copy.wait())

    @pl.when(valid_ref[w] == 1)
    def _():
        x = x_ref[buf].astype(BF16)
        hg = _bdot(x, wg_bf[...])
        hu = _bdot(x, wu_bf[...])
        act = (hg * jax.nn.sigmoid(hg) * hu).astype(BF16)
        y_ref[buf] = _bdot(act, wd_bf[...])
        put(w, buf, lambda copy: copy.start())

    @pl.when(w == n_items - 1)
    def _():
        @pl.when(w >= 1)
        def _():
            put(w - 1, 1 - buf, lambda copy: copy.wait())
        put(w, buf, lambda copy: copy.wait())


def _experts(xs, plan, w_gate, w_up, w_down, layer, n_rows, tm):
    d = xs.shape[1]
    f = w_gate.shape[3]
    n_items = plan[0].shape[0]
    any_space = pl.BlockSpec(memory_space=pl.ANY)
    return pl.pallas_call(
        functools.partial(_experts_kernel, layer=layer),
        grid_spec=pltpu.PrefetchScalarGridSpec(
            num_scalar_prefetch=len(plan),
            grid=(n_items,),
            in_specs=[any_space, any_space, any_space, any_space],
            out_specs=any_space,
            scratch_shapes=[
                pltpu.VMEM((2, tm, d), F32), pltpu.VMEM((2, tm, d), F32),
                pltpu.VMEM((2, d, f), F32), pltpu.VMEM((2, d, f), F32), pltpu.VMEM((2, f, d), F32),
                pltpu.VMEM((d, f), BF16), pltpu.VMEM((d, f), BF16), pltpu.VMEM((f, d), BF16),
                pltpu.SemaphoreType.DMA((2, 3)), pltpu.SemaphoreType.DMA((2,)), pltpu.SemaphoreType.DMA((2,)),
            ],
        ),
        out_shape=jax.ShapeDtypeStruct((n_rows, d), F32),
        compiler_params=_params("arbitrary"),
        name="moe_experts",
    )(*plan, xs, w_gate, w_up, w_down)


def _combine_kernel(dest_ref, h_ref, wts_ref, g_ref, ys_ref, o_ref, y_ref, sem, *, final_norm):
    tm = h_ref.shape[0]
    i = pl.program_id(0)
    steps = pl.num_programs(0)
    n = steps * tm
    buf = i % 2
    groups = tm // SUBLANES

    def row_copy(step, b, grp, j, pick):
        d = dest_ref[pick * n + step * tm + grp * SUBLANES + j]
        return pltpu.make_async_copy(ys_ref.at[pl.ds(d, 1), :], y_ref.at[b, pick, grp, pl.ds(j, 1), :],
                                     sem.at[b])

    def for_group(step, b, grp, fn):
        for j in range(SUBLANES):
            fn(row_copy(step, b, grp, j, 0), 0)
            fn(row_copy(step, b, grp, j, 1), 1)

    def start(copy, pick):
        copy.start(priority=pick)

    def for_rows(step, b, fn):
        def body(grp, carry):
            for_group(step, b, grp, fn)
            return carry
        lax.fori_loop(0, groups, body, 0)

    per = ROW_CHUNK // SUBLANES
    chunks = groups // per

    def mix(c):
        rows = pl.ds(pl.multiple_of(c * ROW_CHUNK, ROW_CHUNK), ROW_CHUNK)
        grps = pl.ds(pl.multiple_of(c * per, per), per)
        wts = wts_ref[rows, :]
        y0 = y_ref[buf, 0, grps].reshape(ROW_CHUNK, -1)
        y1 = y_ref[buf, 1, grps].reshape(ROW_CHUNK, -1)
        out = h_ref[rows, :] + wts[:, 0:1] * y0 + wts[:, 1:2] * y1
        if final_norm:
            out = _rms(out, g_ref[...])
        o_ref[rows, :] = out

    @pl.when(i == 0)
    def _():
        for_rows(0, 0, start)

    for_rows(i, buf, lambda copy, pick: copy.wait())

    @pl.when(i + 1 < steps)
    def _():
        span = 2

        def body(cc, carry):
            for k in range(span * per):
                for_group(i + 1, 1 - buf, cc * span * per + k, start)
            for k in range(span):
                mix(cc * span + k)
            return carry
        lax.fori_loop(0, chunks // span, body, 0)

    @pl.when(i + 1 == steps)
    def _():
        def body(c, carry):
            mix(c)
            return carry
        lax.fori_loop(0, chunks, body, 0, unroll=ROW_UNROLL)


def _combine(h, wts, ys, dest, g_final, final_norm, tm=256):
    n, d = h.shape
    return pl.pallas_call(
        functools.partial(_combine_kernel, final_norm=final_norm),
        grid_spec=pltpu.PrefetchScalarGridSpec(
            num_scalar_prefetch=1,
            grid=(n // tm,),
            in_specs=[pl.BlockSpec((tm, d), lambda i, dest: (i, 0)),
                      pl.BlockSpec((tm, LANES), lambda i, dest: (i, 0)),
                      pl.BlockSpec((1, d), lambda i, dest: (0, 0)),
                      pl.BlockSpec(memory_space=pl.ANY)],
            out_specs=pl.BlockSpec((tm, d), lambda i, dest: (i, 0)),
            scratch_shapes=[pltpu.VMEM((2, 2, tm // SUBLANES, SUBLANES, d), F32),
                            pltpu.SemaphoreType.DMA((2,))],
        ),
        out_shape=jax.ShapeDtypeStruct((n, d), F32),
        compiler_params=_params("arbitrary"),
        name="moe_combine",
    )(dest, h, wts, g_final.reshape(1, d), ys)


def _moe(h, g, w_group, b_group, w_expert, b_expert, w_gate, w_up, w_down, layer, g_final, final_norm,
         tm=256):
    n = h.shape[0]
    ids, wts = _router(h, g, w_group, b_group, w_expert, b_expert)
    n_rows = _sorted_rows(2 * n, tm)
    n_items = n_rows // tm + N_EXPERTS
    dest, plan, fill = _dispatch_plan(ids, tm, n_items)
    xs = _scatter(h, g, dest, fill, n_rows)
    ys = _experts(xs, plan, w_gate, w_up, w_down, layer, n_rows, tm)
    return _combine(h, wts, ys, dest, g_final, final_norm)


def _dot_t(a, b):
    return lax.dot_general(a, b, (((1,), (1,)), ((), ())), preferred_element_type=F32)


def _lane_slab(columns):
    lane = lax.broadcasted_iota(I32, (columns[0].shape[0], LANES), 1)
    slab = jnp.zeros(lane.shape, F32)
    for h, col in enumerate(columns):
        slab = jnp.where(lane == h, col, slab)
    return slab


def _attn_kernel(q_ref, kc_ref, vc_ref, kp_ref, vp_ref, o_ref, lse_ref, *, seg_blocks):
    qblocks = q_ref.shape[0] // BAND
    step = pl.program_id(0)
    banded = seg_blocks > 1
    assert not banded or seg_blocks % qblocks == 0
    n_keys = 2 * BAND if banded else BAND
    qi = lax.broadcasted_iota(I32, (BAND, n_keys), 0)
    kj = lax.broadcasted_iota(I32, (BAND, n_keys), 1)
    mask = (kj >= qi) & (kj <= qi + BAND) if banded else kj <= qi
    scale = HEAD_DIM ** -0.5
    neg = -jnp.inf
    heads = [slice(h * HEAD_DIM, (h + 1) * HEAD_DIM) for h in range(HEADS)]

    for sb in range(qblocks):
        rows = slice(sb * BAND, (sb + 1) * BAND)
        if not banded:
            scores = [_dot_t(q_ref[rows, c], kc_ref[rows, c]) for c in heads]
        elif sb > 0:
            krows = slice((sb - 1) * BAND, (sb + 1) * BAND)
            scores = [_dot_t(q_ref[rows, c], kc_ref[krows, c]) for c in heads]
        else:
            scores = [jnp.concatenate([_dot_t(q_ref[rows, c], kp_ref[:, c]),
                                       _dot_t(q_ref[rows, c], kc_ref[rows, c])], axis=1) for c in heads]
        scores = [jnp.where(mask, s * scale, neg) for s in scores]
        if banded and sb == 0:
            at_start = (step * qblocks) % seg_blocks == 0
            drop = jnp.where(kj < BAND, jnp.where(at_start, neg, 0.0), 0.0)
            scores = [s + drop for s in scores]
        ms = [jnp.max(s, axis=1, keepdims=True) for s in scores]
        ps = [jnp.exp(s - m) for s, m in zip(scores, ms)]
        ls = [jnp.sum(p, axis=1, keepdims=True) for p in ps]
        pb = [p.astype(BF16) for p in ps]
        if not banded:
            accs = [_bdot(p, vc_ref[rows, c]) for p, c in zip(pb, heads)]
        elif sb > 0:
            accs = [_bdot(p, vc_ref[krows, c]) for p, c in zip(pb, heads)]
        else:
            accs = [_bdot(p[:, :BAND], vp_ref[:, c]) + _bdot(p[:, BAND:], vc_ref[rows, c])
                    for p, c in zip(pb, heads)]
        for acc, l, c in zip(accs, ls, heads):
            o_ref[rows, c] = (acc / l).astype(o_ref.dtype)
        lse_ref[rows, :] = _lane_slab([m + jnp.log(l) for m, l in zip(ms, ls)])


def _attn(qkv, cols, seg_blocks, out_dtype, name, rows_per_step=512):
    n = qkv.shape[0]
    cq, ck, cv = cols
    qblocks = rows_per_step // BAND
    return pl.pallas_call(
        functools.partial(_attn_kernel, seg_blocks=seg_blocks),
        grid=(n // rows_per_step,),
        in_specs=[
            pl.BlockSpec((rows_per_step, ATTN_OUT), lambda i: (i, cq)),
            pl.BlockSpec((rows_per_step, ATTN_OUT), lambda i: (i, ck)),
            pl.BlockSpec((rows_per_step, ATTN_OUT), lambda i: (i, cv)),
            pl.BlockSpec((BAND, ATTN_OUT), lambda i: (jnp.maximum(i * qblocks - 1, 0), ck)),
            pl.BlockSpec((BAND, ATTN_OUT), lambda i: (jnp.maximum(i * qblocks - 1, 0), cv)),
        ],
        out_specs=[pl.BlockSpec((rows_per_step, ATTN_OUT), lambda i: (i, 0)),
                   pl.BlockSpec((rows_per_step, LANES), lambda i: (i, 0))],
        out_shape=[jax.ShapeDtypeStruct((n, ATTN_OUT), out_dtype), jax.ShapeDtypeStruct((n, LANES), F32)],
        compiler_params=_params("arbitrary"),
        name=name,
    )(qkv, qkv, qkv, qkv, qkv)


def _attn_skew_kernel(q_ref, k_ref, v_ref, o_ref, lse_ref):
    subs = q_ref.shape[0] // BAND
    keys = k_ref.shape[0]
    u = lax.broadcasted_iota(I32, (BAND, keys), 0)
    kc = lax.broadcasted_iota(I32, (BAND, keys), 1)
    base = subs * (u - (kc & (BAND - 1))) - lax.shift_right_logical(kc, BAND_LOG2)
    scale = HEAD_DIM ** -0.5
    neg = -jnp.inf
    heads = [slice(h * HEAD_DIM, (h + 1) * HEAD_DIM) for h in range(HEADS)]
    for s in range(subs):
        rows = slice(s * BAND, (s + 1) * BAND)
        delta = base + s
        mask = (delta >= 0) & (delta <= BAND)
        scores = [jnp.where(mask, _dot_t(q_ref[rows, c], k_ref[:, c]) * scale, neg) for c in heads]
        ms = [jnp.max(sc, axis=1, keepdims=True) for sc in scores]
        ps = [jnp.exp(sc - m) for sc, m in zip(scores, ms)]
        ls = [jnp.sum(p, axis=1, keepdims=True) for p in ps]
        accs = [_bdot(p.astype(BF16), v_ref[:, c]) for p, c in zip(ps, heads)]
        for acc, l, c in zip(accs, ls, heads):
            o_ref[rows, c] = (acc / l).astype(o_ref.dtype)
        lse_ref[rows, :] = _lane_slab([m + jnp.log(l) for m, l in zip(ms, ls)])


def _attn_skew(qkv, cols, out_dtype, name):
    n = qkv.shape[0]
    cq, ck, cv = cols
    rows = (ATTN_DILATIONS[2] // ATTN_DILATIONS[1]) * BAND
    return pl.pallas_call(
        _attn_skew_kernel,
        grid=(n // rows,),
        in_specs=[pl.BlockSpec((rows, ATTN_OUT), lambda i: (i, cq)),
                  pl.BlockSpec((rows, ATTN_OUT), lambda i: (i, ck)),
                  pl.BlockSpec((rows, ATTN_OUT), lambda i: (i, cv))],
        out_specs=[pl.BlockSpec((rows, ATTN_OUT), lambda i: (i, 0)),
                   pl.BlockSpec((rows, LANES), lambda i: (i, 0))],
        out_shape=[jax.ShapeDtypeStruct((n, ATTN_OUT), out_dtype), jax.ShapeDtypeStruct((n, LANES), F32)],
        compiler_params=_params("arbitrary"),
        name=name,
    )(qkv, qkv, qkv)


def _slot_class(c):
    return (c % 4) * 4 + c // 4


def _qkv_kernel(h_hbm, gq_ref, gkv_ref, wq_ref, wkv_ref, o_ref, x_ref, xq_ref, xkv_ref, sem,
                *, tiles_per_batch, n_groups):
    i, j = pl.program_id(0), pl.program_id(1)
    tiles = pl.num_programs(0)
    tm = x_ref.shape[1]
    q_blocks = n_groups * ATTN_OUT // o_ref.shape[1]

    def fetch(tile, buf, fn):
        if n_groups == 1:
            fn(pltpu.make_async_copy(h_hbm.at[pl.ds(tile * tm, tm), :], x_ref.at[buf], sem.at[buf]))
            return
        per_class = h_hbm.shape[0] // (tiles // tiles_per_batch)
        slots = tm // per_class
        bi, ti = tile // tiles_per_batch, tile % tiles_per_batch
        for c in range(slots):
            cls = _slot_class(ti * slots + c)
            fn(pltpu.make_async_copy(h_hbm.at[pl.ds(bi * per_class, per_class), cls, :],
                                     x_ref.at[buf, pl.ds(c * per_class, per_class), :],
                                     sem.at[buf]))

    @pl.when(j == 0)
    def _():
        buf = i % 2
        _prefetch_tile(i, tiles, fetch)

        def norm(r0):
            x = x_ref[buf, pl.ds(r0, ROW_CHUNK), :]
            xhat = x * lax.rsqrt(jnp.mean(x * x, axis=-1, keepdims=True) + EPS)
            rows = pl.ds(r0, ROW_CHUNK)
            xq_ref[rows, :] = (xhat * gq_ref[...]).astype(BF16)
            xkv_ref[rows, :] = (xhat * gkv_ref[...]).astype(BF16)
        _row_loop(tm, norm)

    @pl.when(j < q_blocks)
    def _():
        o_ref[...] = _bdot(xq_ref[...], wq_ref[...].astype(BF16)).astype(o_ref.dtype)

    @pl.when(j >= q_blocks)
    def _():
        o_ref[...] = _bdot(xkv_ref[...], wkv_ref[...].astype(BF16)).astype(o_ref.dtype)


def _qkv(h, batch, g_q, g_kv, w_q, w_kv, dilated, tm=1024, tn=512):
    n, d = h.shape
    first, n_groups = (1, N_DILATED) if dilated else (0, 1)
    r2 = ATTN_DILATIONS[2]
    hv = h.reshape(n // r2, r2, d) if dilated else h
    per = ATTN_OUT // tn
    q_blocks = n_groups * per
    kv_blocks = 2 * n_groups * per

    def wq_map(i, j):
        return (0, first * per + jnp.minimum(j, q_blocks - 1))

    def wkv_map(i, j):
        jj = jnp.clip(j - q_blocks, 0, kv_blocks - 1)
        return (0, first * per + jj + jnp.where(jj >= q_blocks, (N_ATTN_GROUPS - n_groups) * per, 0))

    return pl.pallas_call(
        functools.partial(_qkv_kernel, tiles_per_batch=(n // batch) // tm, n_groups=n_groups),
        grid=(n // tm, q_blocks + kv_blocks),
        in_specs=[
            pl.BlockSpec(memory_space=pl.ANY),
            pl.BlockSpec((1, d), lambda i, j: (0, 0)),
            pl.BlockSpec((1, d), lambda i, j: (0, 0)),
            pl.BlockSpec((d, tn), wq_map),
            pl.BlockSpec((d, tn), wkv_map),
        ],
        out_specs=pl.BlockSpec((tm, tn), lambda i, j: (i, j)),
        out_shape=jax.ShapeDtypeStruct((n, 3 * n_groups * ATTN_OUT), BF16),
        scratch_shapes=[pltpu.VMEM((2, tm, d), F32), pltpu.VMEM((tm, d), BF16), pltpu.VMEM((tm, d), BF16),
                        pltpu.SemaphoreType.DMA((2,))],
        compiler_params=_params("arbitrary", "arbitrary"),
        name="qkv_dilated" if dilated else "qkv_g0",
    )(hv, g_q.reshape(1, d), g_kv.reshape(1, d), w_q, w_kv)


def _merge_out_kernel(o0_ref, l0_ref, h_ref, w_ref, o1_hbm, l1_hbm, o2_hbm, l2_hbm, out_ref,
                      ob_ref, lb_ref, m_ref, sem, *, tiles_per_batch):
    t = pl.program_id(0)
    tiles = pl.num_programs(0)
    buf = t % 2
    per = ob_ref.shape[2]
    n_cls = ob_ref.shape[3]
    rows = per * n_cls

    def fetch(tile, b, fn):
        bi, ti = tile // tiles_per_batch, tile % tiles_per_batch
        for c in range(n_cls):
            cls = _slot_class(c)
            src_rows = pl.ds(ti * per, per)
            for g, (o_hbm, l_hbm) in enumerate(((o1_hbm, l1_hbm), (o2_hbm, l2_hbm))):
                fn(pltpu.make_async_copy(o_hbm.at[bi * n_cls + c, src_rows, :],
                                         ob_ref.at[b, g, :, cls, :], sem.at[b, 0]))
                fn(pltpu.make_async_copy(l_hbm.at[bi * n_cls + c, src_rows, :],
                                         lb_ref.at[b, g, :, cls, :], sem.at[b, 1]))

    _prefetch_tile(t, tiles, fetch)

    head_of_col = lax.shift_right_logical(lax.broadcasted_iota(I32, (LANES, ATTN_OUT), 1), HEAD_DIM_LOG2)
    spread = jnp.where(head_of_col == lax.broadcasted_iota(I32, (LANES, ATTN_OUT), 0), 1.0, 0.0).astype(BF16)

    chunk = BAND
    for r0 in range(0, rows, chunk):
        rs = slice(r0, r0 + chunk)
        gs = slice(r0 // n_cls, (r0 + chunk) // n_cls)
        lses = (l0_ref[rs, :],
                lb_ref[buf, 0, gs, :, :].reshape(chunk, LANES),
                lb_ref[buf, 1, gs, :, :].reshape(chunk, LANES))
        o1 = ob_ref[buf, 0, gs, :, :].reshape(chunk, ATTN_OUT)
        o2 = ob_ref[buf, 1, gs, :, :].reshape(chunk, ATTN_OUT)
        mx = jnp.maximum(jnp.maximum(lses[0], lses[1]), lses[2])
        ex = [jnp.exp(l - mx) for l in lses]
        inv = 1.0 / (ex[0] + ex[1] + ex[2])
        wide = []
        for e in ex:
            hi, lo = _split_bf16(e * inv)
            wide.append(_bdot(hi, spread) + _bdot(lo, spread))
        merged = wide[0] * o0_ref[rs, :].astype(F32) + wide[1] * o1 + wide[2] * o2
        m_ref[rs, :] = merged.astype(BF16)
    out_ref[...] = h_ref[...] + _bdot(m_ref[...], w_ref[...])


def _merge_out(h, seq, g0, g1, g2, w_o, rows=512):
    n, d = h.shape
    n_cls = ATTN_DILATIONS[2]
    per = rows // n_cls
    cls_len = seq // n_cls
    (o0, l0), (o1, l1), (o2, l2) = g0, g1, g2

    def by_class(a):
        return a.reshape(n // cls_len, cls_len, a.shape[-1])

    return pl.pallas_call(
        functools.partial(_merge_out_kernel, tiles_per_batch=seq // rows),
        grid=(n // rows,),
        in_specs=[
            pl.BlockSpec((rows, ATTN_OUT), lambda t: (t, 0)),
            pl.BlockSpec((rows, LANES), lambda t: (t, 0)),
            pl.BlockSpec((rows, d), lambda t: (t, 0)),
            pl.BlockSpec((ATTN_OUT, d), lambda t: (0, 0)),
            pl.BlockSpec(memory_space=pl.ANY),
            pl.BlockSpec(memory_space=pl.ANY),
            pl.BlockSpec(memory_space=pl.ANY),
            pl.BlockSpec(memory_space=pl.ANY),
        ],
        out_specs=pl.BlockSpec((rows, d), lambda t: (t, 0)),
        out_shape=jax.ShapeDtypeStruct((n, d), F32),
        scratch_shapes=[pltpu.VMEM((2, N_DILATED, per, n_cls, ATTN_OUT), F32),
                        pltpu.VMEM((2, N_DILATED, per, n_cls, LANES), F32),
                        pltpu.VMEM((rows, ATTN_OUT), BF16),
                        pltpu.SemaphoreType.DMA((2, 2))],
        compiler_params=_params("arbitrary"),
        name="attn_merge_out",
    )(o0, l0, h, w_o.astype(BF16), by_class(o1), by_class(l1), by_class(o2), by_class(l2))


def kernel(x, norm_mix_g, norm_ffn_g, conv_w_in, conv_b_in, conv_w_dw, conv_b_dw, conv_ln_g, conv_ln_b,
           conv_w_out, conv_b_out, norm_kv_g, w_kv, attn_w_q, attn_w_o, router_w_group, router_b_group,
           router_w_expert, router_b_expert, expert_w_gate, expert_w_up, expert_w_down, norm_final_g):
    b, s, d = x.shape
    n = b * s
    xf = x.reshape(n, d)

    u = _glu_in(xf, norm_mix_g[0], conv_w_in[0], conv_b_in[0])
    y = _dwconv(u.reshape(b, s, -1), conv_w_dw[0], conv_b_dw[0])
    h = _ln_out(y.reshape(n, -1), conv_ln_g[0], conv_ln_b[0], conv_w_out[0], conv_b_out[0], xf)
    h = _moe(h, norm_ffn_g[0], router_w_group[0], router_b_group[0], router_w_expert[0],
             router_b_expert[0], expert_w_gate, expert_w_up, expert_w_down, 0,
             norm_final_g, False)

    qkv0 = _qkv(h, b, norm_mix_g[1], norm_kv_g, attn_w_q[0], w_kv, dilated=False)
    qkvd = _qkv(h, b, norm_mix_g[1], norm_kv_g, attn_w_q[0], w_kv, dilated=True)
    g0 = _attn(qkv0, (0, 1, 2), s // BAND, BF16, "attn_g0")
    g1 = _attn_skew(qkvd, (0, 2, 4), F32, "attn_g1")
    g2 = _attn(qkvd, (1, 3, 5), 1, F32, "attn_g2")
    h = _merge_out(h, s, g0, g1, g2, attn_w_o[0])
    out = _moe(h, norm_ffn_g[1], router_w_group[1], router_b_group[1], router_w_expert[1],
               router_b_expert[1], expert_w_gate, expert_w_up, expert_w_down, 1,
               norm_final_g, True)
    return out.reshape(b, s, d)
```

```python
import functools

import jax
import jax.numpy as jnp
from jax import lax
from jax.experimental import pallas as pl
from jax.experimental.pallas import tpu as pltpu

F32 = jnp.float32
BF16 = jnp.bfloat16
I32 = jnp.int32

EPS = 1e-6
LANES = 128
V7X_VMEM_BYTES = 64 * 1024 * 1024
VMEM_LIMIT = V7X_VMEM_BYTES - 8 * 1024 * 1024

CONV_WIDTH = 31
HALO = 32
ATTN_DILATIONS = (1, 4, 16)
N_ATTN_GROUPS = 3
N_DILATED = 2
HEADS = 8
HEAD_DIM = 128
HEAD_DIM_LOG2 = 7
BAND = 128
BAND_LOG2 = 7
ATTN_OUT = HEADS * HEAD_DIM
N_GROUPS = 4
EXPERTS_PER_GROUP = 8
N_EXPERTS = N_GROUPS * EXPERTS_PER_GROUP

ROW_CHUNK = 32
ROW_UNROLL = 4
SUBLANES = 8
BULK_DMA_PRIORITY = 1


def _params(*sem):
    return pltpu.CompilerParams(dimension_semantics=sem, vmem_limit_bytes=VMEM_LIMIT)


def _row_loop(n_rows, body, chunk=ROW_CHUNK, unroll=ROW_UNROLL):
    def step(c, carry):
        body(pl.multiple_of(c * chunk, chunk))
        return carry
    lax.fori_loop(0, n_rows // chunk, step, 0, unroll=unroll)


def _rms(x, g):
    ms = jnp.mean(x * x, axis=-1, keepdims=True)
    return x * lax.rsqrt(ms + EPS) * g


def _bdot(a, b):
    return jnp.dot(a, b, preferred_element_type=F32)


def _prefetch_tile(i, n_tiles, fetch, priority=BULK_DMA_PRIORITY):
    buf = i % 2

    def start(copy):
        copy.start(priority=priority)

    @pl.when(i == 0)
    def _():
        fetch(0, 0, start)

    @pl.when(i + 1 < n_tiles)
    def _():
        fetch(i + 1, 1 - buf, start)

    fetch(i, buf, lambda copy: copy.wait())


def _glu_in_kernel(x_hbm, g_ref, wv_ref, wg_ref, bv_ref, bg_ref, o_ref, x_ref, xn_ref, sem):
    tm = x_ref.shape[1]
    i = pl.program_id(0)

    @pl.when(pl.program_id(1) == 0)
    def _():
        def fetch(tile, buf, fn):
            fn(pltpu.make_async_copy(x_hbm.at[pl.ds(tile * tm, tm), :], x_ref.at[buf], sem.at[buf]))
        _prefetch_tile(i, pl.num_programs(0), fetch)

        def norm(r0):
            rows = pl.ds(r0, ROW_CHUNK)
            xn_ref[rows, :] = _rms(x_ref[i % 2, rows, :], g_ref[...]).astype(BF16)
        _row_loop(tm, norm)

    xn = xn_ref[...]
    val = _bdot(xn, wv_ref[...].astype(BF16)) + bv_ref[...]
    gate = _bdot(xn, wg_ref[...].astype(BF16)) + bg_ref[...]
    o_ref[...] = (val * jax.nn.sigmoid(gate)).astype(o_ref.dtype)


def _glu_in(x, g, w_in, b_in, tm=1024, tn=512):
    n, d = x.shape
    c = w_in.shape[1] // 2
    nj = c // tn
    b2 = b_in.reshape(1, 2 * c)
    return pl.pallas_call(
        _glu_in_kernel,
        grid=(n // tm, nj),
        in_specs=[
            pl.BlockSpec(memory_space=pl.ANY),
            pl.BlockSpec((1, d), lambda i, j: (0, 0)),
            pl.BlockSpec((d, tn), lambda i, j: (0, j)),
            pl.BlockSpec((d, tn), lambda i, j: (0, j + nj)),
            pl.BlockSpec((1, tn), lambda i, j: (0, j)),
            pl.BlockSpec((1, tn), lambda i, j: (0, j + nj)),
        ],
        out_specs=pl.BlockSpec((tm, tn), lambda i, j: (i, j)),
        out_shape=jax.ShapeDtypeStruct((n, c), BF16),
        scratch_shapes=[pltpu.VMEM((2, tm, d), F32), pltpu.VMEM((tm, d), BF16), pltpu.SemaphoreType.DMA((2,))],
        compiler_params=_params("arbitrary", "arbitrary"),
        name="glu_in",
    )(x, g.reshape(1, d), w_in, w_in, b2, b2)


def _dwconv_kernel(cur_ref, halo_ref, w_ref, b_ref, o_ref, buf_ref):
    ts, cw = cur_ref.shape[1], cur_ref.shape[2]
    rw = 64
    keep = jnp.where(pl.program_id(1) > 0, 1.0, 0.0)
    buf_ref[0, 0:HALO, :] = halo_ref[0].astype(F32) * keep
    buf_ref[0, HALO:, :] = cur_ref[0].astype(F32)
    shifted_rows = ts + HALO - SUBLANES
    for s in range(1, SUBLANES):
        buf_ref[s, 0:shifted_rows, :] = buf_ref[0, s:s + shifted_rows, :]
    first = HALO - (CONV_WIDTH - 1)
    for c0 in range(0, cw, LANES):
        cols = slice(c0, c0 + LANES)
        for r0 in range(0, ts, rw):
            acc = jnp.broadcast_to(b_ref[:, cols], (rw, LANES))
            for k in range(CONV_WIDTH):
                s = (first + k) % SUBLANES
                start = r0 + first + k - s
                acc = acc + w_ref[k:k + 1, cols] * buf_ref[s, start:start + rw, cols]
            o_ref[0, r0:r0 + rw, cols] = acc.astype(o_ref.dtype)


def _dwconv(u, w_dw, b_dw, ts=256, cw=512):
    b, s, c = u.shape
    hb = ts // HALO
    return pl.pallas_call(
        _dwconv_kernel,
        grid=(b, s // ts, c // cw),
        in_specs=[
            pl.BlockSpec((1, ts, cw), lambda bi, si, ci: (bi, si, ci)),
            pl.BlockSpec((1, HALO, cw), lambda bi, si, ci: (bi, jnp.maximum(si * hb - 1, 0), ci)),
            pl.BlockSpec((CONV_WIDTH, cw), lambda bi, si, ci: (0, ci)),
            pl.BlockSpec((1, cw), lambda bi, si, ci: (0, ci)),
        ],
        out_specs=pl.BlockSpec((1, ts, cw), lambda bi, si, ci: (bi, si, ci)),
        out_shape=jax.ShapeDtypeStruct((b, s, c), BF16),
        scratch_shapes=[pltpu.VMEM((SUBLANES, ts + HALO, cw), F32)],
        compiler_params=_params("arbitrary", "arbitrary", "arbitrary"),
        name="dwconv",
    )(u, u, w_dw, b_dw.reshape(1, c))


def _ln_out_kernel(y_hbm, lg_ref, lb_ref, w_ref, b_ref, res_ref, o_ref, y_ref, a_ref, sem):
    tm = y_ref.shape[1]
    i = pl.program_id(0)

    @pl.when(pl.program_id(1) == 0)
    def _():
        def fetch(tile, buf, fn):
            fn(pltpu.make_async_copy(y_hbm.at[pl.ds(tile * tm, tm), :], y_ref.at[buf], sem.at[buf]))
        _prefetch_tile(i, pl.num_programs(0), fetch)

        def norm(r0):
            rows = pl.ds(r0, ROW_CHUNK)
            y = y_ref[i % 2, rows, :].astype(F32)
            mu = jnp.mean(y, axis=-1, keepdims=True)
            yc = y - mu
            var = jnp.mean(yc * yc, axis=-1, keepdims=True)
            z = yc * lax.rsqrt(var + EPS) * lg_ref[...] + lb_ref[...]
            a_ref[rows, :] = (z * jax.nn.sigmoid(z)).astype(BF16)
        _row_loop(tm, norm)

    o_ref[...] = res_ref[...] + _bdot(a_ref[...], w_ref[...].astype(BF16)) + b_ref[...]


def _ln_out(y, ln_g, ln_b, w_out, b_out, res, tm=1024, tn=1024):
    n, c = y.shape
    d = w_out.shape[1]
    return pl.pallas_call(
        _ln_out_kernel,
        grid=(n // tm, d // tn),
        in_specs=[
            pl.BlockSpec(memory_space=pl.ANY),
            pl.BlockSpec((1, c), lambda i, j: (0, 0)),
            pl.BlockSpec((1, c), lambda i, j: (0, 0)),
            pl.BlockSpec((c, tn), lambda i, j: (0, j)),
            pl.BlockSpec((1, tn), lambda i, j: (0, j)),
            pl.BlockSpec((tm, tn), lambda i, j: (i, j)),
        ],
        out_specs=pl.BlockSpec((tm, tn), lambda i, j: (i, j)),
        out_shape=jax.ShapeDtypeStruct((n, d), F32),
        scratch_shapes=[pltpu.VMEM((2, tm, c), BF16), pltpu.VMEM((tm, c), BF16), pltpu.SemaphoreType.DMA((2,))],
        compiler_params=_params("arbitrary", "arbitrary"),
        name="ln_out",
    )(y, ln_g.reshape(1, c), ln_b.reshape(1, c), w_out, b_out.reshape(1, d), res)


def _split_bf16(a):
    hi = a.astype(BF16)
    lo = (a - hi.astype(F32)).astype(BF16)
    return hi, lo


def _router_kernel(h_ref, g_ref, w_ref, b_ref, ids_ref, wts_ref):
    t = _rms(h_ref[...], g_ref[...])
    t_hi, t_lo = _split_bf16(t)
    w_hi, w_lo = _split_bf16(w_ref[...])
    both = _bdot(t_hi, jnp.concatenate([w_hi, w_lo], axis=1))
    logits = both[:, :LANES] + both[:, LANES:] + _bdot(t_lo, w_hi) + b_ref[...]

    lane = lax.broadcasted_iota(I32, logits.shape, 1)
    lanef = lane.astype(F32)
    neg = -jnp.inf
    is_group = lane < N_GROUPS
    glog = jnp.where(is_group, logits, neg)
    gmax = jnp.max(glog, axis=1, keepdims=True)
    gi = jnp.min(jnp.where(glog == gmax, lanef, float(LANES)), axis=1, keepdims=True)
    gsum = jnp.sum(jnp.where(is_group, jnp.exp(logits - gmax), 0.0), axis=1, keepdims=True)
    gp = 1.0 / gsum

    lo_lane = N_GROUPS + gi * EXPERTS_PER_GROUP
    in_sel = (lanef >= lo_lane) & (lanef < lo_lane + EXPERTS_PER_GROUP)
    sel = jnp.where(in_sel, logits, neg)
    m1 = jnp.max(sel, axis=1, keepdims=True)
    i1 = jnp.min(jnp.where(sel == m1, lanef, float(LANES)), axis=1, keepdims=True)
    sel2 = jnp.where(lanef == i1, neg, sel)
    m2 = jnp.max(sel2, axis=1, keepdims=True)
    i2 = jnp.min(jnp.where(sel2 == m2, lanef, float(LANES)), axis=1, keepdims=True)
    e2 = jnp.exp(m2 - m1)
    w1 = gp / (1.0 + e2)
    w2 = gp * e2 / (1.0 + e2)

    ids = jnp.where(lane == 0, i1 - N_GROUPS, jnp.where(lane == 1, i2 - N_GROUPS, 0.0))
    ids_ref[...] = ids.astype(I32)
    wts_ref[...] = jnp.where(lane == 0, w1, jnp.where(lane == 1, w2, 0.0))


def _router(h, g, w_group, b_group, w_expert, b_expert, tm=512):
    n, d = h.shape
    w_e = jnp.transpose(w_expert, (1, 0, 2)).reshape(d, N_EXPERTS)
    pad = LANES - N_GROUPS - N_EXPERTS
    wr = jnp.concatenate([w_group, w_e, jnp.zeros((d, pad), F32)], axis=1)
    br = jnp.concatenate([b_group, b_expert.reshape(N_EXPERTS), jnp.zeros((pad,), F32)]).reshape(1, LANES)
    return pl.pallas_call(
        _router_kernel,
        grid=(n // tm,),
        in_specs=[
            pl.BlockSpec((tm, d), lambda i: (i, 0)),
            pl.BlockSpec((1, d), lambda i: (0, 0)),
            pl.BlockSpec((d, LANES), lambda i: (0, 0)),
            pl.BlockSpec((1, LANES), lambda i: (0, 0)),
        ],
        out_specs=[pl.BlockSpec((tm, LANES), lambda i: (i, 0)),
                   pl.BlockSpec((tm, LANES), lambda i: (i, 0))],
        out_shape=[jax.ShapeDtypeStruct((n, LANES), I32), jax.ShapeDtypeStruct((n, LANES), F32)],
        compiler_params=_params("arbitrary"),
        name="router",
    )(h, g.reshape(1, d), wr, br)


def _sorted_rows(n_assign, tm):
    return -(-(n_assign + N_EXPERTS * SUBLANES) // tm) * tm


def _dispatch_plan(ids, tm, n_items):
    none = N_EXPERTS
    experts = jnp.arange(N_EXPERTS, dtype=I32)
    upto = experts[:, None] <= experts[None, :]

    def prefix(v):
        return jnp.sum(jnp.where(upto, v[:, None], 0), axis=0)

    e_flat = jnp.concatenate([ids[:, 0], ids[:, 1]])
    onehot = (e_flat[:, None] == experts[None, :]).astype(I32)
    csum = jnp.cumsum(onehot, axis=0)
    counts = csum[-1]
    present = counts > 0
    n_rows = _sorted_rows(e_flat.shape[0], tm)
    aligned = (counts + SUBLANES - 1) // SUBLANES * SUBLANES
    last = jnp.max(jnp.where(present, experts, 0))
    padded = aligned + jnp.where(experts == last, n_rows - jnp.sum(aligned), 0)
    seg_ends = prefix(padded)
    starts = seg_ends - padded
    ends = starts + counts
    dest = jnp.sum(onehot * (csum - 1 + starts[None, :]), axis=1)

    gaps = padded - counts
    gap_ends = prefix(gaps)
    j = jnp.arange(n_rows - e_flat.shape[0], dtype=I32)
    gap_e = jnp.sum((gap_ends[None, :] <= j[:, None]).astype(I32), axis=1)
    first_gap = ends - (gap_ends - gaps)
    fill = j + jnp.sum(jnp.where(gap_e[:, None] == experts[None, :], first_gap[None, :], 0), axis=1)

    items = (padded + tm - 1) // tm
    item_end = prefix(items)
    item_start = item_end - items
    total = item_end[-1]

    ordinal = prefix(present.astype(I32)) - 1
    later = jnp.where(present[None, :] & (experts[None, :] > experts[:, None]), experts[None, :], none)
    nxt = jnp.min(later, axis=1)
    nxt2 = jnp.min(jnp.where(experts[None, :] == nxt[:, None], nxt[None, :], none), axis=1)
    ahead_e = jnp.where(nxt2 < none, nxt2, -1)
    head0 = jnp.min(jnp.where(present, experts, none))
    head1 = jnp.min(jnp.where(experts == head0, nxt, none))
    head = jnp.stack([head0, jnp.where(head1 < none, head1, -1)])

    w = jnp.arange(n_items, dtype=I32)
    valid = w < total
    wc = jnp.minimum(w, total - 1)
    e_w = jnp.sum((item_end[None, :] <= wc[:, None]).astype(I32), axis=1)
    table = jnp.stack([item_start, starts, seg_ends, ordinal, ahead_e], axis=1)
    mine = e_w[:, None] == experts[None, :]
    got = jnp.sum(jnp.where(mine[:, :, None], table[None, :, :], 0), axis=1)
    want = got[:, 1] + (wc - got[:, 0]) * tm
    start = jnp.minimum(want, n_rows - tm)
    lo = want - start
    rows = jnp.where(valid, jnp.minimum(got[:, 2] - want, tm), 0)
    prev_e = jnp.concatenate([jnp.full((1,), -1, I32), e_w[:-1]])
    new_e = valid & (e_w != prev_e)
    plan = (start, e_w, lo, rows, new_e, valid, got[:, 3] % 2, got[:, 4], head)
    return dest.astype(I32), tuple(p.astype(I32) for p in plan), fill.astype(I32)


def _scatter_kernel(dest_ref, fill_ref, h_ref, g_ref, xs_ref, t_ref, sem):
    tm = h_ref.shape[0]
    i = pl.program_id(0)
    steps = pl.num_programs(0)
    n = steps * tm
    buf = i % 2
    groups = tm // SUBLANES

    def row_copy(step, b, grp, j, pick):
        d = dest_ref[pick * n + step * tm + grp * SUBLANES + j]
        return pltpu.make_async_copy(t_ref.at[b, grp, pl.ds(j, 1), :], xs_ref.at[pl.ds(d, 1), :], sem.at[b])

    def for_group(step, b, grp, fn):
        for j in range(SUBLANES):
            fn(row_copy(step, b, grp, j, 0), 0)
            fn(row_copy(step, b, grp, j, 1), 1)

    def for_rows(step, b, fn):
        def body(grp, carry):
            for_group(step, b, grp, fn)
            return carry
        lax.fori_loop(0, groups, body, 0)

    @pl.when(i >= 2)
    def _():
        for_rows(i - 2, buf, lambda copy, pick: copy.wait())

    per = ROW_CHUNK // SUBLANES
    chunks = groups // per

    def norm(c):
        rows = pl.ds(pl.multiple_of(c * ROW_CHUNK, ROW_CHUNK), ROW_CHUNK)
        t = _rms(h_ref[rows, :], g_ref[...])
        t_ref[buf, pl.ds(pl.multiple_of(c * per, per), per)] = t.reshape(per, SUBLANES, t.shape[-1])

    def start(c):
        for k in range(per):
            for_group(i, buf, c * per + k, lambda copy, pick: copy.start(priority=pick))

    norm(0)

    def body(c, carry):
        start(c)
        norm(c + 1)
        return carry
    lax.fori_loop(0, chunks - 1, body, 0)
    start(chunks - 1)

    @pl.when(i == steps - 1)
    def _():
        def fill_copy(k):
            return pltpu.make_async_copy(t_ref.at[buf, 0, pl.ds(0, 1), :],
                                         xs_ref.at[pl.ds(fill_ref[k], 1), :], sem.at[2])

        def fill_start(k, carry):
            fill_copy(k).start()
            return carry
        lax.fori_loop(0, fill_ref.shape[0], fill_start, 0)

        def fill_wait(k, carry):
            fill_copy(k).wait()
            return carry
        lax.fori_loop(0, fill_ref.shape[0], fill_wait, 0)

        @pl.when(i >= 1)
        def _():
            for_rows(i - 1, 1 - buf, lambda copy, pick: copy.wait())
        for_rows(i, buf, lambda copy, pick: copy.wait())


def _scatter(h, g, dest, fill, n_rows, tm=256):
    n, d = h.shape
    return pl.pallas_call(
        _scatter_kernel,
        grid_spec=pltpu.PrefetchScalarGridSpec(
            num_scalar_prefetch=2,
            grid=(n // tm,),
            in_specs=[pl.BlockSpec((tm, d), lambda i, dest, fill: (i, 0)),
                      pl.BlockSpec((1, d), lambda i, dest, fill: (0, 0))],
            out_specs=pl.BlockSpec(memory_space=pl.ANY),
            scratch_shapes=[pltpu.VMEM((2, tm // SUBLANES, SUBLANES, d), F32),
                            pltpu.SemaphoreType.DMA((3,))],
        ),
        out_shape=jax.ShapeDtypeStruct((n_rows, d), F32),
        compiler_params=_params("arbitrary"),
        name="moe_scatter",
    )(dest, fill, h, g.reshape(1, d))


def _experts_kernel(start_ref, exp_ref, lo_ref, rows_ref, new_ref, valid_ref, slot_ref, ahead_ref, head_ref,
                    xs_hbm, wg_hbm, wu_hbm, wd_hbm, ys_hbm, x_ref, y_ref,
                    wg_st, wu_st, wd_st, wg_bf, wu_bf, wd_bf, sem_w, sem_x, sem_y, *, layer):
    w = pl.program_id(0)
    n_items = pl.num_programs(0)
    tm = x_ref.shape[1]
    buf = w % 2
    chunk = 256

    def window(item, b, fn):
        @pl.when(valid_ref[item] == 1)
        def _():
            rows = pl.ds(pl.multiple_of(start_ref[item], SUBLANES), tm)
            fn(pltpu.make_async_copy(xs_hbm.at[rows, :], x_ref.at[b], sem_x.at[b]))
    _prefetch_tile(w, n_items, window, priority=0)

    def fetch(e, s):
        return (pltpu.make_async_copy(wg_hbm.at[layer, e], wg_st.at[s], sem_w.at[s, 0]),
                pltpu.make_async_copy(wu_hbm.at[layer, e], wu_st.at[s], sem_w.at[s, 1]),
                pltpu.make_async_copy(wd_hbm.at[layer, e], wd_st.at[s], sem_w.at[s, 2]))

    @pl.when(w == 0)
    def _():
        for copy in fetch(head_ref[0], 0):
            copy.start(priority=BULK_DMA_PRIORITY)

        @pl.when(head_ref[1] >= 0)
        def _():
            for copy in fetch(head_ref[1], 1):
                copy.start(priority=BULK_DMA_PRIORITY)

    @pl.when(new_ref[w] == 1)
    def _():
        s = slot_ref[w]
        for copy in fetch(exp_ref[w], s):
            copy.wait()

        def cast_in(r0):
            rows = pl.ds(r0, chunk)
            wg_bf[rows, :] = wg_st[s, rows, :].astype(BF16)
            wu_bf[rows, :] = wu_st[s, rows, :].astype(BF16)
        _row_loop(wg_bf.shape[0], cast_in, chunk)

        def cast_down(r0):
            rows = pl.ds(r0, chunk)
            wd_bf[rows, :] = wd_st[s, rows, :].astype(BF16)
        _row_loop(wd_bf.shape[0], cast_down, chunk)

        @pl.when(ahead_ref[w] >= 0)
        def _():
            for copy in fetch(ahead_ref[w], s):
                copy.start(priority=BULK_DMA_PRIORITY)

    def put(item, b, fn):
        rows, lo, start = rows_ref[item], lo_ref[item], start_ref[item]
        size = tm
        while size >= SUBLANES:
            off = lo + (rows & ~(2 * size - 1))

            @pl.when((rows & size) != 0)
            def _(size=size, off=off):
                src = y_ref.at[b, pl.ds(pl.multiple_of(off, SUBLANES), size), :]
                dst = ys_hbm.at[pl.ds(pl.multiple_of(start + off, SUBLANES), size), :]
                fn(pltpu.make_async_copy(src, dst, sem_y.at[b]))
            size //= 2

    @pl.when(w >= 2)
    def _():
        put(w - 2, buf, lambda copy: copy.wait())

    @pl.when(valid_ref[w] == 1)
    def _():
        x = x_ref[buf].astype(BF16)
        hg = _bdot(x, wg_bf[...])
        hu = _bdot(x, wu_bf[...])
        act = (hg * jax.nn.sigmoid(hg) * hu).astype(BF16)
        y_ref[buf] = _bdot(act, wd_bf[...])
        put(w, buf, lambda copy: copy.start())

    @pl.when(w == n_items - 1)
    def _():
        @pl.when(w >= 1)
        def _():
            put(w - 1, 1 - buf, lambda copy: copy.wait())
        put(w, buf, lambda copy: copy.wait())


def _experts(xs, plan, w_gate, w_up, w_down, layer, n_rows, tm):
    d = xs.shape[1]
    f = w_gate.shape[3]
    n_items = plan[0].shape[0]
    any_space = pl.BlockSpec(memory_space=pl.ANY)
    return pl.pallas_call(
        functools.partial(_experts_kernel, layer=layer),
        grid_spec=pltpu.PrefetchScalarGridSpec(
            num_scalar_prefetch=len(plan),
            grid=(n_items,),
            in_specs=[any_space, any_space, any_space, any_space],
            out_specs=any_space,
            scratch_shapes=[
                pltpu.VMEM((2, tm, d), F32), pltpu.VMEM((2, tm, d), F32),
                pltpu.VMEM((2, d, f), F32), pltpu.VMEM((2, d, f), F32), pltpu.VMEM((2, f, d), F32),
                pltpu.VMEM((d, f), BF16), pltpu.VMEM((d, f), BF16), pltpu.VMEM((f, d), BF16),
                pltpu.SemaphoreType.DMA((2, 3)), pltpu.SemaphoreType.DMA((2,)), pltpu.SemaphoreType.DMA((2,)),
            ],
        ),
        out_shape=jax.ShapeDtypeStruct((n_rows, d), F32),
        compiler_params=_params("arbitrary"),
        name="moe_experts",
    )(*plan, xs, w_gate, w_up, w_down)


def _combine_kernel(dest_ref, h_ref, wts_ref, g_ref, ys_ref, o_ref, y_ref, sem, *, final_norm):
    tm = h_ref.shape[0]
    i = pl.program_id(0)
    steps = pl.num_programs(0)
    n = steps * tm
    buf = i % 2
    groups = tm // SUBLANES

    def row_copy(step, b, grp, j, pick):
        d = dest_ref[pick * n + step * tm + grp * SUBLANES + j]
        return pltpu.make_async_copy(ys_ref.at[pl.ds(d, 1), :], y_ref.at[b, pick, grp, pl.ds(j, 1), :],
                                     sem.at[b])

    def for_group(step, b, grp, fn):
        for j in range(SUBLANES):
            fn(row_copy(step, b, grp, j, 0))
            fn(row_copy(step, b, grp, j, 1))

    def start(copy):
        copy.start()

    def for_rows(step, b, fn):
        def body(grp, carry):
            for_group(step, b, grp, fn)
            return carry
        lax.fori_loop(0, groups, body, 0)

    per = ROW_CHUNK // SUBLANES
    chunks = groups // per

    def mix(c):
        rows = pl.ds(pl.multiple_of(c * ROW_CHUNK, ROW_CHUNK), ROW_CHUNK)
        grps = pl.ds(pl.multiple_of(c * per, per), per)
        wts = wts_ref[rows, :]
        y0 = y_ref[buf, 0, grps].reshape(ROW_CHUNK, -1)
        y1 = y_ref[buf, 1, grps].reshape(ROW_CHUNK, -1)
        out = h_ref[rows, :] + wts[:, 0:1] * y0 + wts[:, 1:2] * y1
        if final_norm:
            out = _rms(out, g_ref[...])
        o_ref[rows, :] = out

    @pl.when(i == 0)
    def _():
        for_rows(0, 0, start)

    for_rows(i, buf, lambda copy: copy.wait())

    @pl.when(i + 1 < steps)
    def _():
        span = 2

        def body(cc, carry):
            for k in range(span * per):
                for_group(i + 1, 1 - buf, cc * span * per + k, start)
            for k in range(span):
                mix(cc * span + k)
            return carry
        lax.fori_loop(0, chunks // span, body, 0)

    @pl.when(i + 1 == steps)
    def _():
        def body(c, carry):
            mix(c)
            return carry
        lax.fori_loop(0, chunks, body, 0, unroll=ROW_UNROLL)


def _combine(h, wts, ys, dest, g_final, final_norm, tm=256):
    n, d = h.shape
    return pl.pallas_call(
        functools.partial(_combine_kernel, final_norm=final_norm),
        grid_spec=pltpu.PrefetchScalarGridSpec(
            num_scalar_prefetch=1,
            grid=(n // tm,),
            in_specs=[pl.BlockSpec((tm, d), lambda i, dest: (i, 0)),
                      pl.BlockSpec((tm, LANES), lambda i, dest: (i, 0)),
                      pl.BlockSpec((1, d), lambda i, dest: (0, 0)),
                      pl.BlockSpec(memory_space=pl.ANY)],
            out_specs=pl.BlockSpec((tm, d), lambda i, dest: (i, 0)),
            scratch_shapes=[pltpu.VMEM((2, 2, tm // SUBLANES, SUBLANES, d), F32),
                            pltpu.SemaphoreType.DMA((2,))],
        ),
        out_shape=jax.ShapeDtypeStruct((n, d), F32),
        compiler_params=_params("arbitrary"),
        name="moe_combine",
    )(dest, h, wts, g_final.reshape(1, d), ys)


def _moe(h, g, w_group, b_group, w_expert, b_expert, w_gate, w_up, w_down, layer, g_final, final_norm,
         tm=256):
    n = h.shape[0]
    ids, wts = _router(h, g, w_group, b_group, w_expert, b_expert)
    n_rows = _sorted_rows(2 * n, tm)
    n_items = n_rows // tm + N_EXPERTS
    dest, plan, fill = _dispatch_plan(ids, tm, n_items)
    xs = _scatter(h, g, dest, fill, n_rows)
    ys = _experts(xs, plan, w_gate, w_up, w_down, layer, n_rows, tm)
    return _combine(h, wts, ys, dest, g_final, final_norm)


def _dot_t(a, b):
    return lax.dot_general(a, b, (((1,), (1,)), ((), ())), preferred_element_type=F32)


def _lane_slab(columns):
    lane = lax.broadcasted_iota(I32, (columns[0].shape[0], LANES), 1)
    slab = jnp.zeros(lane.shape, F32)
    for h, col in enumerate(columns):
        slab = jnp.where(lane == h, col, slab)
    return slab


def _attn_kernel(q_ref, kc_ref, vc_ref, kp_ref, vp_ref, o_ref, lse_ref, *, seg_blocks):
    qblocks = q_ref.shape[0] // BAND
    step = pl.program_id(0)
    banded = seg_blocks > 1
    assert not banded or seg_blocks % qblocks == 0
    n_keys = 2 * BAND if banded else BAND
    qi = lax.broadcasted_iota(I32, (BAND, n_keys), 0)
    kj = lax.broadcasted_iota(I32, (BAND, n_keys), 1)
    mask = (kj >= qi) & (kj <= qi + BAND) if banded else kj <= qi
    scale = HEAD_DIM ** -0.5
    neg = -jnp.inf
    heads = [slice(h * HEAD_DIM, (h + 1) * HEAD_DIM) for h in range(HEADS)]

    for sb in range(qblocks):
        rows = slice(sb * BAND, (sb + 1) * BAND)
        if not banded:
            scores = [_dot_t(q_ref[rows, c], kc_ref[rows, c]) for c in heads]
        elif sb > 0:
            krows = slice((sb - 1) * BAND, (sb + 1) * BAND)
            scores = [_dot_t(q_ref[rows, c], kc_ref[krows, c]) for c in heads]
        else:
            scores = [jnp.concatenate([_dot_t(q_ref[rows, c], kp_ref[:, c]),
                                       _dot_t(q_ref[rows, c], kc_ref[rows, c])], axis=1) for c in heads]
        scores = [jnp.where(mask, s * scale, neg) for s in scores]
        if banded and sb == 0:
            at_start = (step * qblocks) % seg_blocks == 0
            drop = jnp.where(kj < BAND, jnp.where(at_start, neg, 0.0), 0.0)
            scores = [s + drop for s in scores]
        ms = [jnp.max(s, axis=1, keepdims=True) for s in scores]
        ps = [jnp.exp(s - m) for s, m in zip(scores, ms)]
        ls = [jnp.sum(p, axis=1, keepdims=True) for p in ps]
        pb = [p.astype(BF16) for p in ps]
        if not banded:
            accs = [_bdot(p, vc_ref[rows, c]) for p, c in zip(pb, heads)]
        elif sb > 0:
            accs = [_bdot(p, vc_ref[krows, c]) for p, c in zip(pb, heads)]
        else:
            accs = [_bdot(p[:, :BAND], vp_ref[:, c]) + _bdot(p[:, BAND:], vc_ref[rows, c])
                    for p, c in zip(pb, heads)]
        for acc, l, c in zip(accs, ls, heads):
            o_ref[rows, c] = (acc / l).astype(o_ref.dtype)
        lse_ref[rows, :] = _lane_slab([m + jnp.log(l) for m, l in zip(ms, ls)])


def _attn(qkv, cols, seg_blocks, out_dtype, name, rows_per_step=512):
    n = qkv.shape[0]
    cq, ck, cv = cols
    qblocks = rows_per_step // BAND
    return pl.pallas_call(
        functools.partial(_attn_kernel, seg_blocks=seg_blocks),
        grid=(n // rows_per_step,),
        in_specs=[
            pl.BlockSpec((rows_per_step, ATTN_OUT), lambda i: (i, cq)),
            pl.BlockSpec((rows_per_step, ATTN_OUT), lambda i: (i, ck)),
            pl.BlockSpec((rows_per_step, ATTN_OUT), lambda i: (i, cv)),
            pl.BlockSpec((BAND, ATTN_OUT), lambda i: (jnp.maximum(i * qblocks - 1, 0), ck)),
            pl.BlockSpec((BAND, ATTN_OUT), lambda i: (jnp.maximum(i * qblocks - 1, 0), cv)),
        ],
        out_specs=[pl.BlockSpec((rows_per_step, ATTN_OUT), lambda i: (i, 0)),
                   pl.BlockSpec((rows_per_step, LANES), lambda i: (i, 0))],
        out_shape=[jax.ShapeDtypeStruct((n, ATTN_OUT), out_dtype), jax.ShapeDtypeStruct((n, LANES), F32)],
        compiler_params=_params("arbitrary"),
        name=name,
    )(qkv, qkv, qkv, qkv, qkv)


def _attn_skew_kernel(q_ref, k_ref, v_ref, o_ref, lse_ref):
    subs = q_ref.shape[0] // BAND
    keys = k_ref.shape[0]
    u = lax.broadcasted_iota(I32, (BAND, keys), 0)
    kc = lax.broadcasted_iota(I32, (BAND, keys), 1)
    base = subs * (u - (kc & (BAND - 1))) - lax.shift_right_logical(kc, BAND_LOG2)
    scale = HEAD_DIM ** -0.5
    neg = -jnp.inf
    heads = [slice(h * HEAD_DIM, (h + 1) * HEAD_DIM) for h in range(HEADS)]
    for s in range(subs):
        rows = slice(s * BAND, (s + 1) * BAND)
        delta = base + s
        mask = (delta >= 0) & (delta <= BAND)
        scores = [jnp.where(mask, _dot_t(q_ref[rows, c], k_ref[:, c]) * scale, neg) for c in heads]
        ms = [jnp.max(sc, axis=1, keepdims=True) for sc in scores]
        ps = [jnp.exp(sc - m) for sc, m in zip(scores, ms)]
        ls = [jnp.sum(p, axis=1, keepdims=True) for p in ps]
        accs = [_bdot(p.astype(BF16), v_ref[:, c]) for p, c in zip(ps, heads)]
        for acc, l, c in zip(accs, ls, heads):
            o_ref[rows, c] = (acc / l).astype(o_ref.dtype)
        lse_ref[rows, :] = _lane_slab([m + jnp.log(l) for m, l in zip(ms, ls)])


def _attn_skew(qkv, cols, out_dtype, name):
    n = qkv.shape[0]
    cq, ck, cv = cols
    rows = (ATTN_DILATIONS[2] // ATTN_DILATIONS[1]) * BAND
    return pl.pallas_call(
        _attn_skew_kernel,
        grid=(n // rows,),
        in_specs=[pl.BlockSpec((rows, ATTN_OUT), lambda i: (i, cq)),
                  pl.BlockSpec((rows, ATTN_OUT), lambda i: (i, ck)),
                  pl.BlockSpec((rows, ATTN_OUT), lambda i: (i, cv))],
        out_specs=[pl.BlockSpec((rows, ATTN_OUT), lambda i: (i, 0)),
                   pl.BlockSpec((rows, LANES), lambda i: (i, 0))],
        out_shape=[jax.ShapeDtypeStruct((n, ATTN_OUT), out_dtype), jax.ShapeDtypeStruct((n, LANES), F32)],
        compiler_params=_params("arbitrary"),
        name=name,
    )(qkv, qkv, qkv)


def _slot_class(c):
    return (c % 4) * 4 + c // 4


def _qkv_kernel(h_hbm, gq_ref, gkv_ref, wq_ref, wkv_ref, o_ref, x_ref, xq_ref, xkv_ref, sem,
                *, tiles_per_batch, n_groups):
    i, j = pl.program_id(0), pl.program_id(1)
    tiles = pl.num_programs(0)
    tm = x_ref.shape[1]
    q_blocks = n_groups * ATTN_OUT // o_ref.shape[1]

    def fetch(tile, buf, fn):
        if n_groups == 1:
            fn(pltpu.make_async_copy(h_hbm.at[pl.ds(tile * tm, tm), :], x_ref.at[buf], sem.at[buf]))
            return
        per_class = h_hbm.shape[0] // (tiles // tiles_per_batch)
        slots = tm // per_class
        bi, ti = tile // tiles_per_batch, tile % tiles_per_batch
        for c in range(slots):
            cls = _slot_class(ti * slots + c)
            fn(pltpu.make_async_copy(h_hbm.at[pl.ds(bi * per_class, per_class), cls, :],
                                     x_ref.at[buf, pl.ds(c * per_class, per_class), :],
                                     sem.at[buf]))

    @pl.when(j == 0)
    def _():
        buf = i % 2
        _prefetch_tile(i, tiles, fetch)

        def norm(r0):
            x = x_ref[buf, pl.ds(r0, ROW_CHUNK), :]
            xhat = x * lax.rsqrt(jnp.mean(x * x, axis=-1, keepdims=True) + EPS)
            rows = pl.ds(r0, ROW_CHUNK)
            xq_ref[rows, :] = (xhat * gq_ref[...]).astype(BF16)
            xkv_ref[rows, :] = (xhat * gkv_ref[...]).astype(BF16)
        _row_loop(tm, norm)

    @pl.when(j < q_blocks)
    def _():
        o_ref[...] = _bdot(xq_ref[...], wq_ref[...].astype(BF16)).astype(o_ref.dtype)

    @pl.when(j >= q_blocks)
    def _():
        o_ref[...] = _bdot(xkv_ref[...], wkv_ref[...].astype(BF16)).astype(o_ref.dtype)


def _qkv(h, batch, g_q, g_kv, w_q, w_kv, dilated, tm=1024, tn=512):
    n, d = h.shape
    first, n_groups = (1, N_DILATED) if dilated else (0, 1)
    r2 = ATTN_DILATIONS[2]
    hv = h.reshape(n // r2, r2, d) if dilated else h
    per = ATTN_OUT // tn
    q_blocks = n_groups * per
    kv_blocks = 2 * n_groups * per

    def wq_map(i, j):
        return (0, first * per + jnp.minimum(j, q_blocks - 1))

    def wkv_map(i, j):
        jj = jnp.clip(j - q_blocks, 0, kv_blocks - 1)
        return (0, first * per + jj + jnp.where(jj >= q_blocks, (N_ATTN_GROUPS - n_groups) * per, 0))

    return pl.pallas_call(
        functools.partial(_qkv_kernel, tiles_per_batch=(n // batch) // tm, n_groups=n_groups),
        grid=(n // tm, q_blocks + kv_blocks),
        in_specs=[
            pl.BlockSpec(memory_space=pl.ANY),
            pl.BlockSpec((1, d), lambda i, j: (0, 0)),
            pl.BlockSpec((1, d), lambda i, j: (0, 0)),
            pl.BlockSpec((d, tn), wq_map),
            pl.BlockSpec((d, tn), wkv_map),
        ],
        out_specs=pl.BlockSpec((tm, tn), lambda i, j: (i, j)),
        out_shape=jax.ShapeDtypeStruct((n, 3 * n_groups * ATTN_OUT), BF16),
        scratch_shapes=[pltpu.VMEM((2, tm, d), F32), pltpu.VMEM((tm, d), BF16), pltpu.VMEM((tm, d), BF16),
                        pltpu.SemaphoreType.DMA((2,))],
        compiler_params=_params("arbitrary", "arbitrary"),
        name="qkv_dilated" if dilated else "qkv_g0",
    )(hv, g_q.reshape(1, d), g_kv.reshape(1, d), w_q, w_kv)


def _merge_out_kernel(o0_ref, l0_ref, h_ref, w_ref, o1_hbm, l1_hbm, o2_hbm, l2_hbm, out_ref,
                      ob_ref, lb_ref, m_ref, sem, *, tiles_per_batch):
    t = pl.program_id(0)
    tiles = pl.num_programs(0)
    buf = t % 2
    per = ob_ref.shape[2]
    n_cls = ob_ref.shape[3]
    rows = per * n_cls

    def fetch(tile, b, fn):
        bi, ti = tile // tiles_per_batch, tile % tiles_per_batch
        for c in range(n_cls):
            cls = _slot_class(c)
            src_rows = pl.ds(ti * per, per)
            for g, (o_hbm, l_hbm) in enumerate(((o1_hbm, l1_hbm), (o2_hbm, l2_hbm))):
                fn(pltpu.make_async_copy(o_hbm.at[bi * n_cls + c, src_rows, :],
                                         ob_ref.at[b, g, :, cls, :], sem.at[b, 0]))
                fn(pltpu.make_async_copy(l_hbm.at[bi * n_cls + c, src_rows, :],
                                         lb_ref.at[b, g, :, cls, :], sem.at[b, 1]))

    _prefetch_tile(t, tiles, fetch)

    head_of_col = lax.shift_right_logical(lax.broadcasted_iota(I32, (LANES, ATTN_OUT), 1), HEAD_DIM_LOG2)
    spread = jnp.where(head_of_col == lax.broadcasted_iota(I32, (LANES, ATTN_OUT), 0), 1.0, 0.0).astype(BF16)

    chunk = BAND
    for r0 in range(0, rows, chunk):
        rs = slice(r0, r0 + chunk)
        gs = slice(r0 // n_cls, (r0 + chunk) // n_cls)
        lses = (l0_ref[rs, :],
                lb_ref[buf, 0, gs, :, :].reshape(chunk, LANES),
                lb_ref[buf, 1, gs, :, :].reshape(chunk, LANES))
        o1 = ob_ref[buf, 0, gs, :, :].reshape(chunk, ATTN_OUT)
        o2 = ob_ref[buf, 1, gs, :, :].reshape(chunk, ATTN_OUT)
        mx = jnp.maximum(jnp.maximum(lses[0], lses[1]), lses[2])
        ex = [jnp.exp(l - mx) for l in lses]
        inv = 1.0 / (ex[0] + ex[1] + ex[2])
        wide = []
        for e in ex:
            hi, lo = _split_bf16(e * inv)
            wide.append(_bdot(hi, spread) + _bdot(lo, spread))
        merged = wide[0] * o0_ref[rs, :].astype(F32) + wide[1] * o1 + wide[2] * o2
        m_ref[rs, :] = merged.astype(BF16)
    out_ref[...] = h_ref[...] + _bdot(m_ref[...], w_ref[...])


def _merge_out(h, seq, g0, g1, g2, w_o, rows=512):
    n, d = h.shape
    n_cls = ATTN_DILATIONS[2]
    per = rows // n_cls
    cls_len = seq // n_cls
    (o0, l0), (o1, l1), (o2, l2) = g0, g1, g2

    def by_class(a):
        return a.reshape(n // cls_len, cls_len, a.shape[-1])

    return pl.pallas_call(
        functools.partial(_merge_out_kernel, tiles_per_batch=seq // rows),
        grid=(n // rows,),
        in_specs=[
            pl.BlockSpec((rows, ATTN_OUT), lambda t: (t, 0)),
            pl.BlockSpec((rows, LANES), lambda t: (t, 0)),
            pl.BlockSpec((rows, d), lambda t: (t, 0)),
            pl.BlockSpec((ATTN_OUT, d), lambda t: (0, 0)),
            pl.BlockSpec(memory_space=pl.ANY),
            pl.BlockSpec(memory_space=pl.ANY),
            pl.BlockSpec(memory_space=pl.ANY),
            pl.BlockSpec(memory_space=pl.ANY),
        ],
        out_specs=pl.BlockSpec((rows, d), lambda t: (t, 0)),
        out_shape=jax.ShapeDtypeStruct((n, d), F32),
        scratch_shapes=[pltpu.VMEM((2, N_DILATED, per, n_cls, ATTN_OUT), F32),
                        pltpu.VMEM((2, N_DILATED, per, n_cls, LANES), F32),
                        pltpu.VMEM((rows, ATTN_OUT), BF16),
                        pltpu.SemaphoreType.DMA((2, 2))],
        compiler_params=_params("arbitrary"),
        name="attn_merge_out",
    )(o0, l0, h, w_o.astype(BF16), by_class(o1), by_class(l1), by_class(o2), by_class(l2))


def kernel(x, norm_mix_g, norm_ffn_g, conv_w_in, conv_b_in, conv_w_dw, conv_b_dw, conv_ln_g, conv_ln_b,
           conv_w_out, conv_b_out, norm_kv_g, w_kv, attn_w_q, attn_w_o, router_w_group, router_b_group,
           router_w_expert, router_b_expert, expert_w_gate, expert_w_up, expert_w_down, norm_final_g):
    b, s, d = x.shape
    n = b * s
    xf = x.reshape(n, d)

    u = _glu_in(xf, norm_mix_g[0], conv_w_in[0], conv_b_in[0])
    y = _dwconv(u.reshape(b, s, -1), conv_w_dw[0], conv_b_dw[0])
    h = _ln_out(y.reshape(n, -1), conv_ln_g[0], conv_ln_b[0], conv_w_out[0], conv_b_out[0], xf)
    h = _moe(h, norm_ffn_g[0], router_w_group[0], router_b_group[0], router_w_expert[0],
             router_b_expert[0], expert_w_gate, expert_w_up, expert_w_down, 0,
             norm_final_g, False)

    qkv0 = _qkv(h, b, norm_mix_g[1], norm_kv_g, attn_w_q[0], w_kv, dilated=False)
    qkvd = _qkv(h, b, norm_mix_g[1], norm_kv_g, attn_w_q[0], w_kv, dilated=True)
    g0 = _attn(qkv0, (0, 1, 2), s // BAND, BF16, "attn_g0")
    g1 = _attn_skew(qkvd, (0, 2, 4), F32, "attn_g1")
    g2 = _attn(qkvd, (1, 3, 5), 1, F32, "attn_g2")
    h = _merge_out(h, s, g0, g1, g2, attn_w_o[0])
    out = _moe(h, norm_ffn_g[1], router_w_group[1], router_b_group[1], router_w_expert[1],
               router_b_expert[1], expert_w_gate, expert_w_up, expert_w_down, 1,
               norm_final_g, True)
    return out.reshape(b, s, d)
```

```python
import functools

import jax
import jax.numpy as jnp
from jax import lax
from jax.experimental import pallas as pl
from jax.experimental.pallas import tpu as pltpu

F32 = jnp.float32
BF16 = jnp.bfloat16
I32 = jnp.int32

EPS = 1e-6
LANES = 128
V7X_VMEM_BYTES = 64 * 1024 * 1024
VMEM_LIMIT = V7X_VMEM_BYTES - 8 * 1024 * 1024

CONV_WIDTH = 31
HALO = 32
ATTN_DILATIONS = (1, 4, 16)
N_ATTN_GROUPS = 3
N_DILATED = 2
HEADS = 8
HEAD_DIM = 128
HEAD_DIM_LOG2 = 7
BAND = 128
BAND_LOG2 = 7
ATTN_OUT = HEADS * HEAD_DIM
N_GROUPS = 4
EXPERTS_PER_GROUP = 8
N_EXPERTS = N_GROUPS * EXPERTS_PER_GROUP

ROW_CHUNK = 32
ROW_UNROLL = 4
SUBLANES = 8
BULK_DMA_PRIORITY = 1


def _params(*sem):
    return pltpu.CompilerParams(dimension_semantics=sem, vmem_limit_bytes=VMEM_LIMIT)


def _row_loop(n_rows, body, chunk=ROW_CHUNK, unroll=ROW_UNROLL):
    def step(c, carry):
        body(pl.multiple_of(c * chunk, chunk))
        return carry
    lax.fori_loop(0, n_rows // chunk, step, 0, unroll=unroll)


def _rms(x, g):
    ms = jnp.mean(x * x, axis=-1, keepdims=True)
    return x * lax.rsqrt(ms + EPS) * g


def _bdot(a, b):
    return jnp.dot(a, b, preferred_element_type=F32)


def _prefetch_tile(i, n_tiles, fetch, priority=BULK_DMA_PRIORITY):
    buf = i % 2

    def start(copy):
        copy.start(priority=priority)

    @pl.when(i == 0)
    def _():
        fetch(0, 0, start)

    @pl.when(i + 1 < n_tiles)
    def _():
        fetch(i + 1, 1 - buf, start)

    fetch(i, buf, lambda copy: copy.wait())


def _glu_in_kernel(x_hbm, g_ref, wv_ref, wg_ref, bv_ref, bg_ref, o_ref, x_ref, xn_ref, sem):
    tm = x_ref.shape[1]
    i = pl.program_id(0)

    @pl.when(pl.program_id(1) == 0)
    def _():
        def fetch(tile, buf, fn):
            fn(pltpu.make_async_copy(x_hbm.at[pl.ds(tile * tm, tm), :], x_ref.at[buf], sem.at[buf]))
        _prefetch_tile(i, pl.num_programs(0), fetch)

        def norm(r0):
            rows = pl.ds(r0, ROW_CHUNK)
            xn_ref[rows, :] = _rms(x_ref[i % 2, rows, :], g_ref[...]).astype(BF16)
        _row_loop(tm, norm)

    xn = xn_ref[...]
    val = _bdot(xn, wv_ref[...].astype(BF16)) + bv_ref[...]
    gate = _bdot(xn, wg_ref[...].astype(BF16)) + bg_ref[...]
    o_ref[...] = (val * jax.nn.sigmoid(gate)).astype(o_ref.dtype)


def _glu_in(x, g, w_in, b_in, tm=1024, tn=512):
    n, d = x.shape
    c = w_in.shape[1] // 2
    nj = c // tn
    b2 = b_in.reshape(1, 2 * c)
    return pl.pallas_call(
        _glu_in_kernel,
        grid=(n // tm, nj),
        in_specs=[
            pl.BlockSpec(memory_space=pl.ANY),
            pl.BlockSpec((1, d), lambda i, j: (0, 0)),
            pl.BlockSpec((d, tn), lambda i, j: (0, j)),
            pl.BlockSpec((d, tn), lambda i, j: (0, j + nj)),
            pl.BlockSpec((1, tn), lambda i, j: (0, j)),
            pl.BlockSpec((1, tn), lambda i, j: (0, j + nj)),
        ],
        out_specs=pl.BlockSpec((tm, tn), lambda i, j: (i, j)),
        out_shape=jax.ShapeDtypeStruct((n, c), BF16),
        scratch_shapes=[pltpu.VMEM((2, tm, d), F32), pltpu.VMEM((tm, d), BF16), pltpu.SemaphoreType.DMA((2,))],
        compiler_params=_params("arbitrary", "arbitrary"),
        name="glu_in",
    )(x, g.reshape(1, d), w_in, w_in, b2, b2)


def _dwconv_kernel(cur_ref, halo_ref, w_ref, b_ref, o_ref, buf_ref):
    ts, cw = cur_ref.shape[1], cur_ref.shape[2]
    rw = 64
    keep = jnp.where(pl.program_id(1) > 0, 1.0, 0.0)
    buf_ref[0, 0:HALO, :] = halo_ref[0].astype(F32) * keep
    buf_ref[0, HALO:, :] = cur_ref[0].astype(F32)
    shifted_rows = ts + HALO - SUBLANES
    for s in range(1, SUBLANES):
        buf_ref[s, 0:shifted_rows, :] = buf_ref[0, s:s + shifted_rows, :]
    first = HALO - (CONV_WIDTH - 1)
    for c0 in range(0, cw, LANES):
        cols = slice(c0, c0 + LANES)
        for r0 in range(0, ts, rw):
            acc = jnp.broadcast_to(b_ref[:, cols], (rw, LANES))
            for k in range(CONV_WIDTH):
                s = (first + k) % SUBLANES
                start = r0 + first + k - s
                acc = acc + w_ref[k:k + 1, cols] * buf_ref[s, start:start + rw, cols]
            o_ref[0, r0:r0 + rw, cols] = acc.astype(o_ref.dtype)


def _dwconv(u, w_dw, b_dw, ts=256, cw=512):
    b, s, c = u.shape
    hb = ts // HALO
    return pl.pallas_call(
        _dwconv_kernel,
        grid=(b, s // ts, c // cw),
        in_specs=[
            pl.BlockSpec((1, ts, cw), lambda bi, si, ci: (bi, si, ci)),
            pl.BlockSpec((1, HALO, cw), lambda bi, si, ci: (bi, jnp.maximum(si * hb - 1, 0), ci)),
            pl.BlockSpec((CONV_WIDTH, cw), lambda bi, si, ci: (0, ci)),
            pl.BlockSpec((1, cw), lambda bi, si, ci: (0, ci)),
        ],
        out_specs=pl.BlockSpec((1, ts, cw), lambda bi, si, ci: (bi, si, ci)),
        out_shape=jax.ShapeDtypeStruct((b, s, c), BF16),
        scratch_shapes=[pltpu.VMEM((SUBLANES, ts + HALO, cw), F32)],
        compiler_params=_params("arbitrary", "arbitrary", "arbitrary"),
        name="dwconv",
    )(u, u, w_dw, b_dw.reshape(1, c))


def _ln_out_kernel(y_hbm, lg_ref, lb_ref, w_ref, b_ref, res_ref, o_ref, y_ref, a_ref, sem):
    tm = y_ref.shape[1]
    i = pl.program_id(0)

    @pl.when(pl.program_id(1) == 0)
    def _():
        def fetch(tile, buf, fn):
            fn(pltpu.make_async_copy(y_hbm.at[pl.ds(tile * tm, tm), :], y_ref.at[buf], sem.at[buf]))
        _prefetch_tile(i, pl.num_programs(0), fetch)

        def norm(r0):
            rows = pl.ds(r0, ROW_CHUNK)
            y = y_ref[i % 2, rows, :].astype(F32)
            mu = jnp.mean(y, axis=-1, keepdims=True)
            yc = y - mu
            var = jnp.mean(yc * yc, axis=-1, keepdims=True)
            z = yc * lax.rsqrt(var + EPS) * lg_ref[...] + lb_ref[...]
            a_ref[rows, :] = (z * jax.nn.sigmoid(z)).astype(BF16)
        _row_loop(tm, norm)

    o_ref[...] = res_ref[...] + _bdot(a_ref[...], w_ref[...].astype(BF16)) + b_ref[...]


def _ln_out(y, ln_g, ln_b, w_out, b_out, res, tm=1024, tn=1024):
    n, c = y.shape
    d = w_out.shape[1]
    return pl.pallas_call(
        _ln_out_kernel,
        grid=(n // tm, d // tn),
        in_specs=[
            pl.BlockSpec(memory_space=pl.ANY),
            pl.BlockSpec((1, c), lambda i, j: (0, 0)),
            pl.BlockSpec((1, c), lambda i, j: (0, 0)),
            pl.BlockSpec((c, tn), lambda i, j: (0, j)),
            pl.BlockSpec((1, tn), lambda i, j: (0, j)),
            pl.BlockSpec((tm, tn), lambda i, j: (i, j)),
        ],
        out_specs=pl.BlockSpec((tm, tn), lambda i, j: (i, j)),
        out_shape=jax.ShapeDtypeStruct((n, d), F32),
        scratch_shapes=[pltpu.VMEM((2, tm, c), BF16), pltpu.VMEM((tm, c), BF16), pltpu.SemaphoreType.DMA((2,))],
        compiler_params=_params("arbitrary", "arbitrary"),
        name="ln_out",
    )(y, ln_g.reshape(1, c), ln_b.reshape(1, c), w_out, b_out.reshape(1, d), res)


def _split_bf16(a):
    hi = a.astype(BF16)
    lo = (a - hi.astype(F32)).astype(BF16)
    return hi, lo


def _router_kernel(h_ref, g_ref, w_ref, b_ref, ids_ref, wts_ref):
    t = _rms(h_ref[...], g_ref[...])
    t_hi, t_lo = _split_bf16(t)
    w_hi, w_lo = _split_bf16(w_ref[...])
    both = _bdot(t_hi, jnp.concatenate([w_hi, w_lo], axis=1))
    logits = both[:, :LANES] + both[:, LANES:] + _bdot(t_lo, w_hi) + b_ref[...]

    lane = lax.broadcasted_iota(I32, logits.shape, 1)
    lanef = lane.astype(F32)
    neg = -jnp.inf
    is_group = lane < N_GROUPS
    glog = jnp.where(is_group, logits, neg)
    gmax = jnp.max(glog, axis=1, keepdims=True)
    gi = jnp.min(jnp.where(glog == gmax, lanef, float(LANES)), axis=1, keepdims=True)
    gsum = jnp.sum(jnp.where(is_group, jnp.exp(logits - gmax), 0.0), axis=1, keepdims=True)
    gp = 1.0 / gsum

    lo_lane = N_GROUPS + gi * EXPERTS_PER_GROUP
    in_sel = (lanef >= lo_lane) & (lanef < lo_lane + EXPERTS_PER_GROUP)
    sel = jnp.where(in_sel, logits, neg)
    m1 = jnp.max(sel, axis=1, keepdims=True)
    i1 = jnp.min(jnp.where(sel == m1, lanef, float(LANES)), axis=1, keepdims=True)
    sel2 = jnp.where(lanef == i1, neg, sel)
    m2 = jnp.max(sel2, axis=1, keepdims=True)
    i2 = jnp.min(jnp.where(sel2 == m2, lanef, float(LANES)), axis=1, keepdims=True)
    e2 = jnp.exp(m2 - m1)
    w1 = gp / (1.0 + e2)
    w2 = gp * e2 / (1.0 + e2)

    ids = jnp.where(lane == 0, i1 - N_GROUPS, jnp.where(lane == 1, i2 - N_GROUPS, 0.0))
    ids_ref[...] = ids.astype(I32)
    wts_ref[...] = jnp.where(lane == 0, w1, jnp.where(lane == 1, w2, 0.0))


def _router(h, g, w_group, b_group, w_expert, b_expert, tm=512):
    n, d = h.shape
    w_e = jnp.transpose(w_expert, (1, 0, 2)).reshape(d, N_EXPERTS)
    pad = LANES - N_GROUPS - N_EXPERTS
    wr = jnp.concatenate([w_group, w_e, jnp.zeros((d, pad), F32)], axis=1)
    br = jnp.concatenate([b_group, b_expert.reshape(N_EXPERTS), jnp.zeros((pad,), F32)]).reshape(1, LANES)
    return pl.pallas_call(
        _router_kernel,
        grid=(n // tm,),
        in_specs=[
            pl.BlockSpec((tm, d), lambda i: (i, 0)),
            pl.BlockSpec((1, d), lambda i: (0, 0)),
            pl.BlockSpec((d, LANES), lambda i: (0, 0)),
            pl.BlockSpec((1, LANES), lambda i: (0, 0)),
        ],
        out_specs=[pl.BlockSpec((tm, LANES), lambda i: (i, 0)),
                   pl.BlockSpec((tm, LANES), lambda i: (i, 0))],
        out_shape=[jax.ShapeDtypeStruct((n, LANES), I32), jax.ShapeDtypeStruct((n, LANES), F32)],
        compiler_params=_params("arbitrary"),
        name="router",
    )(h, g.reshape(1, d), wr, br)


def _sorted_rows(n_assign, tm):
    return -(-(n_assign + N_EXPERTS * SUBLANES) // tm) * tm


def _dispatch_plan(ids, tm, n_items):
    none = N_EXPERTS
    experts = jnp.arange(N_EXPERTS, dtype=I32)
    upto = experts[:, None] <= experts[None, :]

    def prefix(v):
        return jnp.sum(jnp.where(upto, v[:, None], 0), axis=0)

    e_flat = jnp.concatenate([ids[:, 0], ids[:, 1]])
    onehot = (e_flat[:, None] == experts[None, :]).astype(I32)
    csum = jnp.cumsum(onehot, axis=0)
    counts = csum[-1]
    present = counts > 0
    n_rows = _sorted_rows(e_flat.shape[0], tm)
    aligned = (counts + SUBLANES - 1) // SUBLANES * SUBLANES
    last = jnp.max(jnp.where(present, experts, 0))
    padded = aligned + jnp.where(experts == last, n_rows - jnp.sum(aligned), 0)
    seg_ends = prefix(padded)
    starts = seg_ends - padded
    ends = starts + counts
    dest = jnp.sum(onehot * (csum - 1 + starts[None, :]), axis=1)

    gaps = padded - counts
    gap_ends = prefix(gaps)
    j = jnp.arange(n_rows - e_flat.shape[0], dtype=I32)
    gap_e = jnp.sum((gap_ends[None, :] <= j[:, None]).astype(I32), axis=1)
    first_gap = ends - (gap_ends - gaps)
    fill = j + jnp.sum(jnp.where(gap_e[:, None] == experts[None, :], first_gap[None, :], 0), axis=1)

    items = (padded + tm - 1) // tm
    item_end = prefix(items)
    item_start = item_end - items
    total = item_end[-1]

    ordinal = prefix(present.astype(I32)) - 1
    later = jnp.where(present[None, :] & (experts[None, :] > experts[:, None]), experts[None, :], none)
    nxt = jnp.min(later, axis=1)
    nxt2 = jnp.min(jnp.where(experts[None, :] == nxt[:, None], nxt[None, :], none), axis=1)
    ahead_e = jnp.where(nxt2 < none, nxt2, -1)
    head0 = jnp.min(jnp.where(present, experts, none))
    head1 = jnp.min(jnp.where(experts == head0, nxt, none))
    head = jnp.stack([head0, jnp.where(head1 < none, head1, -1)])

    w = jnp.arange(n_items, dtype=I32)
    valid = w < total
    wc = jnp.minimum(w, total - 1)
    e_w = jnp.sum((item_end[None, :] <= wc[:, None]).astype(I32), axis=1)
    table = jnp.stack([item_start, starts, seg_ends, ordinal, ahead_e], axis=1)
    mine = e_w[:, None] == experts[None, :]
    got = jnp.sum(jnp.where(mine[:, :, None], table[None, :, :], 0), axis=1)
    want = got[:, 1] + (wc - got[:, 0]) * tm
    start = jnp.minimum(want, n_rows - tm)
    lo = want - start
    rows = jnp.where(valid, jnp.minimum(got[:, 2] - want, tm), 0)
    prev_e = jnp.concatenate([jnp.full((1,), -1, I32), e_w[:-1]])
    new_e = valid & (e_w != prev_e)
    plan = (start, e_w, lo, rows, new_e, valid, got[:, 3] % 2, got[:, 4], head)
    return dest.astype(I32), tuple(p.astype(I32) for p in plan), fill.astype(I32)


def _scatter_kernel(dest_ref, fill_ref, h_ref, g_ref, xs_ref, t_ref, sem):
    tm = h_ref.shape[0]
    i = pl.program_id(0)
    steps = pl.num_programs(0)
    n = steps * tm
    buf = i % 2
    groups = tm // SUBLANES

    def row_copy(step, b, grp, j, pick):
        d = dest_ref[pick * n + step * tm + grp * SUBLANES + j]
        return pltpu.make_async_copy(t_ref.at[b, grp, pl.ds(j, 1), :], xs_ref.at[pl.ds(d, 1), :], sem.at[b])

    def for_group(step, b, grp, fn):
        for j in range(SUBLANES):
            fn(row_copy(step, b, grp, j, 0), 0)
            fn(row_copy(step, b, grp, j, 1), 1)

    def for_rows(step, b, fn):
        def body(grp, carry):
            for_group(step, b, grp, fn)
            return carry
        lax.fori_loop(0, groups, body, 0)

    @pl.when(i >= 2)
    def _():
        for_rows(i - 2, buf, lambda copy, pick: copy.wait())

    per = ROW_CHUNK // SUBLANES
    chunks = groups // per

    def norm(c):
        rows = pl.ds(pl.multiple_of(c * ROW_CHUNK, ROW_CHUNK), ROW_CHUNK)
        t = _rms(h_ref[rows, :], g_ref[...])
        t_ref[buf, pl.ds(pl.multiple_of(c * per, per), per)] = t.reshape(per, SUBLANES, t.shape[-1])

    def start(c):
        for k in range(per):
            for_group(i, buf, c * per + k, lambda copy, pick: copy.start(priority=pick))

    norm(0)

    def body(c, carry):
        start(c)
        norm(c + 1)
        return carry
    lax.fori_loop(0, chunks - 1, body, 0)
    start(chunks - 1)

    @pl.when(i == steps - 1)
    def _():
        def fill_copy(k):
            return pltpu.make_async_copy(t_ref.at[buf, 0, pl.ds(0, 1), :],
                                         xs_ref.at[pl.ds(fill_ref[k], 1), :], sem.at[2])

        def fill_start(k, carry):
            fill_copy(k).start()
            return carry
        lax.fori_loop(0, fill_ref.shape[0], fill_start, 0)

        def fill_wait(k, carry):
            fill_copy(k).wait()
            return carry
        lax.fori_loop(0, fill_ref.shape[0], fill_wait, 0)

        @pl.when(i >= 1)
        def _():
            for_rows(i - 1, 1 - buf, lambda copy, pick: copy.wait())
        for_rows(i, buf, lambda copy, pick: copy.wait())


def _scatter(h, g, dest, fill, n_rows, tm=512):
    n, d = h.shape
    return pl.pallas_call(
        _scatter_kernel,
        grid_spec=pltpu.PrefetchScalarGridSpec(
            num_scalar_prefetch=2,
            grid=(n // tm,),
            in_specs=[pl.BlockSpec((tm, d), lambda i, dest, fill: (i, 0)),
                      pl.BlockSpec((1, d), lambda i, dest, fill: (0, 0))],
            out_specs=pl.BlockSpec(memory_space=pl.ANY),
            scratch_shapes=[pltpu.VMEM((2, tm // SUBLANES, SUBLANES, d), F32),
                            pltpu.SemaphoreType.DMA((3,))],
        ),
        out_shape=jax.ShapeDtypeStruct((n_rows, d), F32),
        compiler_params=_params("arbitrary"),
        name="moe_scatter",
    )(dest, fill, h, g.reshape(1, d))


def _experts_kernel(start_ref, exp_ref, lo_ref, rows_ref, new_ref, valid_ref, slot_ref, ahead_ref, head_ref,
                    xs_hbm, wg_hbm, wu_hbm, wd_hbm, ys_hbm, x_ref, y_ref,
                    wg_st, wu_st, wd_st, wg_bf, wu_bf, wd_bf, sem_w, sem_x, sem_y, *, layer):
    w = pl.program_id(0)
    n_items = pl.num_programs(0)
    tm = x_ref.shape[1]
    buf = w % 2
    chunk = 256

    def window(item, b, fn):
        @pl.when(valid_ref[item] == 1)
        def _():
            rows = pl.ds(pl.multiple_of(start_ref[item], SUBLANES), tm)
            fn(pltpu.make_async_copy(xs_hbm.at[rows, :], x_ref.at[b], sem_x.at[b]))
    _prefetch_tile(w, n_items, window, priority=0)

    def fetch(e, s):
        return (pltpu.make_async_copy(wg_hbm.at[layer, e], wg_st.at[s], sem_w.at[s, 0]),
                pltpu.make_async_copy(wu_hbm.at[layer, e], wu_st.at[s], sem_w.at[s, 1]),
                pltpu.make_async_copy(wd_hbm.at[layer, e], wd_st.at[s], sem_w.at[s, 2]))

    @pl.when(w == 0)
    def _():
        for copy in fetch(head_ref[0], 0):
            copy.start(priority=BULK_DMA_PRIORITY)

        @pl.when(head_ref[1] >= 0)
        def _():
            for copy in fetch(head_ref[1], 1):
                copy.start(priority=BULK_DMA_PRIORITY)

    @pl.when(new_ref[w] == 1)
    def _():
        s = slot_ref[w]
        for copy in fetch(exp_ref[w], s):
            copy.wait()

        def cast_in(r0):
            rows = pl.ds(r0, chunk)
            wg_bf[rows, :] = wg_st[s, rows, :].astype(BF16)
            wu_bf[rows, :] = wu_st[s, rows, :].astype(BF16)
        _row_loop(wg_bf.shape[0], cast_in, chunk)

        def cast_down(r0):
            rows = pl.ds(r0, chunk)
            wd_bf[rows, :] = wd_st[s, rows, :].astype(BF16)
        _row_loop(wd_bf.shape[0], cast_down, chunk)

        @pl.when(ahead_ref[w] >= 0)
        def _():
            for copy in fetch(ahead_ref[w], s):
                copy.start(priority=BULK_DMA_PRIORITY)

    def put(item, b, fn):
        rows, lo, start = rows_ref[item], lo_ref[item], start_ref[item]
        size = tm
        while size >= SUBLANES:
            off = lo + (rows & ~(2 * size - 1))

            @pl.when((rows & size) != 0)
            def _(size=size, off=off):
                src = y_ref.at[b, pl.ds(pl.multiple_of(off, SUBLANES), size), :]
                dst = ys_hbm.at[pl.ds(pl.multiple_of(start + off, SUBLANES), size), :]
                fn(pltpu.make_async_copy(src, dst, sem_y.at[b]))
            size //= 2

    @pl.when(w >= 2)
    def _():
        put(w - 2, buf, lambda copy: copy.wait())

    @pl.when(valid_ref[w] == 1)
    def _():
        x = x_ref[buf].astype(BF16)
        hg = _bdot(x, wg_bf[...])
        hu = _bdot(x, wu_bf[...])
        act = (hg * jax.nn.sigmoid(hg) * hu).astype(BF16)
        y_ref[buf] = _bdot(act, wd_bf[...])
        put(w, buf, lambda copy: copy.start())

    @pl.when(w == n_items - 1)
    def _():
        @pl.when(w >= 1)
        def _():
            put(w - 1, 1 - buf, lambda copy: copy.wait())
        put(w, buf, lambda copy: copy.wait())


def _experts(xs, plan, w_gate, w_up, w_down, layer, n_rows, tm):
    d = xs.shape[1]
    f = w_gate.shape[3]
    n_items = plan[0].shape[0]
    any_space = pl.BlockSpec(memory_space=pl.ANY)
    return pl.pallas_call(
        functools.partial(_experts_kernel, layer=layer),
        grid_spec=pltpu.PrefetchScalarGridSpec(
            num_scalar_prefetch=len(plan),
            grid=(n_items,),
            in_specs=[any_space, any_space, any_space, any_space],
            out_specs=any_space,
            scratch_shapes=[
                pltpu.VMEM((2, tm, d), F32), pltpu.VMEM((2, tm, d), F32),
                pltpu.VMEM((2, d, f), F32), pltpu.VMEM((2, d, f), F32), pltpu.VMEM((2, f, d), F32),
                pltpu.VMEM((d, f), BF16), pltpu.VMEM((d, f), BF16), pltpu.VMEM((f, d), BF16),
                pltpu.SemaphoreType.DMA((2, 3)), pltpu.SemaphoreType.DMA((2,)), pltpu.SemaphoreType.DMA((2,)),
            ],
        ),
        out_shape=jax.ShapeDtypeStruct((n_rows, d), F32),
        compiler_params=_params("arbitrary"),
        name="moe_experts",
    )(*plan, xs, w_gate, w_up, w_down)


def _combine_kernel(dest_ref, h_ref, wts_ref, g_ref, ys_ref, o_ref, y_ref, sem, *, final_norm):
    tm = h_ref.shape[0]
    i = pl.program_id(0)
    steps = pl.num_programs(0)
    n = steps * tm
    buf = i % 2
    groups = tm // SUBLANES

    def row_copy(step, b, grp, j, pick):
        d = dest_ref[pick * n + step * tm + grp * SUBLANES + j]
        return pltpu.make_async_copy(ys_ref.at[pl.ds(d, 1), :], y_ref.at[b, pick, grp, pl.ds(j, 1), :],
                                     sem.at[b])

    def for_group(step, b, grp, fn):
        for j in range(SUBLANES):
            fn(row_copy(step, b, grp, j, 0))
            fn(row_copy(step, b, grp, j, 1))

    def start(copy):
        copy.start()

    def for_rows(step, b, fn):
        def body(grp, carry):
            for_group(step, b, grp, fn)
            return carry
        lax.fori_loop(0, groups, body, 0)

    per = ROW_CHUNK // SUBLANES
    chunks = groups // per

    def mix(c):
        rows = pl.ds(pl.multiple_of(c * ROW_CHUNK, ROW_CHUNK), ROW_CHUNK)
        grps = pl.ds(pl.multiple_of(c * per, per), per)
        wts = wts_ref[rows, :]
        y0 = y_ref[buf, 0, grps].reshape(ROW_CHUNK, -1)
        y1 = y_ref[buf, 1, grps].reshape(ROW_CHUNK, -1)
        out = h_ref[rows, :] + wts[:, 0:1] * y0 + wts[:, 1:2] * y1
        if final_norm:
            out = _rms(out, g_ref[...])
        o_ref[rows, :] = out

    @pl.when(i == 0)
    def _():
        for_rows(0, 0, start)

    for_rows(i, buf, lambda copy: copy.wait())

    @pl.when(i + 1 < steps)
    def _():
        span = 2

        def body(cc, carry):
            for k in range(span * per):
                for_group(i + 1, 1 - buf, cc * span * per + k, start)
            for k in range(span):
                mix(cc * span + k)
            return carry
        lax.fori_loop(0, chunks // span, body, 0)

    @pl.when(i + 1 == steps)
    def _():
        def body(c, carry):
            mix(c)
            return carry
        lax.fori_loop(0, chunks, body, 0, unroll=ROW_UNROLL)


def _combine(h, wts, ys, dest, g_final, final_norm, tm=512):
    n, d = h.shape
    return pl.pallas_call(
        functools.partial(_combine_kernel, final_norm=final_norm),
        grid_spec=pltpu.PrefetchScalarGridSpec(
            num_scalar_prefetch=1,
            grid=(n // tm,),
            in_specs=[pl.BlockSpec((tm, d), lambda i, dest: (i, 0)),
                      pl.BlockSpec((tm, LANES), lambda i, dest: (i, 0)),
                      pl.BlockSpec((1, d), lambda i, dest: (0, 0)),
                      pl.BlockSpec(memory_space=pl.ANY)],
            out_specs=pl.BlockSpec((tm, d), lambda i, dest: (i, 0)),
            scratch_shapes=[pltpu.VMEM((2, 2, tm // SUBLANES, SUBLANES, d), F32),
                            pltpu.SemaphoreType.DMA((2,))],
        ),
        out_shape=jax.ShapeDtypeStruct((n, d), F32),
        compiler_params=_params("arbitrary"),
        name="moe_combine",
    )(dest, h, wts, g_final.reshape(1, d), ys)


def _moe(h, g, w_group, b_group, w_expert, b_expert, w_gate, w_up, w_down, layer, g_final, final_norm,
         tm=256):
    n = h.shape[0]
    ids, wts = _router(h, g, w_group, b_group, w_expert, b_expert)
    n_rows = _sorted_rows(2 * n, tm)
    n_items = n_rows // tm + N_EXPERTS
    dest, plan, fill = _dispatch_plan(ids, tm, n_items)
    xs = _scatter(h, g, dest, fill, n_rows)
    ys = _experts(xs, plan, w_gate, w_up, w_down, layer, n_rows, tm)
    return _combine(h, wts, ys, dest, g_final, final_norm)


def _dot_t(a, b):
    return lax.dot_general(a, b, (((1,), (1,)), ((), ())), preferred_element_type=F32)


def _lane_slab(columns):
    lane = lax.broadcasted_iota(I32, (columns[0].shape[0], LANES), 1)
    slab = jnp.zeros(lane.shape, F32)
    for h, col in enumerate(columns):
        slab = jnp.where(lane == h, col, slab)
    return slab


def _attn_kernel(q_ref, kc_ref, vc_ref, kp_ref, vp_ref, o_ref, lse_ref, *, seg_blocks):
    qblocks = q_ref.shape[0] // BAND
    step = pl.program_id(0)
    banded = seg_blocks > 1
    assert not banded or seg_blocks % qblocks == 0
    n_keys = 2 * BAND if banded else BAND
    qi = lax.broadcasted_iota(I32, (BAND, n_keys), 0)
    kj = lax.broadcasted_iota(I32, (BAND, n_keys), 1)
    mask = (kj >= qi) & (kj <= qi + BAND) if banded else kj <= qi
    scale = HEAD_DIM ** -0.5
    neg = -jnp.inf
    heads = [slice(h * HEAD_DIM, (h + 1) * HEAD_DIM) for h in range(HEADS)]

    for sb in range(qblocks):
        rows = slice(sb * BAND, (sb + 1) * BAND)
        if not banded:
            scores = [_dot_t(q_ref[rows, c], kc_ref[rows, c]) for c in heads]
        elif sb > 0:
            krows = slice((sb - 1) * BAND, (sb + 1) * BAND)
            scores = [_dot_t(q_ref[rows, c], kc_ref[krows, c]) for c in heads]
        else:
            scores = [jnp.concatenate([_dot_t(q_ref[rows, c], kp_ref[:, c]),
                                       _dot_t(q_ref[rows, c], kc_ref[rows, c])], axis=1) for c in heads]
        scores = [jnp.where(mask, s * scale, neg) for s in scores]
        if banded and sb == 0:
            at_start = (step * qblocks) % seg_blocks == 0
            drop = jnp.where(kj < BAND, jnp.where(at_start, neg, 0.0), 0.0)
            scores = [s + drop for s in scores]
        ms = [jnp.max(s, axis=1, keepdims=True) for s in scores]
        ps = [jnp.exp(s - m) for s, m in zip(scores, ms)]
        ls = [jnp.sum(p, axis=1, keepdims=True) for p in ps]
        pb = [p.astype(BF16) for p in ps]
        if not banded:
            accs = [_bdot(p, vc_ref[rows, c]) for p, c in zip(pb, heads)]
        elif sb > 0:
            accs = [_bdot(p, vc_ref[krows, c]) for p, c in zip(pb, heads)]
        else:
            accs = [_bdot(p[:, :BAND], vp_ref[:, c]) + _bdot(p[:, BAND:], vc_ref[rows, c])
                    for p, c in zip(pb, heads)]
        for acc, l, c in zip(accs, ls, heads):
            o_ref[rows, c] = (acc / l).astype(o_ref.dtype)
        lse_ref[rows, :] = _lane_slab([m + jnp.log(l) for m, l in zip(ms, ls)])


def _attn(qkv, cols, seg_blocks, out_dtype, name, rows_per_step=512):
    n = qkv.shape[0]
    cq, ck, cv = cols
    qblocks = rows_per_step // BAND
    return pl.pallas_call(
        functools.partial(_attn_kernel, seg_blocks=seg_blocks),
        grid=(n // rows_per_step,),
        in_specs=[
            pl.BlockSpec((rows_per_step, ATTN_OUT), lambda i: (i, cq)),
            pl.BlockSpec((rows_per_step, ATTN_OUT), lambda i: (i, ck)),
            pl.BlockSpec((rows_per_step, ATTN_OUT), lambda i: (i, cv)),
            pl.BlockSpec((BAND, ATTN_OUT), lambda i: (jnp.maximum(i * qblocks - 1, 0), ck)),
            pl.BlockSpec((BAND, ATTN_OUT), lambda i: (jnp.maximum(i * qblocks - 1, 0), cv)),
        ],
        out_specs=[pl.BlockSpec((rows_per_step, ATTN_OUT), lambda i: (i, 0)),
                   pl.BlockSpec((rows_per_step, LANES), lambda i: (i, 0))],
        out_shape=[jax.ShapeDtypeStruct((n, ATTN_OUT), out_dtype), jax.ShapeDtypeStruct((n, LANES), F32)],
        compiler_params=_params("arbitrary"),
        name=name,
    )(qkv, qkv, qkv, qkv, qkv)


def _attn_skew_kernel(q_ref, k_ref, v_ref, o_ref, lse_ref):
    subs = q_ref.shape[0] // BAND
    keys = k_ref.shape[0]
    u = lax.broadcasted_iota(I32, (BAND, keys), 0)
    kc = lax.broadcasted_iota(I32, (BAND, keys), 1)
    base = subs * (u - (kc & (BAND - 1))) - lax.shift_right_logical(kc, BAND_LOG2)
    scale = HEAD_DIM ** -0.5
    neg = -jnp.inf
    heads = [slice(h * HEAD_DIM, (h + 1) * HEAD_DIM) for h in range(HEADS)]
    for s in range(subs):
        rows = slice(s * BAND, (s + 1) * BAND)
        delta = base + s
        mask = (delta >= 0) & (delta <= BAND)
        scores = [jnp.where(mask, _dot_t(q_ref[rows, c], k_ref[:, c]) * scale, neg) for c in heads]
        ms = [jnp.max(sc, axis=1, keepdims=True) for sc in scores]
        ps = [jnp.exp(sc - m) for sc, m in zip(scores, ms)]
        ls = [jnp.sum(p, axis=1, keepdims=True) for p in ps]
        accs = [_bdot(p.astype(BF16), v_ref[:, c]) for p, c in zip(ps, heads)]
        for acc, l, c in zip(accs, ls, heads):
            o_ref[rows, c] = (acc / l).astype(o_ref.dtype)
        lse_ref[rows, :] = _lane_slab([m + jnp.log(l) for m, l in zip(ms, ls)])


def _attn_skew(qkv, cols, out_dtype, name):
    n = qkv.shape[0]
    cq, ck, cv = cols
    rows = (ATTN_DILATIONS[2] // ATTN_DILATIONS[1]) * BAND
    return pl.pallas_call(
        _attn_skew_kernel,
        grid=(n // rows,),
        in_specs=[pl.BlockSpec((rows, ATTN_OUT), lambda i: (i, cq)),
                  pl.BlockSpec((rows, ATTN_OUT), lambda i: (i, ck)),
                  pl.BlockSpec((rows, ATTN_OUT), lambda i: (i, cv))],
        out_specs=[pl.BlockSpec((rows, ATTN_OUT), lambda i: (i, 0)),
                   pl.BlockSpec((rows, LANES), lambda i: (i, 0))],
        out_shape=[jax.ShapeDtypeStruct((n, ATTN_OUT), out_dtype), jax.ShapeDtypeStruct((n, LANES), F32)],
        compiler_params=_params("arbitrary"),
        name=name,
    )(qkv, qkv, qkv)


def _slot_class(c):
    return (c % 4) * 4 + c // 4


def _qkv_kernel(h_hbm, gq_ref, gkv_ref, wq_ref, wkv_ref, o_ref, x_ref, xq_ref, xkv_ref, sem,
                *, tiles_per_batch, n_groups):
    i, j = pl.program_id(0), pl.program_id(1)
    tiles = pl.num_programs(0)
    tm = x_ref.shape[1]
    q_blocks = n_groups * ATTN_OUT // o_ref.shape[1]

    def fetch(tile, buf, fn):
        if n_groups == 1:
            fn(pltpu.make_async_copy(h_hbm.at[pl.ds(tile * tm, tm), :], x_ref.at[buf], sem.at[buf]))
            return
        per_class = h_hbm.shape[0] // (tiles // tiles_per_batch)
        slots = tm // per_class
        bi, ti = tile // tiles_per_batch, tile % tiles_per_batch
        for c in range(slots):
            cls = _slot_class(ti * slots + c)
            fn(pltpu.make_async_copy(h_hbm.at[pl.ds(bi * per_class, per_class), cls, :],
                                     x_ref.at[buf, pl.ds(c * per_class, per_class), :],
                                     sem.at[buf]))

    @pl.when(j == 0)
    def _():
        buf = i % 2
        _prefetch_tile(i, tiles, fetch)

        def norm(r0):
            x = x_ref[buf, pl.ds(r0, ROW_CHUNK), :]
            xhat = x * lax.rsqrt(jnp.mean(x * x, axis=-1, keepdims=True) + EPS)
            rows = pl.ds(r0, ROW_CHUNK)
            xq_ref[rows, :] = (xhat * gq_ref[...]).astype(BF16)
            xkv_ref[rows, :] = (xhat * gkv_ref[...]).astype(BF16)
        _row_loop(tm, norm)

    @pl.when(j < q_blocks)
    def _():
        o_ref[...] = _bdot(xq_ref[...], wq_ref[...].astype(BF16)).astype(o_ref.dtype)

    @pl.when(j >= q_blocks)
    def _():
        o_ref[...] = _bdot(xkv_ref[...], wkv_ref[...].astype(BF16)).astype(o_ref.dtype)


def _qkv(h, batch, g_q, g_kv, w_q, w_kv, dilated, tm=1024, tn=512):
    n, d = h.shape
    first, n_groups = (1, N_DILATED) if dilated else (0, 1)
    r2 = ATTN_DILATIONS[2]
    hv = h.reshape(n // r2, r2, d) if dilated else h
    per = ATTN_OUT // tn
    q_blocks = n_groups * per
    kv_blocks = 2 * n_groups * per

    def wq_map(i, j):
        return (0, first * per + jnp.minimum(j, q_blocks - 1))

    def wkv_map(i, j):
        jj = jnp.clip(j - q_blocks, 0, kv_blocks - 1)
        return (0, first * per + jj + jnp.where(jj >= q_blocks, (N_ATTN_GROUPS - n_groups) * per, 0))

    return pl.pallas_call(
        functools.partial(_qkv_kernel, tiles_per_batch=(n // batch) // tm, n_groups=n_groups),
        grid=(n // tm, q_blocks + kv_blocks),
        in_specs=[
            pl.BlockSpec(memory_space=pl.ANY),
            pl.BlockSpec((1, d), lambda i, j: (0, 0)),
            pl.BlockSpec((1, d), lambda i, j: (0, 0)),
            pl.BlockSpec((d, tn), wq_map),
            pl.BlockSpec((d, tn), wkv_map),
        ],
        out_specs=pl.BlockSpec((tm, tn), lambda i, j: (i, j)),
        out_shape=jax.ShapeDtypeStruct((n, 3 * n_groups * ATTN_OUT), BF16),
        scratch_shapes=[pltpu.VMEM((2, tm, d), F32), pltpu.VMEM((tm, d), BF16), pltpu.VMEM((tm, d), BF16),
                        pltpu.SemaphoreType.DMA((2,))],
        compiler_params=_params("arbitrary", "arbitrary"),
        name="qkv_dilated" if dilated else "qkv_g0",
    )(hv, g_q.reshape(1, d), g_kv.reshape(1, d), w_q, w_kv)


def _merge_out_kernel(o0_ref, l0_ref, h_ref, w_ref, o1_hbm, l1_hbm, o2_hbm, l2_hbm, out_ref,
                      ob_ref, lb_ref, m_ref, sem, *, tiles_per_batch):
    t = pl.program_id(0)
    tiles = pl.num_programs(0)
    buf = t % 2
    per = ob_ref.shape[2]
    n_cls = ob_ref.shape[3]
    rows = per * n_cls

    def fetch(tile, b, fn):
        bi, ti = tile // tiles_per_batch, tile % tiles_per_batch
        for c in range(n_cls):
            cls = _slot_class(c)
            src_rows = pl.ds(ti * per, per)
            for g, (o_hbm, l_hbm) in enumerate(((o1_hbm, l1_hbm), (o2_hbm, l2_hbm))):
                fn(pltpu.make_async_copy(o_hbm.at[bi * n_cls + c, src_rows, :],
                                         ob_ref.at[b, g, :, cls, :], sem.at[b, 0]))
                fn(pltpu.make_async_copy(l_hbm.at[bi * n_cls + c, src_rows, :],
                                         lb_ref.at[b, g, :, cls, :], sem.at[b, 1]))

    _prefetch_tile(t, tiles, fetch)

    head_of_col = lax.shift_right_logical(lax.broadcasted_iota(I32, (LANES, ATTN_OUT), 1), HEAD_DIM_LOG2)
    spread = jnp.where(head_of_col == lax.broadcasted_iota(I32, (LANES, ATTN_OUT), 0), 1.0, 0.0).astype(BF16)

    chunk = BAND
    for r0 in range(0, rows, chunk):
        rs = slice(r0, r0 + chunk)
        gs = slice(r0 // n_cls, (r0 + chunk) // n_cls)
        lses = (l0_ref[rs, :],
                lb_ref[buf, 0, gs, :, :].reshape(chunk, LANES),
                lb_ref[buf, 1, gs, :, :].reshape(chunk, LANES))
        o1 = ob_ref[buf, 0, gs, :, :].reshape(chunk, ATTN_OUT)
        o2 = ob_ref[buf, 1, gs, :, :].reshape(chunk, ATTN_OUT)
        mx = jnp.maximum(jnp.maximum(lses[0], lses[1]), lses[2])
        ex = [jnp.exp(l - mx) for l in lses]
        inv = 1.0 / (ex[0] + ex[1] + ex[2])
        wide = []
        for e in ex:
            hi, lo = _split_bf16(e * inv)
            wide.append(_bdot(hi, spread) + _bdot(lo, spread))
        merged = wide[0] * o0_ref[rs, :].astype(F32) + wide[1] * o1 + wide[2] * o2
        m_ref[rs, :] = merged.astype(BF16)
    out_ref[...] = h_ref[...] + _bdot(m_ref[...], w_ref[...])


def _merge_out(h, seq, g0, g1, g2, w_o, rows=512):
    n, d = h.shape
    n_cls = ATTN_DILATIONS[2]
    per = rows // n_cls
    cls_len = seq // n_cls
    (o0, l0), (o1, l1), (o2, l2) = g0, g1, g2

    def by_class(a):
        return a.reshape(n // cls_len, cls_len, a.shape[-1])

    return pl.pallas_call(
        functools.partial(_merge_out_kernel, tiles_per_batch=seq // rows),
        grid=(n // rows,),
        in_specs=[
            pl.BlockSpec((rows, ATTN_OUT), lambda t: (t, 0)),
            pl.BlockSpec((rows, LANES), lambda t: (t, 0)),
            pl.BlockSpec((rows, d), lambda t: (t, 0)),
            pl.BlockSpec((ATTN_OUT, d), lambda t: (0, 0)),
            pl.BlockSpec(memory_space=pl.ANY),
            pl.BlockSpec(memory_space=pl.ANY),
            pl.BlockSpec(memory_space=pl.ANY),
            pl.BlockSpec(memory_space=pl.ANY),
        ],
        out_specs=pl.BlockSpec((rows, d), lambda t: (t, 0)),
        out_shape=jax.ShapeDtypeStruct((n, d), F32),
        scratch_shapes=[pltpu.VMEM((2, N_DILATED, per, n_cls, ATTN_OUT), F32),
                        pltpu.VMEM((2, N_DILATED, per, n_cls, LANES), F32),
                        pltpu.VMEM((rows, ATTN_OUT), BF16),
                        pltpu.SemaphoreType.DMA((2, 2))],
        compiler_params=_params("arbitrary"),
        name="attn_merge_out",
    )(o0, l0, h, w_o.astype(BF16), by_class(o1), by_class(l1), by_class(o2), by_class(l2))


def kernel(x, norm_mix_g, norm_ffn_g, conv_w_in, conv_b_in, conv_w_dw, conv_b_dw, conv_ln_g, conv_ln_b,
           conv_w_out, conv_b_out, norm_kv_g, w_kv, attn_w_q, attn_w_o, router_w_group, router_b_group,
           router_w_expert, router_b_expert, expert_w_gate, expert_w_up, expert_w_down, norm_final_g):
    b, s, d = x.shape
    n = b * s
    xf = x.reshape(n, d)

    u = _glu_in(xf, norm_mix_g[0], conv_w_in[0], conv_b_in[0])
    y = _dwconv(u.reshape(b, s, -1), conv_w_dw[0], conv_b_dw[0])
    h = _ln_out(y.reshape(n, -1), conv_ln_g[0], conv_ln_b[0], conv_w_out[0], conv_b_out[0], xf)
    h = _moe(h, norm_ffn_g[0], router_w_group[0], router_b_group[0], router_w_expert[0],
             router_b_expert[0], expert_w_gate, expert_w_up, expert_w_down, 0,
             norm_final_g, False)

    qkv0 = _qkv(h, b, norm_mix_g[1], norm_kv_g, attn_w_q[0], w_kv, dilated=False)
    qkvd = _qkv(h, b, norm_mix_g[1], norm_kv_g, attn_w_q[0], w_kv, dilated=True)
    g0 = _attn(qkv0, (0, 1, 2), s // BAND, BF16, "attn_g0")
    g1 = _attn_skew(qkvd, (0, 2, 4), F32, "attn_g1")
    g2 = _attn(qkvd, (1, 3, 5), 1, F32, "attn_g2")
    h = _merge_out(h, s, g0, g1, g2, attn_w_o[0])
    out = _moe(h, norm_ffn_g[1], router_w_group[1], router_b_group[1], router_w_expert[1],
               router_b_expert[1], expert_w_gate, expert_w_up, expert_w_down, 1,
               norm_final_g, True)
    return out.reshape(b, s, d)
```
